```python
import jax, jax.numpy as jnp
from jax import lax
import numpy as np

D_MODEL = 1024
BATCH = 16
SEQ = 256
DEPTH = 4
DEC_BATCH = 8
DEC_SEQ = 2048
PAST_LEN = 512

GRID_W = 64
HEAD_DIM = 64
FNET_GROUPS = 8
FNET_GROUP_DIM = 64
FNET_WIDTH = FNET_GROUPS * FNET_GROUP_DIM
CONV_WIDTH = 512
CONV_K = 31
ATTN_HEADS = 8
ATTN_KV_HEADS = 2
ATTN_GROUP = ATTN_HEADS // ATTN_KV_HEADS
ATTN_Q = ATTN_HEADS * HEAD_DIM
ATTN_KV = ATTN_KV_HEADS * HEAD_DIM
WIN_HEADS = 8
WIN_KV_HEADS = 2
WIN_GROUP = WIN_HEADS // WIN_KV_HEADS
WIN_Q = WIN_HEADS * HEAD_DIM
WIN_KV = WIN_KV_HEADS * HEAD_DIM
WINDOW = 128
Q_BLOCK = 128
ROPE_THETA = 10000.0
N_BRANCH = 4
IN_SIZES = (FNET_WIDTH, 2 * CONV_WIDTH, ATTN_Q, ATTN_KV, ATTN_KV, WIN_Q, WIN_KV, WIN_KV, N_BRANCH * D_MODEL)
IN_COLS = sum(IN_SIZES)
N_KEYS = 128
N_EXPERTS = N_KEYS * N_KEYS
PEER_HEADS = 8
PEER_QDIM = 256
PEER_HALF = PEER_QDIM // 2
PEER_TOPK = 16
PEER_CHUNK = 128
EPS = 1e-6
NEG_INF = -1e30

kernel_name = "hybrid_diffusion_prefix_peer_step"


def rmsnorm(x, g):
    xf = x.astype(jnp.float32)
    y = xf * lax.rsqrt(jnp.mean(jnp.square(xf), -1, keepdims=True) + EPS)
    return (y * g.astype(jnp.float32)).astype(x.dtype)


def modulation(cond, w, b):
    ada = jax.nn.silu(cond) @ w + b
    ada = ada.reshape(cond.shape[0], 1, 6, D_MODEL)
    return tuple(ada[:, :, i] for i in range(6))


def axial_rope(s):
    n_rows = s // GRID_W
    row = jnp.repeat(jnp.arange(n_rows), GRID_W).astype(jnp.float32)
    col = jnp.tile(jnp.arange(GRID_W), n_rows).astype(jnp.float32)
    n_f = HEAD_DIM // 4
    inv = 1.0 / (ROPE_THETA ** (jnp.arange(n_f, dtype=jnp.float32) / n_f))
    ang = jnp.concatenate([row[:, None] * inv, col[:, None] * inv], axis=-1)
    return jnp.cos(ang), jnp.sin(ang)


def apply_rope(x, cos, sin):
    half = HEAD_DIM // 2
    shp = (1, x.shape[1]) + (1,) * (x.ndim - 3) + (half,)
    c = cos.reshape(shp).astype(x.dtype)
    s_ = sin.reshape(shp).astype(x.dtype)
    x1, x2 = x[..., :half], x[..., half:]
    return jnp.concatenate([x1 * c - x2 * s_, x1 * s_ + x2 * c], axis=-1)


def block_attention(q, k, v, sink):
    b, sq, kvh, g, hd = q.shape
    nb = sq // Q_BLOCK
    scale = hd ** -0.5
    qb = jnp.moveaxis(q.reshape(b, nb, Q_BLOCK, kvh, g, hd), 1, 0)

    def one(qi):
        s = jnp.einsum("bqhgd,bkhd->bhgqk", qi, k).astype(jnp.float32) * scale
        if sink is not None:
            sc = jnp.broadcast_to(sink.astype(jnp.float32)[None, :, :, None, None], (b, kvh, g, Q_BLOCK, 1))
            p = jax.nn.softmax(jnp.concatenate([s, sc], axis=-1), axis=-1)[..., :-1]
        else:
            p = jax.nn.softmax(s, axis=-1)
        return jnp.einsum("bhgqk,bkhd->bqhgd", p.astype(v.dtype), v)

    o = lax.map(one, qb)
    return jnp.moveaxis(o, 0, 1).reshape(b, sq, kvh, g, hd)


def window_attention(q, k, v, k_ctx, v_ctx, sink):
    b, s, kvh, g, hd = q.shape
    nb = s // Q_BLOCK
    span = 3 * Q_BLOCK
    scale = hd ** -0.5
    pad = ((0, 0), (Q_BLOCK, Q_BLOCK), (0, 0), (0, 0))
    kp = jnp.pad(k, pad)
    vp = jnp.pad(v, pad)
    qb = jnp.moveaxis(q.reshape(b, nb, Q_BLOCK, kvh, g, hd), 1, 0)
    rel = (jnp.arange(span)[None, :] - Q_BLOCK) - jnp.arange(Q_BLOCK)[:, None]
    near = jnp.abs(rel) <= WINDOW
    sink_col = jnp.broadcast_to(sink.astype(jnp.float32)[None, :, :, None, None], (b, kvh, g, Q_BLOCK, 1))

    def one(args):
        qi, blk = args
        start = blk * Q_BLOCK
        kw = lax.dynamic_slice_in_dim(kp, start, span, axis=1)
        vw = lax.dynamic_slice_in_dim(vp, start, span, axis=1)
        kpos = start - Q_BLOCK + jnp.arange(span)
        valid = near & ((kpos >= 0) & (kpos < s))[None, :]
        s_loc = jnp.einsum("bqhgd,bkhd->bhgqk", qi, kw).astype(jnp.float32) * scale
        s_loc = jnp.where(valid, s_loc, NEG_INF)
        s_ctx = jnp.einsum("bqhgd,bkhd->bhgqk", qi, k_ctx).astype(jnp.float32) * scale
        p = jax.nn.softmax(jnp.concatenate([s_loc, s_ctx, sink_col], axis=-1), axis=-1).astype(v.dtype)
        return (jnp.einsum("bhgqk,bkhd->bqhgd", p[..., :span], vw)
                + jnp.einsum("bhgqk,bkhd->bqhgd", p[..., span:-1], v_ctx))

    o = lax.map(one, (qb, jnp.arange(nb)))
    return jnp.moveaxis(o, 0, 1).reshape(b, s, kvh, g, hd)


def fourier_mix(u):
    b, s, _ = u.shape
    ug = u.reshape(b, s, FNET_GROUPS, FNET_GROUP_DIM).astype(jnp.float32)
    f = jnp.fft.fft2(ug, axes=(1, 3), norm="ortho").real
    return f.reshape(b, s, FNET_WIDTH).astype(u.dtype)


def conformer_conv(u, w_dw, b_dw, ln_g, ln_b):
    a, gte = jnp.split(u, 2, axis=-1)
    h = a * jax.nn.sigmoid(gte)
    h = lax.conv_general_dilated(h, w_dw[:, None, :].astype(h.dtype), (1,), [(CONV_K // 2, CONV_K // 2)],
                                 dimension_numbers=("NWC", "WIO", "NWC"),
                                 feature_group_count=CONV_WIDTH) + b_dw
    hf = h.astype(jnp.float32)
    mu = jnp.mean(hf, -1, keepdims=True)
    var = jnp.mean(jnp.square(hf - mu), -1, keepdims=True)
    hn = (hf - mu) * lax.rsqrt(var + EPS) * ln_g.astype(jnp.float32) + ln_b.astype(jnp.float32)
    return jax.nn.silu(hn).astype(u.dtype)


def mixer_block(h, lp, rope, ctx_kv):
    b, s, _ = h.shape
    offs = [int(o) for o in np.cumsum(IN_SIZES)[:-1]]
    f_in, c_in, aq, ak, av, wq, wk, wv, gl = jnp.split(h @ lp["w_in"], offs, axis=-1)
    aq = rmsnorm(aq.reshape(b, s, ATTN_KV_HEADS, ATTN_GROUP, HEAD_DIM), lp["q_norm_g"])
    ak = rmsnorm(ak.reshape(b, s, ATTN_KV_HEADS, HEAD_DIM), lp["k_norm_g"])
    av = av.reshape(b, s, ATTN_KV_HEADS, HEAD_DIM)
    wq = wq.reshape(b, s, WIN_KV_HEADS, WIN_GROUP, HEAD_DIM)
    wk = wk.reshape(b, s, WIN_KV_HEADS, HEAD_DIM)
    wv = wv.reshape(b, s, WIN_KV_HEADS, HEAD_DIM)
    sink = lp["win_sink"].reshape(WIN_KV_HEADS, WIN_GROUP)
    if ctx_kv is None:
        o_attn = block_attention(aq, ak, av, None)
        o_win = block_attention(wq, wk, wv, sink)
        ctx_tensors = (ak, av, wk, wv)
    else:
        cos, sin = rope
        ka_ctx, va_ctx, kw_ctx, vw_ctx = ctx_kv
        aq, ak = apply_rope(aq, cos, sin), apply_rope(ak, cos, sin)
        wq, wk = apply_rope(wq, cos, sin), apply_rope(wk, cos, sin)
        o_attn = block_attention(aq, jnp.concatenate([ak, ka_ctx], axis=1),
                                 jnp.concatenate([av, va_ctx], axis=1), None)
        o_win = window_attention(wq, wk, wv, kw_ctx, vw_ctx, sink)
        ctx_tensors = None
    y_f = fourier_mix(f_in) @ lp["w_fourier_out"]
    y_c = conformer_conv(c_in, lp["conv_dw"], lp["conv_b"], lp["conv_ln_g"], lp["conv_ln_b"]) @ lp["w_conv_out"]
    y_a = o_attn.reshape(b, s, ATTN_Q) @ lp["w_attn_out"]
    y_w = o_win.reshape(b, s, WIN_Q) @ lp["w_win_out"]
    gates = jax.nn.sigmoid(gl.reshape(b, s, N_BRANCH, D_MODEL) + lp["b_gate"])
    merged = gates[:, :, 0] * y_f + gates[:, :, 1] * y_c + gates[:, :, 2] * y_a + gates[:, :, 3] * y_w
    return merged @ lp["w_out"], ctx_tensors


def peer_ffn(h, w_q, subkeys, u_tab, v_tab):
    b, s, d = h.shape
    hc_all = h.reshape(-1, PEER_CHUNK, d)

    def one(hc):
        q = (hc @ w_q).reshape(PEER_CHUNK, PEER_HEADS, 2, PEER_HALF)
        sc = jnp.einsum("chpd,pkd->chpk", q, subkeys).astype(jnp.float32)
        s_half, i_half = lax.top_k(sc, PEER_TOPK)
        cand = s_half[:, :, 0, :, None] + s_half[:, :, 1, None, :]
        cand_idx = i_half[:, :, 0, :, None] * N_KEYS + i_half[:, :, 1, None, :]
        cand = cand.reshape(PEER_CHUNK, PEER_HEADS, PEER_TOPK * PEER_TOPK)
        cand_idx = cand_idx.reshape(PEER_CHUNK, PEER_HEADS, PEER_TOPK * PEER_TOPK)
        top_s, top_pos = lax.top_k(cand, PEER_TOPK)
        eidx = jnp.take_along_axis(cand_idx, top_pos, axis=-1)
        gate = jax.nn.softmax(top_s, axis=-1)
        u = u_tab[eidx]
        v = v_tab[eidx]
        act = jax.nn.gelu(jnp.einsum("chkd,cd->chk", u, hc).astype(jnp.float32))
        return jnp.einsum("chk,chkd->cd", (gate * act).astype(v.dtype), v)

    return lax.map(one, hc_all).reshape(b, s, d)


def trunk_layer(x, mod, lp, rope, ctx_kv):
    sh1, sc1, g1, sh2, sc2, g2 = mod
    h = rmsnorm(x, lp["norm1_g"]) * (1 + sc1) + sh1
    mix, ctx_tensors = mixer_block(h, lp, rope, ctx_kv)
    x = x + g1 * mix
    h = rmsnorm(x, lp["norm2_g"]) * (1 + sc2) + sh2
    x = x + g2 * peer_ffn(h, lp["w_peer_q"], lp["peer_subkeys"], lp["peer_u"], lp["peer_v"])
    return x, ctx_tensors


def setup_inputs(seed: int = 0) -> dict:
    key = jax.random.key(seed)
    ks = iter(jax.random.split(key, 40))
    f32 = jnp.float32

    def nrm(shape, scale):
        return jax.random.normal(next(ks), shape, f32) * scale

    def gain(shape):
        return 1.0 + nrm(shape, 0.02)

    kv_shape = (DEC_BATCH, DEPTH, PAST_LEN, ATTN_KV_HEADS, HEAD_DIM)
    wkv_shape = (DEC_BATCH, DEPTH, PAST_LEN, WIN_KV_HEADS, HEAD_DIM)
    return {
        "x_prompt": nrm((BATCH, SEQ, D_MODEL), 1.0),
        "x_sample": nrm((DEC_BATCH, DEC_SEQ, D_MODEL), 1.0),
        "cache_attn_k": nrm(kv_shape, 1.0),
        "cache_attn_v": nrm(kv_shape, 1.0),
        "cache_win_k": nrm(wkv_shape, 1.0),
        "cache_win_v": nrm(wkv_shape, 1.0),
        "c": nrm((DEC_BATCH, D_MODEL), 1.0),
        "c_ctx": nrm((D_MODEL,), 1.0),
        "norm1_g": gain((DEPTH, D_MODEL)),
        "norm2_g": gain((DEPTH, D_MODEL)),
        "w_ada": nrm((DEPTH, D_MODEL, 6 * D_MODEL), 0.5 * D_MODEL ** -0.5),
        "b_ada": nrm((DEPTH, 6 * D_MODEL), 0.02),
        "w_in": nrm((DEPTH, D_MODEL, IN_COLS), D_MODEL ** -0.5),
        "b_gate": nrm((DEPTH, N_BRANCH, D_MODEL), 0.02),
        "w_fourier_out": nrm((DEPTH, FNET_WIDTH, D_MODEL), FNET_WIDTH ** -0.5),
        "conv_dw": nrm((DEPTH, CONV_K, CONV_WIDTH), CONV_K ** -0.5),
        "conv_b": nrm((DEPTH, CONV_WIDTH), 0.02),
        "conv_ln_g": gain((DEPTH, CONV_WIDTH)),
        "conv_ln_b": nrm((DEPTH, CONV_WIDTH), 0.02),
        "w_conv_out": nrm((DEPTH, CONV_WIDTH, D_MODEL), CONV_WIDTH ** -0.5),
        "q_norm_g": gain((DEPTH, HEAD_DIM)),
        "k_norm_g": gain((DEPTH, HEAD_DIM)),
        "w_attn_out": nrm((DEPTH, ATTN_Q, D_MODEL), ATTN_Q ** -0.5),
        "win_sink": nrm((DEPTH, WIN_HEADS), 1.0),
        "w_win_out": nrm((DEPTH, WIN_Q, D_MODEL), WIN_Q ** -0.5),
        "w_out": nrm((DEPTH, D_MODEL, D_MODEL), D_MODEL ** -0.5),
        "w_peer_q": nrm((DEPTH, D_MODEL, PEER_HEADS * PEER_QDIM), D_MODEL ** -0.5),
        "peer_subkeys": nrm((DEPTH, 2, N_KEYS, PEER_HALF), PEER_HALF ** -0.5),
        "peer_u": nrm((DEPTH, N_EXPERTS, D_MODEL), D_MODEL ** -0.5),
        "peer_v": nrm((DEPTH, N_EXPERTS, D_MODEL), PEER_HEADS ** -0.5),
        "final_norm_g": gain((D_MODEL,)),
    }


def reference(x_prompt, x_sample, cache_attn_k, cache_attn_v, cache_win_k, cache_win_v, c, c_ctx,
              norm1_g, norm2_g, w_ada, b_ada, w_in, b_gate, w_fourier_out, conv_dw, conv_b,
              conv_ln_g, conv_ln_b, w_conv_out, q_norm_g, k_norm_g, w_attn_out, win_sink, w_win_out,
              w_out, w_peer_q, peer_subkeys, peer_u, peer_v, final_norm_g):
    rope = axial_rope(x_sample.shape[1])
    xp, xs = x_prompt, x_sample
    ctx_out = []
    for l in range(DEPTH):
        lp = {
            "norm1_g": norm1_g[l], "norm2_g": norm2_g[l], "w_in": w_in[l], "b_gate": b_gate[l],
            "w_fourier_out": w_fourier_out[l], "conv_dw": conv_dw[l], "conv_b": conv_b[l],
            "conv_ln_g": conv_ln_g[l], "conv_ln_b": conv_ln_b[l], "w_conv_out": w_conv_out[l],
            "q_norm_g": q_norm_g[l], "k_norm_g": k_norm_g[l], "w_attn_out": w_attn_out[l],
            "win_sink": win_sink[l], "w_win_out": w_win_out[l], "w_out": w_out[l],
            "w_peer_q": w_peer_q[l], "peer_subkeys": peer_subkeys[l], "peer_u": peer_u[l], "peer_v": peer_v[l],
        }
        mod_ctx = modulation(c_ctx[None, :], w_ada[l], b_ada[l])
        xp, ctx_t = trunk_layer(xp, mod_ctx, lp, None, None)
        ctx_out.append(ctx_t)
        mod_lat = modulation(c, w_ada[l], b_ada[l])
        cached = (cache_attn_k[:, l], cache_attn_v[:, l], cache_win_k[:, l], cache_win_v[:, l])
        xs, _ = trunk_layer(xs, mod_lat, lp, rope, cached)
    y_prompt = rmsnorm(xp, final_norm_g)
    y_sample = rmsnorm(xs, final_norm_g)
    new_attn_k = jnp.stack([t[0] for t in ctx_out], axis=1)
    new_attn_v = jnp.stack([t[1] for t in ctx_out], axis=1)
    new_win_k = jnp.stack([t[2] for t in ctx_out], axis=1)
    new_win_v = jnp.stack([t[3] for t in ctx_out], axis=1)
    return (y_prompt, y_sample, new_attn_k, new_attn_v, new_win_k, new_win_v)
```

```python
import functools

import numpy as np
import jax
import jax.numpy as jnp
from jax import lax
from jax.experimental import pallas as pl
from jax.experimental.pallas import tpu as pltpu

F32 = jnp.float32
BF16 = jnp.bfloat16

D_MODEL = 1024
DEPTH = 4
GRID_W = 64
HEAD_DIM = 64
FNET_GROUP_DIM = 64
FNET_WIDTH = 512
CONV_WIDTH = 512
CONV_K = 31
N_HEADS = 8
ATTN_Q = 512
ATTN_KV = 128
WINDOW = 128
ROPE_THETA = 10000.0
N_BRANCH = 4
COL_F, COL_C, COL_QKV, COL_GL = 0, 512, 1536, 3072
IN_COLS = 7168
N_KEYS = 128
N_EXPERTS = N_KEYS * N_KEYS
PEER_HEADS = 8
PEER_HALF = 128
PEER_TOPK = 16
EPS = 1e-6
NEG_INF = -1e30
LANES = 128

VMEM_LIMIT = 56 * 1024 * 1024


def _cparams(*sem):
    return pltpu.CompilerParams(dimension_semantics=sem, vmem_limit_bytes=VMEM_LIMIT)


def _dot(a, b):
    return jnp.dot(a, b, preferred_element_type=F32)


def _dot_nt(a, b):
    return lax.dot_general(a, b, (((1,), (1,)), ((), ())), preferred_element_type=F32)


def _normmod(x, g, sc, sh):
    ms = jnp.mean(x * x, axis=-1, keepdims=True)
    y = x * lax.rsqrt(ms + EPS)
    return (y * g) * (1.0 + sc) + sh


def _ada_kernel(c_ref, w_ref, b_ref, o_ref):
    c = c_ref[...]
    a = (c * jax.nn.sigmoid(c)).astype(BF16)
    o_ref[0] = _dot(a, w_ref[0].astype(BF16)) + b_ref[0]


def _ada_all(cond16, w_ada, b_ada):
    tn = 1536
    n = w_ada.shape[2]
    return pl.pallas_call(
        _ada_kernel,
        grid=(DEPTH, n // tn),
        in_specs=[
            pl.BlockSpec((16, D_MODEL), lambda l, j: (0, 0)),
            pl.BlockSpec((1, D_MODEL, tn), lambda l, j: (l, 0, j)),
            pl.BlockSpec((1, 1, tn), lambda l, j: (l, 0, j)),
        ],
        out_specs=pl.BlockSpec((1, 16, tn), lambda l, j: (l, 0, j)),
        out_shape=jax.ShapeDtypeStruct((DEPTH, 16, n), F32),
        compiler_params=_cparams("arbitrary", "arbitrary"),
        name="ada",
    )(cond16, w_ada, b_ada.reshape(DEPTH, 1, n))


def _in_plain_kernel(x_ref, mod_ref, g_ref, w_ref, o_ref, h_ref):
    @pl.when(pl.program_id(1) == 0)
    def _():
        h = _normmod(x_ref[...], g_ref[...], mod_ref[0, 1:2, :], mod_ref[0, 0:1, :])
        h_ref[...] = h.astype(BF16)

    o_ref[...] = _dot(h_ref[...], w_ref[...]).astype(o_ref.dtype)


def _in_plain(x, mod, g, w_in, seq, col_start, ncols, tn, out_dtype, tt=512):
    t = x.shape[0]
    nb = mod.shape[0]
    tiles_per_batch = seq // tt if nb > 1 else None
    mod_idx = (lambda i, j: (i // tiles_per_batch, 0, 0)) if nb > 1 else (lambda i, j: (0, 0, 0))
    c0 = col_start // tn
    return pl.pallas_call(
        _in_plain_kernel,
        grid=(t // tt, ncols // tn),
        in_specs=[
            pl.BlockSpec((tt, D_MODEL), lambda i, j: (i, 0)),
            pl.BlockSpec((1, 6, D_MODEL), mod_idx),
            pl.BlockSpec((1, D_MODEL), lambda i, j: (0, 0)),
            pl.BlockSpec((D_MODEL, tn), lambda i, j: (0, c0 + j)),
        ],
        out_specs=pl.BlockSpec((tt, tn), lambda i, j: (i, j)),
        out_shape=jax.ShapeDtypeStruct((t, ncols), out_dtype),
        scratch_shapes=[pltpu.VMEM((tt, D_MODEL), BF16)],
        compiler_params=_cparams("arbitrary", "arbitrary"),
        name="in_plain",
    )(x, mod, g, w_in)


def _seg_mean_sq(x, bd):
    sq = x * x
    hi = sq.astype(BF16)
    lo = (sq - hi.astype(F32)).astype(BF16)
    return (_dot(hi, bd) + _dot(lo, bd)) * (1.0 / HEAD_DIM)


def _rope(x, cos, sin_signed):
    w = x.shape[1]
    fwd = pltpu.roll(x, HEAD_DIM // 2, axis=1)
    bwd = pltpu.roll(x, w - HEAD_DIM // 2, axis=1)
    lane = lax.broadcasted_iota(jnp.int32, x.shape, 1)
    first = (lane % HEAD_DIM) < (HEAD_DIM // 2)
    partner = jnp.where(first, bwd, fwd)
    return x * cos + partner * sin_signed


def _store_head_padded(o_ref, k):
    kr = pltpu.roll(k, HEAD_DIM, axis=1)
    lane = lax.broadcasted_iota(jnp.int32, k.shape, 1)
    lo = lane < HEAD_DIM
    zero = jnp.zeros_like(k)
    o_ref[:, 0:128] = jnp.where(lo, k, zero).astype(BF16)
    o_ref[:, 128:256] = jnp.where(lo, zero, kr).astype(BF16)
    o_ref[:, 256:384] = jnp.where(lo, kr, zero).astype(BF16)
    o_ref[:, 384:512] = jnp.where(lo, zero, k).astype(BF16)


def _in_qkv_kernel(*refs, use_rope):
    if use_rope:
        (x_ref, mod_ref, g_ref, w_ref, bd_ref, qg_ref, kg_ref, cos_ref, sin_ref,
         qa_ref, qw_ref, kxa_ref, vxa_ref, kxw_ref, vxw_ref, kvf_ref) = refs
    else:
        (x_ref, mod_ref, g_ref, w_ref, bd_ref, qg_ref, kg_ref,
         qa_ref, qw_ref, kxa_ref, vxa_ref, kxw_ref, vxw_ref, kvf_ref) = refs
    h = _normmod(x_ref[...], g_ref[...], mod_ref[0, 1:2, :], mod_ref[0, 0:1, :]).astype(BF16)
    acc = _dot(h, w_ref[...])
    aq = acc[:, 0:512]
    ak = acc[:, 512:640]
    av = acc[:, 640:768]
    wq = acc[:, 768:1280]
    wk = acc[:, 1280:1408]
    wv = acc[:, 1408:1536]
    bd = bd_ref[...]
    aq = aq * lax.rsqrt(_seg_mean_sq(aq, bd) + EPS) * qg_ref[...]
    ak = ak * lax.rsqrt(_seg_mean_sq(ak, bd[0:128, 0:128]) + EPS) * kg_ref[...]
    if use_rope:
        cos = cos_ref[...]
        sin = sin_ref[...]
        aq = _rope(aq, cos, sin)
        wq = _rope(wq, cos, sin)
        ak = _rope(ak, cos[:, 0:128], sin[:, 0:128])
        wk = _rope(wk, cos[:, 0:128], sin[:, 0:128])
    scale = HEAD_DIM ** -0.5
    qa_ref[...] = (aq * scale).astype(BF16)
    qw_ref[...] = (wq * scale).astype(BF16)
    _store_head_padded(kxa_ref, ak)
    _store_head_padded(vxa_ref, av)
    _store_head_padded(kxw_ref, wk)
    _store_head_padded(vxw_ref, wv)
    kvf_ref[:, 0:128] = ak
    kvf_ref[:, 128:256] = av
    kvf_ref[:, 256:384] = wk
    kvf_ref[:, 384:512] = wv


def _in_qkv(x, mod, g, w_in, seq, bd, qg8, kg2, rope_tabs, tt=256):
    t = x.shape[0]
    nb = mod.shape[0]
    tpb = seq // tt
    mod_idx = (lambda i: (i // tpb, 0, 0)) if nb > 1 else (lambda i: (0, 0, 0))
    use_rope = rope_tabs is not None
    ncol = 1536
    in_specs = [
        pl.BlockSpec((tt, D_MODEL), lambda i: (i, 0)),
        pl.BlockSpec((1, 6, D_MODEL), mod_idx),
        pl.BlockSpec((1, D_MODEL), lambda i: (0, 0)),
        pl.BlockSpec((D_MODEL, ncol), lambda i: (0, COL_QKV // ncol)),
        pl.BlockSpec((512, 512), lambda i: (0, 0)),
        pl.BlockSpec((1, 512), lambda i: (0, 0)),
        pl.BlockSpec((1, 128), lambda i: (0, 0)),
    ]
    args = [x, mod, g, w_in, bd, qg8, kg2]
    if use_rope:
        in_specs += [pl.BlockSpec((tt, 512), lambda i: (i % tpb, 0))] * 2
        args += list(rope_tabs)
    tok = pl.BlockSpec((tt, 512), lambda i: (i, 0))
    return pl.pallas_call(
        functools.partial(_in_qkv_kernel, use_rope=use_rope),
        grid=(t // tt,),
        in_specs=in_specs,
        out_specs=[tok] * 7,
        out_shape=[jax.ShapeDtypeStruct((t, 512), BF16)] * 6 + [jax.ShapeDtypeStruct((t, 512), F32)],
        compiler_params=_cparams("arbitrary"),
        name="in_qkv",
    )(*args)


def _attn_kernel(*refs, tq, seq, n_ctx, windowed):
    if n_ctx:
        sink_ref, q_ref, kx_ref, vx_ref, kc_ref, vc_ref, o_ref = refs
    else:
        sink_ref, q_ref, kx_ref, vx_ref, o_ref = refs
    i = pl.program_id(1)
    if windowed:
        span = tq + 2 * WINDOW
        start = jnp.clip(i * tq - WINDOW, 0, seq - span)
        start = pl.multiple_of(start, WINDOW)
        qpos = i * tq + lax.broadcasted_iota(jnp.int32, (tq, span), 0)
        kpos = start + lax.broadcasted_iota(jnp.int32, (tq, span), 1)
        valid = jnp.abs(kpos - qpos) <= WINDOW
    for jb in range(4):
        q2 = q_ref[:, jb * 128:(jb + 1) * 128]
        acc = None
        for e in range(2):
            head = jb * 2 + e
            v_idx = (jb // 2) * 2 + e
            cols = slice(v_idx * 128, (v_idx + 1) * 128)
            if windowed:
                k_loc = kx_ref[pl.ds(start, span), cols]
                v_loc = vx_ref[pl.ds(start, span), cols]
            else:
                k_loc = kx_ref[:, cols]
                v_loc = vx_ref[:, cols]
            s = _dot_nt(q2, k_loc)
            if windowed:
                s = jnp.where(valid, s, NEG_INF)
            sink = sink_ref[head]
            m = jnp.maximum(jnp.max(s, axis=-1, keepdims=True), sink)
            if n_ctx:
                s_c = _dot_nt(q2, kc_ref[:, cols])
                m = jnp.maximum(m, jnp.max(s_c, axis=-1, keepdims=True))
            p = jnp.exp(s - m)
            l = jnp.sum(p, axis=-1, keepdims=True) + jnp.exp(sink - m)
            o = _dot(p.astype(BF16), v_loc)
            if n_ctx:
                p_c = jnp.exp(s_c - m)
                l = l + jnp.sum(p_c, axis=-1, keepdims=True)
                o = o + _dot(p_c.astype(BF16), vc_ref[:, cols])
            o = o / l
            acc = o if acc is None else acc + o
        o_ref[:, jb * 128:(jb + 1) * 128] = acc.astype(o_ref.dtype)


def _attention(q, kx, vx, sink, batch, seq, ctx=None, windowed=False, tq=256):
    n_ctx = 0 if ctx is None else ctx[0].shape[0] // batch
    nq = seq // tq
    in_specs = [
        pl.BlockSpec(memory_space=pltpu.SMEM),
        pl.BlockSpec((tq, 512), lambda b, i: (b * nq + i, 0)),
        pl.BlockSpec((seq, 512), lambda b, i: (b, 0)),
        pl.BlockSpec((seq, 512), lambda b, i: (b, 0)),
    ]
    args = [sink, q, kx, vx]
    if n_ctx:
        in_specs += [pl.BlockSpec((n_ctx, 512), lambda b, i: (b, 0))] * 2
        args += list(ctx)
    return pl.pallas_call(
        functools.partial(_attn_kernel, tq=tq, seq=seq, n_ctx=n_ctx, windowed=windowed),
        grid=(batch, nq),
        in_specs=in_specs,
        out_specs=pl.BlockSpec((tq, 512), lambda b, i: (b * nq + i, 0)),
        out_shape=jax.ShapeDtypeStruct((batch * seq, 512), BF16),
        compiler_params=_cparams("arbitrary", "arbitrary"),
        name="attn_win" if windowed else "attn",
    )(*args)


def _fourier_kernel(f_ref, ccs_ref, dft_ref, o_ref, z_ref, *, seq, scale):
    @pl.when(pl.program_id(1) == 0)
    def _():
        z = _dot(f_ref[...], ccs_ref[...])
        z_ref[0:seq, :] = z[:, 0:FNET_WIDTH].astype(BF16)
        z_ref[seq:2 * seq, :] = z[:, FNET_WIDTH:].astype(BF16)

    o_ref[...] = (_dot(dft_ref[...], z_ref[...]) * scale).astype(o_ref.dtype)


def _fourier(f_in, ccs, dft, batch, seq, tr):
    nr = seq // tr
    return pl.pallas_call(
        functools.partial(_fourier_kernel, seq=seq, scale=float((seq * FNET_GROUP_DIM) ** -0.5)),
        grid=(batch, nr),
        in_specs=[
            pl.BlockSpec((seq, FNET_WIDTH), lambda b, i: (b, 0)),
            pl.BlockSpec((FNET_WIDTH, 2 * FNET_WIDTH), lambda b, i: (0, 0)),
            pl.BlockSpec((tr, 2 * seq), lambda b, i: (i, 0)),
        ],
        out_specs=pl.BlockSpec((tr, FNET_WIDTH), lambda b, i: (b * nr + i, 0)),
        out_shape=jax.ShapeDtypeStruct((batch * seq, FNET_WIDTH), BF16),
        scratch_shapes=[pltpu.VMEM((2 * seq, FNET_WIDTH), BF16)],
        compiler_params=_cparams("arbitrary", "arbitrary"),
        name="fourier",
    )(f_in, ccs, dft)


def _dft_tables(seq):
    k = np.arange(seq, dtype=np.int64)
    ang = 2.0 * np.pi * ((k[:, None] * k[None, :]) % seq).astype(np.float64) / seq
    return np.concatenate([np.cos(ang), -np.sin(ang)], axis=1).astype(np.float32)


def _channel_dft_tables():
    g = FNET_GROUP_DIM
    k = np.arange(g, dtype=np.int64)
    ang = 2.0 * np.pi * ((k[:, None] * k[None, :]) % g).astype(np.float64) / g
    eye = np.eye(FNET_WIDTH // g)
    return np.concatenate([np.kron(eye, np.cos(ang)), np.kron(eye, np.sin(ang))], axis=1).astype(np.float32)


CONV_PAD = 16
CONV_ROWS = 128


def _conv_kernel(c_ref, dw_ref, b_ref, lg_ref, lb_ref, o_ref, hp_ref, *, seq):
    zeros = jnp.zeros((CONV_PAD, CONV_WIDTH), F32)
    hp_ref[0:CONV_PAD, :] = zeros
    hp_ref[seq + CONV_PAD:seq + 2 * CONV_PAD, :] = zeros

    def glu_body(r, carry):
        base = pl.multiple_of(r * CONV_ROWS, CONV_ROWS)
        a = c_ref[pl.ds(base, CONV_ROWS), 0:CONV_WIDTH]
        gt = c_ref[pl.ds(base, CONV_ROWS), CONV_WIDTH:2 * CONV_WIDTH]
        hp_ref[pl.ds(base + CONV_PAD, CONV_ROWS), :] = a * jax.nn.sigmoid(gt)
        return carry

    lax.fori_loop(0, seq // CONV_ROWS, glu_body, 0)

    blk_rows = CONV_ROWS + 2 * CONV_PAD

    def conv_body(r, carry):
        base = pl.multiple_of(r * CONV_ROWS, CONV_ROWS)
        outs = []
        for cch in range(CONV_WIDTH // LANES):
            ls = slice(cch * LANES, (cch + 1) * LANES)
            blk = hp_ref[pl.ds(base, blk_rows), ls]
            acc = jnp.zeros((CONV_ROWS, LANES), F32)
            for s in range(8):
                blk_s = blk if s == 0 else pltpu.roll(blk, blk_rows - s, axis=0)
                for m in range(4):
                    off = 8 * m + s
                    k = off - 1
                    if 0 <= k < CONV_K:
                        acc = acc + blk_s[8 * m:8 * m + CONV_ROWS, :] * dw_ref[k:k + 1, ls]
            outs.append(acc)
        y = jnp.concatenate(outs, axis=1) + b_ref[...]
        mu = jnp.mean(y, axis=-1, keepdims=True)
        yc = y - mu
        var = jnp.mean(yc * yc, axis=-1, keepdims=True)
        hn = yc * lax.rsqrt(var + EPS) * lg_ref[...] + lb_ref[...]
        o_ref[pl.ds(base, CONV_ROWS), :] = (hn * jax.nn.sigmoid(hn)).astype(o_ref.dtype)
        return carry

    lax.fori_loop(0, seq // CONV_ROWS, conv_body, 0)


def _conv(c_in, dw, b, lg, lb, batch, seq):
    vec = pl.BlockSpec((1, CONV_WIDTH), lambda bi: (0, 0))
    return pl.pallas_call(
        functools.partial(_conv_kernel, seq=seq),
        grid=(batch,),
        in_specs=[
            pl.BlockSpec((seq, 2 * CONV_WIDTH), lambda bi: (bi, 0)),
            pl.BlockSpec((CONV_K, CONV_WIDTH), lambda bi: (0, 0)),
            vec, vec, vec,
        ],
        out_specs=pl.BlockSpec((seq, CONV_WIDTH), lambda bi: (bi, 0)),
        out_shape=jax.ShapeDtypeStruct((batch * seq, CONV_WIDTH), BF16),
        scratch_shapes=[pltpu.VMEM((seq + 2 * CONV_PAD, CONV_WIDTH), F32)],
        compiler_params=_cparams("arbitrary"),
        name="conv",
    )(c_in, dw, b, lg, lb)


def _out_kernel(f_ref, c_ref, a_ref, w_ref, gl_ref, x_ref, mod_ref, bg_ref, g2_ref,
                wf_ref, wc_ref, wa_ref, ww_ref, wo_ref, x1_ref, h2t_ref):
    merged = None
    for i, (br, wt) in enumerate(((f_ref, wf_ref), (c_ref, wc_ref), (a_ref, wa_ref), (w_ref, ww_ref))):
        y = _dot(br[...], wt[...])
        gate = jax.nn.sigmoid(gl_ref[:, i * D_MODEL:(i + 1) * D_MODEL] + bg_ref[i:i + 1, :])
        merged = gate * y if merged is None else merged + gate * y
    mix = _dot(merged.astype(BF16), wo_ref[...])
    x1 = x_ref[...] + mod_ref[0, 2:3, :] * mix
    x1_ref[...] = x1
    h2 = _normmod(x1, g2_ref[...], mod_ref[0, 4:5, :], mod_ref[0, 3:4, :])
    h2t_ref[...] = h2.T.astype(BF16)


def _out_proj(f, c, a, w, gl, x, mod, bg, g2, wf, wc, wa, ww, wo, seq, tt=256):
    t = x.shape[0]
    nb = mod.shape[0]
    tpb = seq // tt
    mod_idx = (lambda i: (i // tpb, 0, 0)) if nb > 1 else (lambda i: (0, 0, 0))
    br = pl.BlockSpec((tt, 512), lambda i: (i, 0))
    wbr = pl.BlockSpec((512, D_MODEL), lambda i: (0, 0))
    return pl.pallas_call(
        _out_kernel,
        grid=(t // tt,),
        in_specs=[
            br, br, br, br,
            pl.BlockSpec((tt, N_BRANCH * D_MODEL), lambda i: (i, 0)),
            pl.BlockSpec((tt, D_MODEL), lambda i: (i, 0)),
            pl.BlockSpec((1, 6, D_MODEL), mod_idx),
            pl.BlockSpec((N_BRANCH, D_MODEL), lambda i: (0, 0)),
            pl.BlockSpec((1, D_MODEL), lambda i: (0, 0)),
            wbr, wbr, wbr, wbr,
            pl.BlockSpec((D_MODEL, D_MODEL), lambda i: (0, 0)),
        ],
        out_specs=[
            pl.BlockSpec((tt, D_MODEL), lambda i: (i, 0)),
            pl.BlockSpec((D_MODEL, tt), lambda i: (0, i)),
        ],
        out_shape=[jax.ShapeDtypeStruct((t, D_MODEL), F32), jax.ShapeDtypeStruct((D_MODEL, t), BF16)],
        compiler_params=_cparams("arbitrary"),
        name="out_proj",
    )(f, c, a, w, gl, x, mod, bg, g2, wf, wc, wa, ww, wo)


NOT_TOP = 99.0


def _route_kernel(h2t_ref, wq_ref, sk_ref, a0_ref, n0_ref, a1_ref, r1_ref,
                  qt_ref, sc_ref, srt_ref, rnk_ref, *, tt):
    ncol = tt // LANES
    qt_ref[...] = _dot(wq_ref[...], h2t_ref[...]).astype(BF16)
    for hp in range(2 * PEER_HEADS):
        sc_ref[hp] = _dot(sk_ref[hp % 2], qt_ref[hp * PEER_HALF:(hp + 1) * PEER_HALF, :])

    key_iota = lax.broadcasted_iota(jnp.int32, (N_KEYS, LANES), 0).astype(F32)

    def stage1(idx, carry):
        hp = idx // ncol
        ls = pl.ds(pl.multiple_of((idx % ncol) * LANES, LANES), LANES)
        s = sc_ref[hp, :, ls]
        rank = jnp.full((N_KEYS, LANES), NOT_TOP, F32)
        tops = []
        for it in range(PEER_TOPK):
            m = jnp.max(s, axis=0, keepdims=True)
            first = jnp.min(jnp.where(s == m, key_iota, float(N_KEYS)), axis=0, keepdims=True)
            hit = key_iota == first
            s = jnp.where(hit, -jnp.inf, s)
            rank = jnp.where(hit, float(it), rank)
            tops.append(m)
        srt_ref[hp, :, ls] = jnp.concatenate(tops, axis=0)
        rnk_ref[hp, :, ls] = rank
        return carry

    lax.fori_loop(0, 2 * PEER_HEADS * ncol, stage1, 0)

    sub16 = lax.broadcasted_iota(jnp.int32, (16, LANES), 0).astype(F32)
    sub8 = lax.broadcasted_iota(jnp.int32, (8, LANES), 0).astype(F32)

    def stage2(idx, carry):
        h = idx // ncol
        ls = pl.ds(pl.multiple_of((idx % ncol) * LANES, LANES), LANES)
        s0 = srt_ref[2 * h, :, ls]
        s1 = srt_ref[2 * h + 1, :, ls]
        cands = [s0[0:1, :] + s1]
        poss = [sub16]
        for a in range(1, 8):
            c = s0[a:a + 1, :] + s1[0:8, :]
            cands.append(jnp.where(sub8 < float(16 // (a + 1)), c, -jnp.inf))
            poss.append(sub8 + float(16 * a))
        cands.append(s0[8:16, :] + s1[0:1, :])
        poss.append((sub8 + 8.0) * 16.0)
        sels = [jnp.zeros_like(c) for c in cands]
        tops = []
        for it in range(PEER_TOPK):
            m = None
            for c in cands:
                cm = jnp.max(c, axis=0, keepdims=True)
                m = cm if m is None else jnp.maximum(m, cm)
            first = None
            for c, p in zip(cands, poss):
                f = jnp.min(jnp.where(c == m, p, 999.0), axis=0, keepdims=True)
                first = f if first is None else jnp.minimum(first, f)
            for j in range(len(cands)):
                hit = poss[j] == first
                cands[j] = jnp.where(hit, -jnp.inf, cands[j])
                sels[j] = jnp.where(hit, 1.0, sels[j])
            tops.append(m)
        z = None
        for m in tops:
            e = jnp.exp(m - tops[0])
            z = e if z is None else z + e
        inv_z = 1.0 / z
        n_lo = jnp.zeros((8, LANES), F32)
        for a in range(8):
            n_a = jnp.sum(sels[a], axis=0, keepdims=True)
            n_lo = jnp.where(sub8 == float(a), n_a, n_lo)
        n_hi = sels[8]
        rank0 = rnk_ref[2 * h, :, ls]
        rank1 = rnk_ref[2 * h + 1, :, ls]
        n0 = jnp.zeros((N_KEYS, LANES), F32)
        for a in range(8):
            n0 = jnp.where(rank0 == float(a), n_lo[a:a + 1, :], n0)
            n0 = jnp.where(rank0 == float(a + 8), n_hi[a:a + 1, :], n0)
        sc0 = sc_ref[2 * h, :, ls]
        sc1 = sc_ref[2 * h + 1, :, ls]
        a0 = jnp.where(rank0 < float(PEER_TOPK), jnp.exp(sc0 - s0[0:1, :]) * inv_z, 0.0)
        a1 = jnp.where(rank1 < float(PEER_TOPK), jnp.exp(sc1 - s1[0:1, :]), 0.0)
        a0_ref[h, :, ls] = a0
        n0_ref[h, :, ls] = n0
        a1_ref[h, :, ls] = a1
        r1_ref[h, :, ls] = rank1
        return carry

    lax.fori_loop(0, PEER_HEADS * ncol, stage2, 0)


def _route(h2t, wq_t, sk, tt=256):
    t = h2t.shape[1]
    out = pl.BlockSpec((PEER_HEADS, N_KEYS, tt), lambda i: (0, 0, i))
    shp = jax.ShapeDtypeStruct((PEER_HEADS, N_KEYS, t), F32)
    return pl.pallas_call(
        functools.partial(_route_kernel, tt=tt),
        grid=(t // tt,),
        in_specs=[
            pl.BlockSpec((D_MODEL, tt), lambda i: (0, i)),
            pl.BlockSpec((2 * PEER_HEADS * PEER_HALF, D_MODEL), lambda i: (0, 0)),
            pl.BlockSpec((2, N_KEYS, PEER_HALF), lambda i: (0, 0, 0)),
        ],
        out_specs=[out] * 4,
        out_shape=[shp] * 4,
        scratch_shapes=[
            pltpu.VMEM((2 * PEER_HEADS * PEER_HALF, tt), BF16),
            pltpu.VMEM((2 * PEER_HEADS, N_KEYS, tt), F32),
            pltpu.VMEM((2 * PEER_HEADS, PEER_TOPK, tt), F32),
            pltpu.VMEM((2 * PEER_HEADS, N_KEYS, tt), F32),
        ],
        compiler_params=_cparams("arbitrary"),
        name="peer_route",
    )(h2t, wq_t, sk)


def _peer_kernel(h2t_ref, a0_ref, n0_ref, a1_ref, r1_ref, u_ref, vt_ref, x1_ref, mod_ref,
                 o_ref, s_ref, w_ref, acc_ref, *, tt, ec):
    e = pl.program_id(1)
    n_i = ec // N_KEYS
    assert n_i == 8, "one expert chunk must cover exactly one sublane tile of half-0 keys"

    @pl.when(e == 0)
    def _():
        acc_ref[...] = jnp.zeros_like(acc_ref)

    s_ref[...] = _dot(u_ref[...], h2t_ref[...])

    i0 = pl.multiple_of(e * n_i, n_i)

    def body(col, carry):
        ls = pl.ds(pl.multiple_of(col * LANES, LANES), LANES)
        n0t = [n0_ref[h, pl.ds(i0, n_i), ls] for h in range(PEER_HEADS)]
        a0t = [a0_ref[h, pl.ds(i0, n_i), ls] for h in range(PEER_HEADS)]
        for il in range(n_i):
            rows = slice(il * N_KEYS, (il + 1) * N_KEYS)
            g = jnp.zeros((N_KEYS, LANES), F32)
            for h in range(PEER_HEADS):
                n0 = n0t[h][il:il + 1, :]
                a0 = a0t[h][il:il + 1, :]
                g = g + jnp.where(r1_ref[h, :, ls] < n0, a0 * a1_ref[h, :, ls], 0.0)
            act = jax.nn.gelu(s_ref[rows, ls])
            w_ref[rows, ls] = (g * act).astype(BF16)
        return carry

    lax.fori_loop(0, tt // LANES, body, 0)
    acc_ref[...] += _dot(vt_ref[...], w_ref[...])

    @pl.when(e == pl.num_programs(1) - 1)
    def _():
        o_ref[...] = x1_ref[...] + mod_ref[0, 5:6, :] * acc_ref[...].T


def _peer(h2t, a0, n0, a1, r1, u, vt, x1, mod, seq, tt=512, ec=1024):
    t = x1.shape[0]
    nb = mod.shape[0]
    tpb = seq // tt
    mod_idx = (lambda i, e: (i // tpb, 0, 0)) if nb > 1 else (lambda i, e: (0, 0, 0))
    rt = pl.BlockSpec((PEER_HEADS, N_KEYS, tt), lambda i, e: (0, 0, i))
    return pl.pallas_call(
        functools.partial(_peer_kernel, tt=tt, ec=ec),
        grid=(t // tt, N_EXPERTS // ec),
        in_specs=[
            pl.BlockSpec((D_MODEL, tt), lambda i, e: (0, i)),
            rt, rt, rt, rt,
            pl.BlockSpec((ec, D_MODEL), lambda i, e: (e, 0)),
            pl.BlockSpec((D_MODEL, ec), lambda i, e: (0, e)),
            pl.BlockSpec((tt, D_MODEL), lambda i, e: (i, 0)),
            pl.BlockSpec((1, 6, D_MODEL), mod_idx),
        ],
        out_specs=pl.BlockSpec((tt, D_MODEL), lambda i, e: (i, 0)),
        out_shape=jax.ShapeDtypeStruct((t, D_MODEL), F32),
        scratch_shapes=[
            pltpu.VMEM((ec, tt), F32),
            pltpu.VMEM((ec, tt), BF16),
            pltpu.VMEM((D_MODEL, tt), F32),
        ],
        compiler_params=_cparams("arbitrary", "arbitrary"),
        name="peer_mix",
    )(h2t, a0, n0, a1, r1, u, vt, x1, mod)


def _final_norm_kernel(x_ref, g_ref, o_ref):
    x = x_ref[...]
    ms = jnp.mean(x * x, axis=-1, keepdims=True)
    o_ref[...] = x * lax.rsqrt(ms + EPS) * g_ref[...]


def _final_norm(x, g, tt=512):
    t = x.shape[0]
    return pl.pallas_call(
        _final_norm_kernel,
        grid=(t // tt,),
        in_specs=[pl.BlockSpec((tt, D_MODEL), lambda i: (i, 0)), pl.BlockSpec((1, D_MODEL), lambda i: (0, 0))],
        out_specs=pl.BlockSpec((tt, D_MODEL), lambda i: (i, 0)),
        out_shape=jax.ShapeDtypeStruct((t, D_MODEL), F32),
        compiler_params=_cparams("arbitrary"),
        name="final_norm",
    )(x, g)


def _rope_tables(seq):
    n_rows = seq // GRID_W
    row = jnp.repeat(jnp.arange(n_rows), GRID_W).astype(F32)
    col = jnp.tile(jnp.arange(GRID_W), n_rows).astype(F32)
    n_f = HEAD_DIM // 4
    inv = 1.0 / (ROPE_THETA ** (jnp.arange(n_f, dtype=F32) / n_f))
    ang = jnp.concatenate([row[:, None] * inv, col[:, None] * inv], axis=-1)
    cos, sin = jnp.cos(ang), jnp.sin(ang)
    cos_t = jnp.tile(jnp.concatenate([cos, cos], axis=-1), (1, N_HEADS))
    sin_t = jnp.tile(jnp.concatenate([-sin, sin], axis=-1), (1, N_HEADS))
    return cos_t, sin_t


def _pad_heads(kv):
    z = jnp.zeros_like(kv[:, :HEAD_DIM])
    h0, h1 = kv[:, :HEAD_DIM], kv[:, HEAD_DIM:]
    return jnp.concatenate([h0, z, z, h0, h1, z, z, h1], axis=-1).astype(BF16)


def _layer(x, mod, lp, consts, batch, seq, ctx_kv):
    is_ctx = ctx_kv is None
    g1 = lp["norm1_g"]
    w_in = lp["w_in"]
    f_in = _in_plain(x, mod, g1, w_in, seq, COL_F, 512, 512, BF16)
    c_in = _in_plain(x, mod, g1, w_in, seq, COL_C, 1024, 512, F32)
    gl = _in_plain(x, mod, g1, w_in, seq, COL_GL, 4096, 1024, F32)
    qa, qw, kxa, vxa, kxw, vxw, kvf = _in_qkv(
        x, mod, g1, w_in, seq, consts["bd"], lp["q_norm_g"], lp["k_norm_g"],
        None if is_ctx else consts["rope"])
    dft = consts["dft_ctx"] if is_ctx else consts["dft_lat"]
    f = _fourier(f_in, consts["ccs"], dft, batch, seq, min(seq, 512))
    cv = _conv(c_in, lp["conv_dw"], lp["conv_b"], lp["conv_ln_g"], lp["conv_ln_b"], batch, seq)
    no_sink = jnp.full((N_HEADS,), NEG_INF, F32)
    if is_ctx:
        a = _attention(qa, kxa, vxa, no_sink, batch, seq)
        w = _attention(qw, kxw, vxw, lp["win_sink"], batch, seq)
    else:
        ka_c, va_c, kw_c, vw_c = ctx_kv
        a = _attention(qa, kxa, vxa, no_sink, batch, seq, ctx=(ka_c, va_c))
        w = _attention(qw, kxw, vxw, lp["win_sink"], batch, seq, ctx=(kw_c, vw_c), windowed=True)
    x1, h2t = _out_proj(f, cv, a, w, gl, x, mod, lp["b_gate"], lp["norm2_g"],
                        lp["w_fourier_out"], lp["w_conv_out"], lp["w_attn_out"], lp["w_win_out"],
                        lp["w_out"], seq)
    a0, n0, a1, r1 = _route(h2t, lp["w_peer_q_t"], lp["peer_subkeys"])
    x2 = _peer(h2t, a0, n0, a1, r1, lp["peer_u"], lp["peer_v_t"], x1, mod, seq)
    return x2, kvf


def kernel(x_prompt, x_sample, cache_attn_k, cache_attn_v, cache_win_k, cache_win_v, c, c_ctx, norm1_g, norm2_g, w_ada, b_ada, w_in, b_gate, w_fourier_out, conv_dw, conv_b, conv_ln_g, conv_ln_b, w_conv_out, q_norm_g, k_norm_g, w_attn_out, win_sink, w_win_out, w_out, w_peer_q, peer_subkeys, peer_u, peer_v, final_norm_g):
    b_ctx, s_ctx, _ = x_prompt.shape
    b_lat, s_lat, _ = x_sample.shape
    past = cache_attn_k.shape[2]

    cond = jnp.concatenate([c_ctx[None, :], c, jnp.zeros((16 - 1 - b_lat, D_MODEL), F32)], axis=0)
    ada = _ada_all(cond, w_ada, b_ada).reshape(DEPTH, 16, 6, D_MODEL)

    consts = {
        "bd": jnp.asarray(np.kron(np.eye(N_HEADS), np.ones((HEAD_DIM, HEAD_DIM))), BF16),
        "ccs": jnp.asarray(_channel_dft_tables()).astype(BF16),
        "dft_ctx": jnp.asarray(_dft_tables(s_ctx)).astype(BF16),
        "dft_lat": jnp.asarray(_dft_tables(s_lat)).astype(BF16),
        "rope": _rope_tables(s_lat),
    }

    xp = x_prompt.reshape(b_ctx * s_ctx, D_MODEL)
    xs = x_sample.reshape(b_lat * s_lat, D_MODEL)
    kv_out = []
    for l in range(DEPTH):
        lp = {
            "norm1_g": norm1_g[l][None, :], "norm2_g": norm2_g[l][None, :],
            "w_in": w_in[l].astype(BF16), "b_gate": b_gate[l],
            "w_fourier_out": w_fourier_out[l].astype(BF16),
            "conv_dw": conv_dw[l], "conv_b": conv_b[l][None, :],
            "conv_ln_g": conv_ln_g[l][None, :], "conv_ln_b": conv_ln_b[l][None, :],
            "w_conv_out": w_conv_out[l].astype(BF16),
            "q_norm_g": jnp.tile(q_norm_g[l], N_HEADS)[None, :],
            "k_norm_g": jnp.tile(k_norm_g[l], 2)[None, :],
            "w_attn_out": w_attn_out[l].astype(BF16), "win_sink": win_sink[l],
            "w_win_out": w_win_out[l].astype(BF16), "w_out": w_out[l].astype(BF16),
            "w_peer_q_t": w_peer_q[l].T.astype(BF16),
            "peer_subkeys": peer_subkeys[l].astype(BF16),
            "peer_u": peer_u[l].astype(BF16),
            "peer_v_t": peer_v[l].T.astype(BF16),
        }
        mod_ctx = ada[l, 0:1]
        mod_lat = ada[l, 1:1 + b_lat]
        xp, kvf = _layer(xp, mod_ctx, lp, consts, b_ctx, s_ctx, None)
        kv_out.append(kvf)
        cached = tuple(
            _pad_heads(cc[:, l].reshape(b_lat * past, ATTN_KV))
            for cc in (cache_attn_k, cache_attn_v, cache_win_k, cache_win_v))
        xs, _ = _layer(xs, mod_lat, lp, consts, b_lat, s_lat, cached)

    y_prompt = _final_norm(xp, final_norm_g[None, :]).reshape(b_ctx, s_ctx, D_MODEL)
    y_sample = _final_norm(xs, final_norm_g[None, :]).reshape(b_lat, s_lat, D_MODEL)
    outs = []
    for j in range(4):
        per_layer = [kvf[:, j * 128:(j + 1) * 128].reshape(b_ctx, s_ctx, 2, HEAD_DIM) for kvf in kv_out]
        outs.append(jnp.stack(per_layer, axis=1))
    return (y_prompt, y_sample, outs[0], outs[1], outs[2], outs[3])
```

```python
import functools

import numpy as np
import jax
import jax.numpy as jnp
from jax import lax
from jax.experimental import pallas as pl
from jax.experimental.pallas import tpu as pltpu

F32 = jnp.float32
BF16 = jnp.bfloat16

D_MODEL = 1024
DEPTH = 4
GRID_W = 64
HEAD_DIM = 64
FNET_GROUP_DIM = 64
FNET_WIDTH = 512
CONV_WIDTH = 512
CONV_K = 31
N_HEADS = 8
ATTN_Q = 512
ATTN_KV = 128
WINDOW = 128
ROPE_THETA = 10000.0
N_BRANCH = 4
COL_F, COL_C, COL_QKV, COL_GL = 0, 512, 1536, 3072
IN_COLS = 7168
N_KEYS = 128
N_EXPERTS = N_KEYS * N_KEYS
PEER_HEADS = 8
PEER_HALF = 128
PEER_TOPK = 16
EPS = 1e-6
NEG_INF = -1e30
LANES = 128

VMEM_LIMIT = 56 * 1024 * 1024


def _cparams(*sem):
    return pltpu.CompilerParams(dimension_semantics=sem, vmem_limit_bytes=VMEM_LIMIT)


def _dot(a, b):
    return jnp.dot(a, b, preferred_element_type=F32)


def _dot_nt(a, b):
    return lax.dot_general(a, b, (((1,), (1,)), ((), ())), preferred_element_type=F32)


def _normmod(x, g, sc, sh):
    ms = jnp.mean(x * x, axis=-1, keepdims=True)
    y = x * lax.rsqrt(ms + EPS)
    return (y * g) * (1.0 + sc) + sh


def _ada_kernel(c_ref, w_ref, b_ref, o_ref):
    c = c_ref[...]
    a = (c * jax.nn.sigmoid(c)).astype(BF16)
    o_ref[0] = _dot(a, w_ref[0].astype(BF16)) + b_ref[0]


def _ada_all(cond16, w_ada, b_ada):
    tn = 1536
    n = w_ada.shape[2]
    return pl.pallas_call(
        _ada_kernel,
        grid=(DEPTH, n // tn),
        in_specs=[
            pl.BlockSpec((16, D_MODEL), lambda l, j: (0, 0)),
            pl.BlockSpec((1, D_MODEL, tn), lambda l, j: (l, 0, j)),
            pl.BlockSpec((1, 1, tn), lambda l, j: (l, 0, j)),
        ],
        out_specs=pl.BlockSpec((1, 16, tn), lambda l, j: (l, 0, j)),
        out_shape=jax.ShapeDtypeStruct((DEPTH, 16, n), F32),
        compiler_params=_cparams("arbitrary", "arbitrary"),
        name="ada",
    )(cond16, w_ada, b_ada.reshape(DEPTH, 1, n))


def _in_plain_kernel(x_ref, mod_ref, g_ref, w_ref, o_ref, h_ref):
    @pl.when(pl.program_id(1) == 0)
    def _():
        h = _normmod(x_ref[...], g_ref[...], mod_ref[0, 1:2, :], mod_ref[0, 0:1, :])
        h_ref[...] = h.astype(BF16)

    o_ref[...] = _dot(h_ref[...], w_ref[...]).astype(o_ref.dtype)


def _in_plain(x, mod, g, w_in, seq, col_start, ncols, tn, out_dtype, tt=512):
    t = x.shape[0]
    nb = mod.shape[0]
    tiles_per_batch = seq // tt if nb > 1 else None
    mod_idx = (lambda i, j: (i // tiles_per_batch, 0, 0)) if nb > 1 else (lambda i, j: (0, 0, 0))
    c0 = col_start // tn
    return pl.pallas_call(
        _in_plain_kernel,
        grid=(t // tt, ncols // tn),
        in_specs=[
            pl.BlockSpec((tt, D_MODEL), lambda i, j: (i, 0)),
            pl.BlockSpec((1, 6, D_MODEL), mod_idx),
            pl.BlockSpec((1, D_MODEL), lambda i, j: (0, 0)),
            pl.BlockSpec((D_MODEL, tn), lambda i, j: (0, c0 + j)),
        ],
        out_specs=pl.BlockSpec((tt, tn), lambda i, j: (i, j)),
        out_shape=jax.ShapeDtypeStruct((t, ncols), out_dtype),
        scratch_shapes=[pltpu.VMEM((tt, D_MODEL), BF16)],
        compiler_params=_cparams("arbitrary", "arbitrary"),
        name="in_plain",
    )(x, mod, g, w_in)


def _seg_mean_sq(x, bd):
    sq = x * x
    hi = sq.astype(BF16)
    lo = (sq - hi.astype(F32)).astype(BF16)
    return (_dot(hi, bd) + _dot(lo, bd)) * (1.0 / HEAD_DIM)


def _rope(x, cos, sin_signed):
    w = x.shape[1]
    fwd = pltpu.roll(x, HEAD_DIM // 2, axis=1)
    bwd = pltpu.roll(x, w - HEAD_DIM // 2, axis=1)
    lane = lax.broadcasted_iota(jnp.int32, x.shape, 1)
    first = (lane % HEAD_DIM) < (HEAD_DIM // 2)
    partner = jnp.where(first, bwd, fwd)
    return x * cos + partner * sin_signed


def _store_head_padded(o_ref, k):
    kr = pltpu.roll(k, HEAD_DIM, axis=1)
    lane = lax.broadcasted_iota(jnp.int32, k.shape, 1)
    lo = lane < HEAD_DIM
    zero = jnp.zeros_like(k)
    o_ref[:, 0:128] = jnp.where(lo, k, zero).astype(BF16)
    o_ref[:, 128:256] = jnp.where(lo, zero, kr).astype(BF16)
    o_ref[:, 256:384] = jnp.where(lo, kr, zero).astype(BF16)
    o_ref[:, 384:512] = jnp.where(lo, zero, k).astype(BF16)


def _in_qkv_kernel(*refs, use_rope):
    if use_rope:
        (x_ref, mod_ref, g_ref, w_ref, bd_ref, qg_ref, kg_ref, cos_ref, sin_ref,
         qa_ref, qw_ref, kxa_ref, vxa_ref, kxw_ref, vxw_ref, kvf_ref) = refs
    else:
        (x_ref, mod_ref, g_ref, w_ref, bd_ref, qg_ref, kg_ref,
         qa_ref, qw_ref, kxa_ref, vxa_ref, kxw_ref, vxw_ref, kvf_ref) = refs
    h = _normmod(x_ref[...], g_ref[...], mod_ref[0, 1:2, :], mod_ref[0, 0:1, :]).astype(BF16)
    acc = _dot(h, w_ref[...])
    aq = acc[:, 0:512]
    ak = acc[:, 512:640]
    av = acc[:, 640:768]
    wq = acc[:, 768:1280]
    wk = acc[:, 1280:1408]
    wv = acc[:, 1408:1536]
    bd = bd_ref[...]
    aq = aq * lax.rsqrt(_seg_mean_sq(aq, bd) + EPS) * qg_ref[...]
    ak = ak * lax.rsqrt(_seg_mean_sq(ak, bd[0:128, 0:128]) + EPS) * kg_ref[...]
    if use_rope:
        cos = cos_ref[...]
        sin = sin_ref[...]
        aq = _rope(aq, cos, sin)
        wq = _rope(wq, cos, sin)
        ak = _rope(ak, cos[:, 0:128], sin[:, 0:128])
        wk = _rope(wk, cos[:, 0:128], sin[:, 0:128])
    scale = HEAD_DIM ** -0.5
    qa_ref[...] = (aq * scale).astype(BF16)
    qw_ref[...] = (wq * scale).astype(BF16)
    _store_head_padded(kxa_ref, ak)
    _store_head_padded(vxa_ref, av)
    _store_head_padded(kxw_ref, wk)
    _store_head_padded(vxw_ref, wv)
    kvf_ref[:, 0:128] = ak
    kvf_ref[:, 128:256] = av
    kvf_ref[:, 256:384] = wk
    kvf_ref[:, 384:512] = wv


def _in_qkv(x, mod, g, w_in, seq, bd, qg8, kg2, rope_tabs, tt=256):
    t = x.shape[0]
    nb = mod.shape[0]
    tpb = seq // tt
    mod_idx = (lambda i: (i // tpb, 0, 0)) if nb > 1 else (lambda i: (0, 0, 0))
    use_rope = rope_tabs is not None
    ncol = 1536
    in_specs = [
        pl.BlockSpec((tt, D_MODEL), lambda i: (i, 0)),
        pl.BlockSpec((1, 6, D_MODEL), mod_idx),
        pl.BlockSpec((1, D_MODEL), lambda i: (0, 0)),
        pl.BlockSpec((D_MODEL, ncol), lambda i: (0, COL_QKV // ncol)),
        pl.BlockSpec((512, 512), lambda i: (0, 0)),
        pl.BlockSpec((1, 512), lambda i: (0, 0)),
        pl.BlockSpec((1, 128), lambda i: (0, 0)),
    ]
    args = [x, mod, g, w_in, bd, qg8, kg2]
    if use_rope:
        in_specs += [pl.BlockSpec((tt, 512), lambda i: (i % tpb, 0))] * 2
        args += list(rope_tabs)
    tok = pl.BlockSpec((tt, 512), lambda i: (i, 0))
    return pl.pallas_call(
        functools.partial(_in_qkv_kernel, use_rope=use_rope),
        grid=(t // tt,),
        in_specs=in_specs,
        out_specs=[tok] * 7,
        out_shape=[jax.ShapeDtypeStruct((t, 512), BF16)] * 6 + [jax.ShapeDtypeStruct((t, 512), F32)],
        compiler_params=_cparams("arbitrary"),
        name="in_qkv",
    )(*args)


def _attn_kernel(*refs, tq, seq, n_ctx, windowed):
    if n_ctx:
        sink_ref, q_ref, kx_ref, vx_ref, kc_ref, vc_ref, o_ref = refs
    else:
        sink_ref, q_ref, kx_ref, vx_ref, o_ref = refs
    i = pl.program_id(1)
    if windowed:
        span = tq + 2 * WINDOW
        start = jnp.clip(i * tq - WINDOW, 0, seq - span)
        start = pl.multiple_of(start, WINDOW)
        qpos = i * tq + lax.broadcasted_iota(jnp.int32, (tq, span), 0)
        kpos = start + lax.broadcasted_iota(jnp.int32, (tq, span), 1)
        valid = jnp.abs(kpos - qpos) <= WINDOW
    for jb in range(4):
        q2 = q_ref[:, jb * 128:(jb + 1) * 128]
        acc = None
        for e in range(2):
            head = jb * 2 + e
            v_idx = (jb // 2) * 2 + e
            cols = slice(v_idx * 128, (v_idx + 1) * 128)
            if windowed:
                k_loc = kx_ref[pl.ds(start, span), cols]
                v_loc = vx_ref[pl.ds(start, span), cols]
            else:
                k_loc = kx_ref[:, cols]
                v_loc = vx_ref[:, cols]
            s = _dot_nt(q2, k_loc)
            if windowed:
                s = jnp.where(valid, s, NEG_INF)
            sink = sink_ref[head]
            m = jnp.maximum(jnp.max(s, axis=-1, keepdims=True), sink)
            if n_ctx:
                s_c = _dot_nt(q2, kc_ref[:, cols])
                m = jnp.maximum(m, jnp.max(s_c, axis=-1, keepdims=True))
            p = jnp.exp(s - m)
            l = jnp.sum(p, axis=-1, keepdims=True) + jnp.exp(sink - m)
            o = _dot(p.astype(BF16), v_loc)
            if n_ctx:
                p_c = jnp.exp(s_c - m)
                l = l + jnp.sum(p_c, axis=-1, keepdims=True)
                o = o + _dot(p_c.astype(BF16), vc_ref[:, cols])
            o = o / l
            acc = o if acc is None else acc + o
        o_ref[:, jb * 128:(jb + 1) * 128] = acc.astype(o_ref.dtype)


def _attention(q, kx, vx, sink, batch, seq, ctx=None, windowed=False, tq=256):
    n_ctx = 0 if ctx is None else ctx[0].shape[0] // batch
    nq = seq // tq
    in_specs = [
        pl.BlockSpec(memory_space=pltpu.SMEM),
        pl.BlockSpec((tq, 512), lambda b, i: (b * nq + i, 0)),
        pl.BlockSpec((seq, 512), lambda b, i: (b, 0)),
        pl.BlockSpec((seq, 512), lambda b, i: (b, 0)),
    ]
    args = [sink, q, kx, vx]
    if n_ctx:
        in_specs += [pl.BlockSpec((n_ctx, 512), lambda b, i: (b, 0))] * 2
        args += list(ctx)
    return pl.pallas_call(
        functools.partial(_attn_kernel, tq=tq, seq=seq, n_ctx=n_ctx, windowed=windowed),
        grid=(batch, nq),
        in_specs=in_specs,
        out_specs=pl.BlockSpec((tq, 512), lambda b, i: (b * nq + i, 0)),
        out_shape=jax.ShapeDtypeStruct((batch * seq, 512), BF16),
        compiler_params=_cparams("arbitrary", "arbitrary"),
        name="attn_win" if windowed else "attn",
    )(*args)


def _fourier_kernel(f_ref, ccs_ref, dft_ref, o_ref, z_ref, *, seq, scale):
    @pl.when(pl.program_id(1) == 0)
    def _():
        z = _dot(f_ref[...], ccs_ref[...])
        z_ref[0:seq, :] = z[:, 0:FNET_WIDTH].astype(BF16)
        z_ref[seq:2 * seq, :] = z[:, FNET_WIDTH:].astype(BF16)

    o_ref[...] = (_dot(dft_ref[...], z_ref[...]) * scale).astype(o_ref.dtype)


def _fourier(f_in, ccs, dft, batch, seq, tr):
    nr = seq // tr
    return pl.pallas_call(
        functools.partial(_fourier_kernel, seq=seq, scale=float((seq * FNET_GROUP_DIM) ** -0.5)),
        grid=(batch, nr),
        in_specs=[
            pl.BlockSpec((seq, FNET_WIDTH), lambda b, i: (b, 0)),
            pl.BlockSpec((FNET_WIDTH, 2 * FNET_WIDTH), lambda b, i: (0, 0)),
            pl.BlockSpec((tr, 2 * seq), lambda b, i: (i, 0)),
        ],
        out_specs=pl.BlockSpec((tr, FNET_WIDTH), lambda b, i: (b * nr + i, 0)),
        out_shape=jax.ShapeDtypeStruct((batch * seq, FNET_WIDTH), BF16),
        scratch_shapes=[pltpu.VMEM((2 * seq, FNET_WIDTH), BF16)],
        compiler_params=_cparams("arbitrary", "arbitrary"),
        name="fourier",
    )(f_in, ccs, dft)


def _dft_tables(seq):
    k = np.arange(seq, dtype=np.int64)
    ang = 2.0 * np.pi * ((k[:, None] * k[None, :]) % seq).astype(np.float64) / seq
    return np.concatenate([np.cos(ang), -np.sin(ang)], axis=1).astype(np.float32)


def _channel_dft_tables():
    g = FNET_GROUP_DIM
    k = np.arange(g, dtype=np.int64)
    ang = 2.0 * np.pi * ((k[:, None] * k[None, :]) % g).astype(np.float64) / g
    eye = np.eye(FNET_WIDTH // g)
    return np.concatenate([np.kron(eye, np.cos(ang)), np.kron(eye, np.sin(ang))], axis=1).astype(np.float32)


CONV_PAD = 16
CONV_ROWS = 128


def _conv_kernel(c_ref, dw_ref, b_ref, lg_ref, lb_ref, o_ref, hp_ref, *, seq):
    zeros = jnp.zeros((CONV_PAD, CONV_WIDTH), F32)
    hp_ref[0:CONV_PAD, :] = zeros
    hp_ref[seq + CONV_PAD:seq + 2 * CONV_PAD, :] = zeros

    def glu_body(r, carry):
        base = pl.multiple_of(r * CONV_ROWS, CONV_ROWS)
        a = c_ref[pl.ds(base, CONV_ROWS), 0:CONV_WIDTH]
        gt = c_ref[pl.ds(base, CONV_ROWS), CONV_WIDTH:2 * CONV_WIDTH]
        hp_ref[pl.ds(base + CONV_PAD, CONV_ROWS), :] = a * jax.nn.sigmoid(gt)
        return carry

    lax.fori_loop(0, seq // CONV_ROWS, glu_body, 0)

    blk_rows = CONV_ROWS + 2 * CONV_PAD

    def conv_body(r, carry):
        base = pl.multiple_of(r * CONV_ROWS, CONV_ROWS)
        outs = []
        for cch in range(CONV_WIDTH // LANES):
            ls = slice(cch * LANES, (cch + 1) * LANES)
            blk = hp_ref[pl.ds(base, blk_rows), ls]
            acc = jnp.zeros((CONV_ROWS, LANES), F32)
            for s in range(8):
                blk_s = blk if s == 0 else pltpu.roll(blk, blk_rows - s, axis=0)
                for m in range(4):
                    off = 8 * m + s
                    k = off - 1
                    if 0 <= k < CONV_K:
                        acc = acc + blk_s[8 * m:8 * m + CONV_ROWS, :] * dw_ref[k:k + 1, ls]
            outs.append(acc)
        y = jnp.concatenate(outs, axis=1) + b_ref[...]
        mu = jnp.mean(y, axis=-1, keepdims=True)
        yc = y - mu
        var = jnp.mean(yc * yc, axis=-1, keepdims=True)
        hn = yc * lax.rsqrt(var + EPS) * lg_ref[...] + lb_ref[...]
        o_ref[pl.ds(base, CONV_ROWS), :] = (hn * jax.nn.sigmoid(hn)).astype(o_ref.dtype)
        return carry

    lax.fori_loop(0, seq // CONV_ROWS, conv_body, 0)


def _conv(c_in, dw, b, lg, lb, batch, seq):
    vec = pl.BlockSpec((1, CONV_WIDTH), lambda bi: (0, 0))
    return pl.pallas_call(
        functools.partial(_conv_kernel, seq=seq),
        grid=(batch,),
        in_specs=[
            pl.BlockSpec((seq, 2 * CONV_WIDTH), lambda bi: (bi, 0)),
            pl.BlockSpec((CONV_K, CONV_WIDTH), lambda bi: (0, 0)),
            vec, vec, vec,
        ],
        out_specs=pl.BlockSpec((seq, CONV_WIDTH), lambda bi: (bi, 0)),
        out_shape=jax.ShapeDtypeStruct((batch * seq, CONV_WIDTH), BF16),
        scratch_shapes=[pltpu.VMEM((seq + 2 * CONV_PAD, CONV_WIDTH), F32)],
        compiler_params=_cparams("arbitrary"),
        name="conv",
    )(c_in, dw, b, lg, lb)


def _out_kernel(f_ref, c_ref, a_ref, w_ref, gl_ref, x_ref, mod_ref, bg_ref, g2_ref,
                wf_ref, wc_ref, wa_ref, ww_ref, wo_ref, x1_ref, h2t_ref):
    merged = None
    for i, (br, wt) in enumerate(((f_ref, wf_ref), (c_ref, wc_ref), (a_ref, wa_ref), (w_ref, ww_ref))):
        y = _dot(br[...], wt[...])
        gate = jax.nn.sigmoid(gl_ref[:, i * D_MODEL:(i + 1) * D_MODEL] + bg_ref[i:i + 1, :])
        merged = gate * y if merged is None else merged + gate * y
    mix = _dot(merged.astype(BF16), wo_ref[...])
    x1 = x_ref[...] + mod_ref[0, 2:3, :] * mix
    x1_ref[...] = x1
    h2 = _normmod(x1, g2_ref[...], mod_ref[0, 4:5, :], mod_ref[0, 3:4, :])
    h2t_ref[...] = h2.T.astype(BF16)


def _out_proj(f, c, a, w, gl, x, mod, bg, g2, wf, wc, wa, ww, wo, seq, tt=256):
    t = x.shape[0]
    nb = mod.shape[0]
    tpb = seq // tt
    mod_idx = (lambda i: (i // tpb, 0, 0)) if nb > 1 else (lambda i: (0, 0, 0))
    br = pl.BlockSpec((tt, 512), lambda i: (i, 0))
    wbr = pl.BlockSpec((512, D_MODEL), lambda i: (0, 0))
    return pl.pallas_call(
        _out_kernel,
        grid=(t // tt,),
        in_specs=[
            br, br, br, br,
            pl.BlockSpec((tt, N_BRANCH * D_MODEL), lambda i: (i, 0)),
            pl.BlockSpec((tt, D_MODEL), lambda i: (i, 0)),
            pl.BlockSpec((1, 6, D_MODEL), mod_idx),
            pl.BlockSpec((N_BRANCH, D_MODEL), lambda i: (0, 0)),
            pl.BlockSpec((1, D_MODEL), lambda i: (0, 0)),
            wbr, wbr, wbr, wbr,
            pl.BlockSpec((D_MODEL, D_MODEL), lambda i: (0, 0)),
        ],
        out_specs=[
            pl.BlockSpec((tt, D_MODEL), lambda i: (i, 0)),
            pl.BlockSpec((D_MODEL, tt), lambda i: (0, i)),
        ],
        out_shape=[jax.ShapeDtypeStruct((t, D_MODEL), F32), jax.ShapeDtypeStruct((D_MODEL, t), BF16)],
        compiler_params=_cparams("arbitrary"),
        name="out_proj",
    )(f, c, a, w, gl, x, mod, bg, g2, wf, wc, wa, ww, wo)


NOT_TOP = 99.0


def _route_kernel(h2t_ref, wq_ref, sk_ref, a0_ref, n0_ref, a1_ref, r1_ref,
                  qt_ref, sc_ref, srt_ref, rnk_ref, *, tt):
    ncol = tt // LANES
    qt_ref[...] = _dot(wq_ref[...], h2t_ref[...]).astype(BF16)
    for hp in range(2 * PEER_HEADS):
        sc_ref[hp] = _dot(sk_ref[hp % 2], qt_ref[hp * PEER_HALF:(hp + 1) * PEER_HALF, :])

    key_iota = lax.broadcasted_iota(jnp.int32, (N_KEYS, LANES), 0).astype(F32)

    def top16(s):
        rank = jnp.full((N_KEYS, LANES), NOT_TOP, F32)
        tops = []
        for it in range(PEER_TOPK):
            m = jnp.max(s, axis=0, keepdims=True)
            first = jnp.min(jnp.where(s == m, key_iota, float(N_KEYS)), axis=0, keepdims=True)
            hit = key_iota == first
            s = jnp.where(hit, -jnp.inf, s)
            rank = jnp.where(hit, float(it), rank)
            tops.append(m)
        return jnp.concatenate(tops, axis=0), rank

    def stage1(hp, carry):
        for col in range(ncol):
            ls = slice(col * LANES, (col + 1) * LANES)
            tops, rank = top16(sc_ref[hp, :, ls])
            srt_ref[hp, :, ls] = tops
            rnk_ref[hp, :, ls] = rank
        return carry

    lax.fori_loop(0, 2 * PEER_HEADS, stage1, 0)

    sub8 = lax.broadcasted_iota(jnp.int32, (8, LANES), 0).astype(F32)

    def _tree(op, xs):
        xs = list(xs)
        while len(xs) > 1:
            xs = [op(xs[i], xs[i + 1]) if i + 1 < len(xs) else xs[i] for i in range(0, len(xs), 2)]
        return xs[0]

    def combine(h, ls):
        s0 = srt_ref[2 * h, :, ls]
        s1 = srt_ref[2 * h + 1, :, ls]
        cands = [s0[0:1, :] + s1[0:8, :], s0[0:1, :] + s1[8:16, :]]
        poss = [sub8, sub8 + 8.0]
        for a in range(1, 8):
            c = s0[a:a + 1, :] + s1[0:8, :]
            cands.append(jnp.where(sub8 < float(16 // (a + 1)), c, -jnp.inf))
            poss.append(sub8 + float(16 * a))
        cands.append(s0[8:16, :] + s1[0:1, :])
        poss.append((sub8 + 8.0) * 16.0)
        sels = [jnp.zeros_like(c) for c in cands]
        tops = []
        for it in range(PEER_TOPK):
            m = jnp.max(_tree(jnp.maximum, cands), axis=0, keepdims=True)
            firsts = [jnp.where(c == m, p, 999.0) for c, p in zip(cands, poss)]
            first = jnp.min(_tree(jnp.minimum, firsts), axis=0, keepdims=True)
            for j in range(len(cands)):
                hit = poss[j] == first
                cands[j] = jnp.where(hit, -jnp.inf, cands[j])
                sels[j] = jnp.where(hit, 1.0, sels[j])
            tops.append(m)
        z = _tree(jnp.add, [jnp.exp(m - tops[0]) for m in tops])
        inv_z = 1.0 / z
        n_rows = [jnp.sum(sels[0] + sels[1], axis=0, keepdims=True)]
        n_rows += [jnp.sum(sels[a + 1], axis=0, keepdims=True) for a in range(1, 8)]
        n_hi = sels[9]
        rank0 = rnk_ref[2 * h, :, ls]
        rank1 = rnk_ref[2 * h + 1, :, ls]
        n0 = jnp.zeros((N_KEYS, LANES), F32)
        for a in range(8):
            n0 = jnp.where(rank0 == float(a), n_rows[a], n0)
            n0 = jnp.where(rank0 == float(a + 8), n_hi[a:a + 1, :], n0)
        sc0 = sc_ref[2 * h, :, ls]
        sc1 = sc_ref[2 * h + 1, :, ls]
        a0 = jnp.where(rank0 < float(PEER_TOPK), jnp.exp(sc0 - s0[0:1, :]) * inv_z, 0.0)
        a1 = jnp.where(rank1 < float(PEER_TOPK), jnp.exp(sc1 - s1[0:1, :]), 0.0)
        a0_ref[h, :, ls] = a0
        n0_ref[h, :, ls] = n0
        a1_ref[h, :, ls] = a1
        r1_ref[h, :, ls] = rank1

    def stage2(h, carry):
        for col in range(ncol):
            combine(h, slice(col * LANES, (col + 1) * LANES))
        return carry

    lax.fori_loop(0, PEER_HEADS, stage2, 0)


def _route(h2t, wq_t, sk, tt=256):
    t = h2t.shape[1]
    out = pl.BlockSpec((PEER_HEADS, N_KEYS, tt), lambda i: (0, 0, i))
    shp = jax.ShapeDtypeStruct((PEER_HEADS, N_KEYS, t), F32)
    return pl.pallas_call(
        functools.partial(_route_kernel, tt=tt),
        grid=(t // tt,),
        in_specs=[
            pl.BlockSpec((D_MODEL, tt), lambda i: (0, i)),
            pl.BlockSpec((2 * PEER_HEADS * PEER_HALF, D_MODEL), lambda i: (0, 0)),
            pl.BlockSpec((2, N_KEYS, PEER_HALF), lambda i: (0, 0, 0)),
        ],
        out_specs=[out] * 4,
        out_shape=[shp] * 4,
        scratch_shapes=[
            pltpu.VMEM((2 * PEER_HEADS * PEER_HALF, tt), BF16),
            pltpu.VMEM((2 * PEER_HEADS, N_KEYS, tt), F32),
            pltpu.VMEM((2 * PEER_HEADS, PEER_TOPK, tt), F32),
            pltpu.VMEM((2 * PEER_HEADS, N_KEYS, tt), F32),
        ],
        compiler_params=_cparams("arbitrary"),
        name="peer_route",
    )(h2t, wq_t, sk)


def _gelu_tanh(x):
    c = 2.0 * 0.7978845608028654
    u = x * (c + (c * 0.044715) * (x * x))
    return x / (1.0 + jnp.exp(-u))


def _peer_kernel(h2t_ref, a0_ref, n0_ref, a1_ref, r1_ref, u_ref, vt_ref, x1_ref, mod_ref,
                 o_ref, s_ref, w_ref, acc_ref, *, tt, ec, ne, total):
    g = pl.program_id(0)
    n_i = ec // N_KEYS
    assert n_i == 8, "one expert chunk must cover exactly one sublane tile of half-0 keys"
    slot = g % 2
    other = 1 - slot
    g_gate = jnp.clip(g - 1, 0, total - 1)
    c_out = jnp.clip(g - 2, 0, total - 1) % ne

    @pl.when(g == 0)
    def _():
        s_ref[...] = jnp.zeros_like(s_ref)
        w_ref[...] = jnp.zeros_like(w_ref)

    @pl.when(c_out == 0)
    def _():
        acc_ref[...] = jnp.zeros_like(acc_ref)

    s_ref[slot] = _dot(u_ref[...], h2t_ref[...])

    i0 = pl.multiple_of((g_gate % ne) * n_i, n_i)
    for col in range(tt // LANES):
        ls = slice(col * LANES, (col + 1) * LANES)
        n0t = [n0_ref[h, pl.ds(i0, n_i), ls] for h in range(PEER_HEADS)]
        a0t = [a0_ref[h, pl.ds(i0, n_i), ls] for h in range(PEER_HEADS)]
        for il in range(n_i):
            rows = slice(il * N_KEYS, (il + 1) * N_KEYS)
            gate = None
            for h in range(PEER_HEADS):
                n0 = n0t[h][il:il + 1, :]
                a0 = a0t[h][il:il + 1, :]
                term = jnp.where(r1_ref[h, :, ls] < n0, a0 * a1_ref[h, :, ls], 0.0)
                gate = term if gate is None else gate + term
            act = _gelu_tanh(s_ref[other, rows, ls])
            w_ref[other, rows, ls] = (gate * act).astype(BF16)

    acc_ref[...] += _dot(vt_ref[...], w_ref[slot])

    @pl.when((g >= 2) & (c_out == ne - 1))
    def _():
        o_ref[...] = x1_ref[...] + mod_ref[0, 5:6, :] * acc_ref[...].T


def _peer(h2t, a0, n0, a1, r1, u, vt, x1, mod, seq, tt=512, ec=1024):
    t = x1.shape[0]
    nb = mod.shape[0]
    tpb = seq // tt
    ne = N_EXPERTS // ec
    total = (t // tt) * ne

    def st_a(g):
        return jnp.minimum(g, total - 1)

    def st_b(g):
        return jnp.clip(g - 1, 0, total - 1)

    def st_c(g):
        return jnp.clip(g - 2, 0, total - 1)

    mod_idx = (lambda g: (st_c(g) // ne // tpb, 0, 0)) if nb > 1 else (lambda g: (0, 0, 0))
    rt = pl.BlockSpec((PEER_HEADS, N_KEYS, tt), lambda g: (0, 0, st_b(g) // ne))
    return pl.pallas_call(
        functools.partial(_peer_kernel, tt=tt, ec=ec, ne=ne, total=total),
        grid=(total + 2,),
        in_specs=[
            pl.BlockSpec((D_MODEL, tt), lambda g: (0, st_a(g) // ne)),
            rt, rt, rt, rt,
            pl.BlockSpec((ec, D_MODEL), lambda g: (st_a(g) % ne, 0)),
            pl.BlockSpec((D_MODEL, ec), lambda g: (0, st_c(g) % ne)),
            pl.BlockSpec((tt, D_MODEL), lambda g: (st_c(g) // ne, 0)),
            pl.BlockSpec((1, 6, D_MODEL), mod_idx),
        ],
        out_specs=pl.BlockSpec((tt, D_MODEL), lambda g: (st_c(g) // ne, 0)),
        out_shape=jax.ShapeDtypeStruct((t, D_MODEL), F32),
        scratch_shapes=[
            pltpu.VMEM((2, ec, tt), F32),
            pltpu.VMEM((2, ec, tt), BF16),
            pltpu.VMEM((D_MODEL, tt), F32),
        ],
        compiler_params=_cparams("arbitrary"),
        name="peer_mix",
    )(h2t, a0, n0, a1, r1, u, vt, x1, mod)


def _final_norm_kernel(x_ref, g_ref, o_ref):
    x = x_ref[...]
    ms = jnp.mean(x * x, axis=-1, keepdims=True)
    o_ref[...] = x * lax.rsqrt(ms + EPS) * g_ref[...]


def _final_norm(x, g, tt=512):
    t = x.shape[0]
    return pl.pallas_call(
        _final_norm_kernel,
        grid=(t // tt,),
        in_specs=[pl.BlockSpec((tt, D_MODEL), lambda i: (i, 0)), pl.BlockSpec((1, D_MODEL), lambda i: (0, 0))],
        out_specs=pl.BlockSpec((tt, D_MODEL), lambda i: (i, 0)),
        out_shape=jax.ShapeDtypeStruct((t, D_MODEL), F32),
        compiler_params=_cparams("arbitrary"),
        name="final_norm",
    )(x, g)


def _rope_tables(seq):
    n_rows = seq // GRID_W
    row = jnp.repeat(jnp.arange(n_rows), GRID_W).astype(F32)
    col = jnp.tile(jnp.arange(GRID_W), n_rows).astype(F32)
    n_f = HEAD_DIM // 4
    inv = 1.0 / (ROPE_THETA ** (jnp.arange(n_f, dtype=F32) / n_f))
    ang = jnp.concatenate([row[:, None] * inv, col[:, None] * inv], axis=-1)
    cos, sin = jnp.cos(ang), jnp.sin(ang)
    cos_t = jnp.tile(jnp.concatenate([cos, cos], axis=-1), (1, N_HEADS))
    sin_t = jnp.tile(jnp.concatenate([-sin, sin], axis=-1), (1, N_HEADS))
    return cos_t, sin_t


def _pad_heads(kv):
    z = jnp.zeros_like(kv[:, :HEAD_DIM])
    h0, h1 = kv[:, :HEAD_DIM], kv[:, HEAD_DIM:]
    return jnp.concatenate([h0, z, z, h0, h1, z, z, h1], axis=-1).astype(BF16)


def _layer(x, mod, lp, consts, batch, seq, ctx_kv):
    is_ctx = ctx_kv is None
    g1 = lp["norm1_g"]
    w_in = lp["w_in"]
    f_in = _in_plain(x, mod, g1, w_in, seq, COL_F, 512, 512, BF16)
    c_in = _in_plain(x, mod, g1, w_in, seq, COL_C, 1024, 512, F32)
    gl = _in_plain(x, mod, g1, w_in, seq, COL_GL, 4096, 1024, F32)
    qa, qw, kxa, vxa, kxw, vxw, kvf = _in_qkv(
        x, mod, g1, w_in, seq, consts["bd"], lp["q_norm_g"], lp["k_norm_g"],
        None if is_ctx else consts["rope"])
    dft = consts["dft_ctx"] if is_ctx else consts["dft_lat"]
    f = _fourier(f_in, consts["ccs"], dft, batch, seq, min(seq, 512))
    cv = _conv(c_in, lp["conv_dw"], lp["conv_b"], lp["conv_ln_g"], lp["conv_ln_b"], batch, seq)
    no_sink = jnp.full((N_HEADS,), NEG_INF, F32)
    if is_ctx:
        a = _attention(qa, kxa, vxa, no_sink, batch, seq)
        w = _attention(qw, kxw, vxw, lp["win_sink"], batch, seq)
    else:
        ka_c, va_c, kw_c, vw_c = ctx_kv
        a = _attention(qa, kxa, vxa, no_sink, batch, seq, ctx=(ka_c, va_c))
        w = _attention(qw, kxw, vxw, lp["win_sink"], batch, seq, ctx=(kw_c, vw_c), windowed=True)
    x1, h2t = _out_proj(f, cv, a, w, gl, x, mod, lp["b_gate"], lp["norm2_g"],
                        lp["w_fourier_out"], lp["w_conv_out"], lp["w_attn_out"], lp["w_win_out"],
                        lp["w_out"], seq)
    a0, n0, a1, r1 = _route(h2t, lp["w_peer_q_t"], lp["peer_subkeys"])
    x2 = _peer(h2t, a0, n0, a1, r1, lp["peer_u"], lp["peer_v_t"], x1, mod, seq)
    return x2, kvf


def kernel(x_prompt, x_sample, cache_attn_k, cache_attn_v, cache_win_k, cache_win_v, c, c_ctx, norm1_g, norm2_g, w_ada, b_ada, w_in, b_gate, w_fourier_out, conv_dw, conv_b, conv_ln_g, conv_ln_b, w_conv_out, q_norm_g, k_norm_g, w_attn_out, win_sink, w_win_out, w_out, w_peer_q, peer_subkeys, peer_u, peer_v, final_norm_g):
    b_ctx, s_ctx, _ = x_prompt.shape
    b_lat, s_lat, _ = x_sample.shape
    past = cache_attn_k.shape[2]

    cond = jnp.concatenate([c_ctx[None, :], c, jnp.zeros((16 - 1 - b_lat, D_MODEL), F32)], axis=0)
    ada = _ada_all(cond, w_ada, b_ada).reshape(DEPTH, 16, 6, D_MODEL)

    consts = {
        "bd": jnp.asarray(np.kron(np.eye(N_HEADS), np.ones((HEAD_DIM, HEAD_DIM))), BF16),
        "ccs": jnp.asarray(_channel_dft_tables()).astype(BF16),
        "dft_ctx": jnp.asarray(_dft_tables(s_ctx)).astype(BF16),
        "dft_lat": jnp.asarray(_dft_tables(s_lat)).astype(BF16),
        "rope": _rope_tables(s_lat),
    }

    xp = x_prompt.reshape(b_ctx * s_ctx, D_MODEL)
    xs = x_sample.reshape(b_lat * s_lat, D_MODEL)
    kv_out = []
    for l in range(DEPTH):
        lp = {
            "norm1_g": norm1_g[l][None, :], "norm2_g": norm2_g[l][None, :],
            "w_in": w_in[l].astype(BF16), "b_gate": b_gate[l],
            "w_fourier_out": w_fourier_out[l].astype(BF16),
            "conv_dw": conv_dw[l], "conv_b": conv_b[l][None, :],
            "conv_ln_g": conv_ln_g[l][None, :], "conv_ln_b": conv_ln_b[l][None, :],
            "w_conv_out": w_conv_out[l].astype(BF16),
            "q_norm_g": jnp.tile(q_norm_g[l], N_HEADS)[None, :],
            "k_norm_g": jnp.tile(k_norm_g[l], 2)[None, :],
            "w_attn_out": w_attn_out[l].astype(BF16), "win_sink": win_sink[l],
            "w_win_out": w_win_out[l].astype(BF16), "w_out": w_out[l].astype(BF16),
            "w_peer_q_t": w_peer_q[l].T.astype(BF16),
            "peer_subkeys": peer_subkeys[l].astype(BF16),
            "peer_u": peer_u[l].astype(BF16),
            "peer_v_t": peer_v[l].T.astype(BF16),
        }
        mod_ctx = ada[l, 0:1]
        mod_lat = ada[l, 1:1 + b_lat]
        xp, kvf = _layer(xp, mod_ctx, lp, consts, b_ctx, s_ctx, None)
        kv_out.append(kvf)
        cached = tuple(
            _pad_heads(cc[:, l].reshape(b_lat * past, ATTN_KV))
            for cc in (cache_attn_k, cache_attn_v, cache_win_k, cache_win_v))
        xs, _ = _layer(xs, mod_lat, lp, consts, b_lat, s_lat, cached)

    y_prompt = _final_norm(xp, final_norm_g[None, :]).reshape(b_ctx, s_ctx, D_MODEL)
    y_sample = _final_norm(xs, final_norm_g[None, :]).reshape(b_lat, s_lat, D_MODEL)
    outs = []
    for j in range(4):
        per_layer = [kvf[:, j * 128:(j + 1) * 128].reshape(b_ctx, s_ctx, 2, HEAD_DIM) for kvf in kv_out]
        outs.append(jnp.stack(per_layer, axis=1))
    return (y_prompt, y_sample, outs[0], outs[1], outs[2], outs[3])
```

```python
import functools

import numpy as np
import jax
import jax.numpy as jnp
from jax import lax
from jax.experimental import pallas as pl
from jax.experimental.pallas import tpu as pltpu

F32 = jnp.float32
BF16 = jnp.bfloat16

D_MODEL = 1024
DEPTH = 4
GRID_W = 64
HEAD_DIM = 64
FNET_GROUP_DIM = 64
FNET_WIDTH = 512
CONV_WIDTH = 512
CONV_K = 31
N_HEADS = 8
ATTN_Q = 512
ATTN_KV = 128
WINDOW = 128
ROPE_THETA = 10000.0
N_BRANCH = 4
COL_F, COL_C, COL_QKV, COL_GL = 0, 512, 1536, 3072
IN_COLS = 7168
N_KEYS = 128
N_EXPERTS = N_KEYS * N_KEYS
PEER_HEADS = 8
PEER_HALF = 128
PEER_TOPK = 16
EPS = 1e-6
NEG_INF = -1e30
LANES = 128
PACK_ROWS = 16

VMEM_LIMIT = 56 * 1024 * 1024


def _cparams(*sem):
    return pltpu.CompilerParams(dimension_semantics=sem, vmem_limit_bytes=VMEM_LIMIT)


def _dot(a, b):
    return jnp.dot(a, b, preferred_element_type=F32)


def _dot_nt(a, b):
    return lax.dot_general(a, b, (((1,), (1,)), ((), ())), preferred_element_type=F32)


def _normmod(x, g, sc, sh):
    ms = jnp.mean(x * x, axis=-1, keepdims=True)
    y = x * lax.rsqrt(ms + EPS)
    return (y * g) * (1.0 + sc) + sh


def _ada_kernel(c_ref, w_ref, b_ref, o_ref):
    c = c_ref[...]
    a = (c * jax.nn.sigmoid(c)).astype(BF16)
    o_ref[0] = _dot(a, w_ref[0].astype(BF16)) + b_ref[0]


def _ada_all(cond16, w_ada, b_ada):
    tn = 1536
    n = w_ada.shape[2]
    return pl.pallas_call(
        _ada_kernel,
        grid=(DEPTH, n // tn),
        in_specs=[
            pl.BlockSpec((16, D_MODEL), lambda l, j: (0, 0)),
            pl.BlockSpec((1, D_MODEL, tn), lambda l, j: (l, 0, j)),
            pl.BlockSpec((1, 1, tn), lambda l, j: (l, 0, j)),
        ],
        out_specs=pl.BlockSpec((1, 16, tn), lambda l, j: (l, 0, j)),
        out_shape=jax.ShapeDtypeStruct((DEPTH, 16, n), F32),
        compiler_params=_cparams("arbitrary", "arbitrary"),
        name="ada",
    )(cond16, w_ada, b_ada.reshape(DEPTH, 1, n))


def _in_plain_kernel(x_ref, mod_ref, g_ref, w_ref, o_ref, h_ref):
    @pl.when(pl.program_id(1) == 0)
    def _():
        h = _normmod(x_ref[...], g_ref[...], mod_ref[0, 1:2, :], mod_ref[0, 0:1, :])
        h_ref[...] = h.astype(BF16)

    o_ref[...] = _dot(h_ref[...], w_ref[...]).astype(o_ref.dtype)


def _in_plain(x, mod, g, w_in, seq, col_start, ncols, tn, out_dtype, tt=512):
    t = x.shape[0]
    nb = mod.shape[0]
    tiles_per_batch = seq // tt if nb > 1 else None
    mod_idx = (lambda i, j: (i // tiles_per_batch, 0, 0)) if nb > 1 else (lambda i, j: (0, 0, 0))
    c0 = col_start // tn
    return pl.pallas_call(
        _in_plain_kernel,
        grid=(t // tt, ncols // tn),
        in_specs=[
            pl.BlockSpec((tt, D_MODEL), lambda i, j: (i, 0)),
            pl.BlockSpec((1, 6, D_MODEL), mod_idx),
            pl.BlockSpec((1, D_MODEL), lambda i, j: (0, 0)),
            pl.BlockSpec((D_MODEL, tn), lambda i, j: (0, c0 + j)),
        ],
        out_specs=pl.BlockSpec((tt, tn), lambda i, j: (i, j)),
        out_shape=jax.ShapeDtypeStruct((t, ncols), out_dtype),
        scratch_shapes=[pltpu.VMEM((tt, D_MODEL), BF16)],
        compiler_params=_cparams("arbitrary", "arbitrary"),
        name="in_plain",
    )(x, mod, g, w_in)


def _seg_mean_sq(x, bd):
    sq = x * x
    hi = sq.astype(BF16)
    lo = (sq - hi.astype(F32)).astype(BF16)
    return (_dot(hi, bd) + _dot(lo, bd)) * (1.0 / HEAD_DIM)


def _rope(x, cos, sin_signed):
    w = x.shape[1]
    fwd = pltpu.roll(x, HEAD_DIM // 2, axis=1)
    bwd = pltpu.roll(x, w - HEAD_DIM // 2, axis=1)
    lane = lax.broadcasted_iota(jnp.int32, x.shape, 1)
    first = (lane % HEAD_DIM) < (HEAD_DIM // 2)
    partner = jnp.where(first, bwd, fwd)
    return x * cos + partner * sin_signed


def _store_head_padded(o_ref, k):
    kr = pltpu.roll(k, HEAD_DIM, axis=1)
    lane = lax.broadcasted_iota(jnp.int32, k.shape, 1)
    lo = lane < HEAD_DIM
    zero = jnp.zeros_like(k)
    o_ref[:, 0:128] = jnp.where(lo, k, zero).astype(BF16)
    o_ref[:, 128:256] = jnp.where(lo, zero, kr).astype(BF16)
    o_ref[:, 256:384] = jnp.where(lo, kr, zero).astype(BF16)
    o_ref[:, 384:512] = jnp.where(lo, zero, k).astype(BF16)


def _in_qkv_kernel(*refs, use_rope):
    if use_rope:
        (x_ref, mod_ref, g_ref, w_ref, bd_ref, qg_ref, kg_ref, cos_ref, sin_ref,
         qa_ref, qw_ref, kxa_ref, vxa_ref, kxw_ref, vxw_ref, kvf_ref) = refs
    else:
        (x_ref, mod_ref, g_ref, w_ref, bd_ref, qg_ref, kg_ref,
         qa_ref, qw_ref, kxa_ref, vxa_ref, kxw_ref, vxw_ref, kvf_ref) = refs
    h = _normmod(x_ref[...], g_ref[...], mod_ref[0, 1:2, :], mod_ref[0, 0:1, :]).astype(BF16)
    acc = _dot(h, w_ref[...])
    aq = acc[:, 0:512]
    ak = acc[:, 512:640]
    av = acc[:, 640:768]
    wq = acc[:, 768:1280]
    wk = acc[:, 1280:1408]
    wv = acc[:, 1408:1536]
    bd = bd_ref[...]
    aq = aq * lax.rsqrt(_seg_mean_sq(aq, bd) + EPS) * qg_ref[...]
    ak = ak * lax.rsqrt(_seg_mean_sq(ak, bd[0:128, 0:128]) + EPS) * kg_ref[...]
    if use_rope:
        cos = cos_ref[...]
        sin = sin_ref[...]
        aq = _rope(aq, cos, sin)
        wq = _rope(wq, cos, sin)
        ak = _rope(ak, cos[:, 0:128], sin[:, 0:128])
        wk = _rope(wk, cos[:, 0:128], sin[:, 0:128])
    scale = HEAD_DIM ** -0.5
    qa_ref[...] = (aq * scale).astype(BF16)
    qw_ref[...] = (wq * scale).astype(BF16)
    _store_head_padded(kxa_ref, ak)
    _store_head_padded(vxa_ref, av)
    _store_head_padded(kxw_ref, wk)
    _store_head_padded(vxw_ref, wv)
    kvf_ref[:, 0:128] = ak
    kvf_ref[:, 128:256] = av
    kvf_ref[:, 256:384] = wk
    kvf_ref[:, 384:512] = wv


def _in_qkv(x, mod, g, w_in, seq, bd, qg8, kg2, rope_tabs, tt=256):
    t = x.shape[0]
    nb = mod.shape[0]
    tpb = seq // tt
    mod_idx = (lambda i: (i // tpb, 0, 0)) if nb > 1 else (lambda i: (0, 0, 0))
    use_rope = rope_tabs is not None
    ncol = 1536
    in_specs = [
        pl.BlockSpec((tt, D_MODEL), lambda i: (i, 0)),
        pl.BlockSpec((1, 6, D_MODEL), mod_idx),
        pl.BlockSpec((1, D_MODEL), lambda i: (0, 0)),
        pl.BlockSpec((D_MODEL, ncol), lambda i: (0, COL_QKV // ncol)),
        pl.BlockSpec((512, 512), lambda i: (0, 0)),
        pl.BlockSpec((1, 512), lambda i: (0, 0)),
        pl.BlockSpec((1, 128), lambda i: (0, 0)),
    ]
    args = [x, mod, g, w_in, bd, qg8, kg2]
    if use_rope:
        in_specs += [pl.BlockSpec((tt, 512), lambda i: (i % tpb, 0))] * 2
        args += list(rope_tabs)
    tok = pl.BlockSpec((tt, 512), lambda i: (i, 0))
    return pl.pallas_call(
        functools.partial(_in_qkv_kernel, use_rope=use_rope),
        grid=(t // tt,),
        in_specs=in_specs,
        out_specs=[tok] * 7,
        out_shape=[jax.ShapeDtypeStruct((t, 512), BF16)] * 6 + [jax.ShapeDtypeStruct((t, 512), F32)],
        compiler_params=_cparams("arbitrary"),
        name="in_qkv",
    )(*args)


def _attn_kernel(*refs, tq, seq, n_ctx, windowed):
    if n_ctx:
        sink_ref, q_ref, kx_ref, vx_ref, kc_ref, vc_ref, o_ref = refs
    else:
        sink_ref, q_ref, kx_ref, vx_ref, o_ref = refs
    i = pl.program_id(1)
    if windowed:
        span = tq + 2 * WINDOW
        start = jnp.clip(i * tq - WINDOW, 0, seq - span)
        start = pl.multiple_of(start, WINDOW)
        qpos = i * tq + lax.broadcasted_iota(jnp.int32, (tq, span), 0)
        kpos = start + lax.broadcasted_iota(jnp.int32, (tq, span), 1)
        valid = jnp.abs(kpos - qpos) <= WINDOW
    for jb in range(4):
        q2 = q_ref[:, jb * 128:(jb + 1) * 128]
        acc = None
        for e in range(2):
            head = jb * 2 + e
            v_idx = (jb // 2) * 2 + e
            cols = slice(v_idx * 128, (v_idx + 1) * 128)
            if windowed:
                k_loc = kx_ref[pl.ds(start, span), cols]
                v_loc = vx_ref[pl.ds(start, span), cols]
            else:
                k_loc = kx_ref[:, cols]
                v_loc = vx_ref[:, cols]
            s = _dot_nt(q2, k_loc)
            if windowed:
                s = jnp.where(valid, s, NEG_INF)
            sink = sink_ref[head]
            m = jnp.maximum(jnp.max(s, axis=-1, keepdims=True), sink)
            if n_ctx:
                s_c = _dot_nt(q2, kc_ref[:, cols])
                m = jnp.maximum(m, jnp.max(s_c, axis=-1, keepdims=True))
            p = jnp.exp(s - m)
            l = jnp.sum(p, axis=-1, keepdims=True) + jnp.exp(sink - m)
            o = _dot(p.astype(BF16), v_loc)
            if n_ctx:
                p_c = jnp.exp(s_c - m)
                l = l + jnp.sum(p_c, axis=-1, keepdims=True)
                o = o + _dot(p_c.astype(BF16), vc_ref[:, cols])
            o = o / l
            acc = o if acc is None else acc + o
        o_ref[:, jb * 128:(jb + 1) * 128] = acc.astype(o_ref.dtype)


def _attention(q, kx, vx, sink, batch, seq, ctx=None, windowed=False, tq=256):
    n_ctx = 0 if ctx is None else ctx[0].shape[0] // batch
    nq = seq // tq
    in_specs = [
        pl.BlockSpec(memory_space=pltpu.SMEM),
        pl.BlockSpec((tq, 512), lambda b, i: (b * nq + i, 0)),
        pl.BlockSpec((seq, 512), lambda b, i: (b, 0)),
        pl.BlockSpec((seq, 512), lambda b, i: (b, 0)),
    ]
    args = [sink, q, kx, vx]
    if n_ctx:
        in_specs += [pl.BlockSpec((n_ctx, 512), lambda b, i: (b, 0))] * 2
        args += list(ctx)
    return pl.pallas_call(
        functools.partial(_attn_kernel, tq=tq, seq=seq, n_ctx=n_ctx, windowed=windowed),
        grid=(batch, nq),
        in_specs=in_specs,
        out_specs=pl.BlockSpec((tq, 512), lambda b, i: (b * nq + i, 0)),
        out_shape=jax.ShapeDtypeStruct((batch * seq, 512), BF16),
        compiler_params=_cparams("arbitrary", "arbitrary"),
        name="attn_win" if windowed else "attn",
    )(*args)


def _fourier_kernel(f_ref, ccs_ref, dft_ref, o_ref, z_ref, *, seq, scale):
    @pl.when(pl.program_id(1) == 0)
    def _():
        z = _dot(f_ref[...], ccs_ref[...])
        z_ref[0:seq, :] = z[:, 0:FNET_WIDTH].astype(BF16)
        z_ref[seq:2 * seq, :] = z[:, FNET_WIDTH:].astype(BF16)

    o_ref[...] = (_dot(dft_ref[...], z_ref[...]) * scale).astype(o_ref.dtype)


def _fourier(f_in, ccs, dft, batch, seq, tr):
    nr = seq // tr
    return pl.pallas_call(
        functools.partial(_fourier_kernel, seq=seq, scale=float((seq * FNET_GROUP_DIM) ** -0.5)),
        grid=(batch, nr),
        in_specs=[
            pl.BlockSpec((seq, FNET_WIDTH), lambda b, i: (b, 0)),
            pl.BlockSpec((FNET_WIDTH, 2 * FNET_WIDTH), lambda b, i: (0, 0)),
            pl.BlockSpec((tr, 2 * seq), lambda b, i: (i, 0)),
        ],
        out_specs=pl.BlockSpec((tr, FNET_WIDTH), lambda b, i: (b * nr + i, 0)),
        out_shape=jax.ShapeDtypeStruct((batch * seq, FNET_WIDTH), BF16),
        scratch_shapes=[pltpu.VMEM((2 * seq, FNET_WIDTH), BF16)],
        compiler_params=_cparams("arbitrary", "arbitrary"),
        name="fourier",
    )(f_in, ccs, dft)


def _dft_tables(seq):
    k = np.arange(seq, dtype=np.int64)
    ang = 2.0 * np.pi * ((k[:, None] * k[None, :]) % seq).astype(np.float64) / seq
    return np.concatenate([np.cos(ang), -np.sin(ang)], axis=1).astype(np.float32)


def _channel_dft_tables():
    g = FNET_GROUP_DIM
    k = np.arange(g, dtype=np.int64)
    ang = 2.0 * np.pi * ((k[:, None] * k[None, :]) % g).astype(np.float64) / g
    eye = np.eye(FNET_WIDTH // g)
    return np.concatenate([np.kron(eye, np.cos(ang)), np.kron(eye, np.sin(ang))], axis=1).astype(np.float32)


CONV_PAD = 16
CONV_ROWS = 128


def _conv_kernel(c_ref, dw_ref, b_ref, lg_ref, lb_ref, o_ref, hp_ref, *, seq):
    zeros = jnp.zeros((CONV_PAD, CONV_WIDTH), F32)
    hp_ref[0:CONV_PAD, :] = zeros
    hp_ref[seq + CONV_PAD:seq + 2 * CONV_PAD, :] = zeros

    def glu_body(r, carry):
        base = pl.multiple_of(r * CONV_ROWS, CONV_ROWS)
        a = c_ref[pl.ds(base, CONV_ROWS), 0:CONV_WIDTH]
        gt = c_ref[pl.ds(base, CONV_ROWS), CONV_WIDTH:2 * CONV_WIDTH]
        hp_ref[pl.ds(base + CONV_PAD, CONV_ROWS), :] = a * jax.nn.sigmoid(gt)
        return carry

    lax.fori_loop(0, seq // CONV_ROWS, glu_body, 0)

    blk_rows = CONV_ROWS + 2 * CONV_PAD

    def conv_body(r, carry):
        base = pl.multiple_of(r * CONV_ROWS, CONV_ROWS)
        outs = []
        for cch in range(CONV_WIDTH // LANES):
            ls = slice(cch * LANES, (cch + 1) * LANES)
            blk = hp_ref[pl.ds(base, blk_rows), ls]
            acc = jnp.zeros((CONV_ROWS, LANES), F32)
            for s in range(8):
                blk_s = blk if s == 0 else pltpu.roll(blk, blk_rows - s, axis=0)
                for m in range(4):
                    off = 8 * m + s
                    k = off - 1
                    if 0 <= k < CONV_K:
                        acc = acc + blk_s[8 * m:8 * m + CONV_ROWS, :] * dw_ref[k:k + 1, ls]
            outs.append(acc)
        y = jnp.concatenate(outs, axis=1) + b_ref[...]
        mu = jnp.mean(y, axis=-1, keepdims=True)
        yc = y - mu
        var = jnp.mean(yc * yc, axis=-1, keepdims=True)
        hn = yc * lax.rsqrt(var + EPS) * lg_ref[...] + lb_ref[...]
        o_ref[pl.ds(base, CONV_ROWS), :] = (hn * jax.nn.sigmoid(hn)).astype(o_ref.dtype)
        return carry

    lax.fori_loop(0, seq // CONV_ROWS, conv_body, 0)


def _conv(c_in, dw, b, lg, lb, batch, seq):
    vec = pl.BlockSpec((1, CONV_WIDTH), lambda bi: (0, 0))
    return pl.pallas_call(
        functools.partial(_conv_kernel, seq=seq),
        grid=(batch,),
        in_specs=[
            pl.BlockSpec((seq, 2 * CONV_WIDTH), lambda bi: (bi, 0)),
            pl.BlockSpec((CONV_K, CONV_WIDTH), lambda bi: (0, 0)),
            vec, vec, vec,
        ],
        out_specs=pl.BlockSpec((seq, CONV_WIDTH), lambda bi: (bi, 0)),
        out_shape=jax.ShapeDtypeStruct((batch * seq, CONV_WIDTH), BF16),
        scratch_shapes=[pltpu.VMEM((seq + 2 * CONV_PAD, CONV_WIDTH), F32)],
        compiler_params=_cparams("arbitrary"),
        name="conv",
    )(c_in, dw, b, lg, lb)


def _out_kernel(f_ref, c_ref, a_ref, w_ref, gl_ref, x_ref, mod_ref, bg_ref, g2_ref,
                wf_ref, wc_ref, wa_ref, ww_ref, wo_ref, x1_ref, h2t_ref):
    merged = None
    for i, (br, wt) in enumerate(((f_ref, wf_ref), (c_ref, wc_ref), (a_ref, wa_ref), (w_ref, ww_ref))):
        y = _dot(br[...], wt[...])
        gate = jax.nn.sigmoid(gl_ref[:, i * D_MODEL:(i + 1) * D_MODEL] + bg_ref[i:i + 1, :])
        merged = gate * y if merged is None else merged + gate * y
    mix = _dot(merged.astype(BF16), wo_ref[...])
    x1 = x_ref[...] + mod_ref[0, 2:3, :] * mix
    x1_ref[...] = x1
    h2 = _normmod(x1, g2_ref[...], mod_ref[0, 4:5, :], mod_ref[0, 3:4, :])
    h2t_ref[...] = h2.T.astype(BF16)


def _out_proj(f, c, a, w, gl, x, mod, bg, g2, wf, wc, wa, ww, wo, seq, tt=256):
    t = x.shape[0]
    nb = mod.shape[0]
    tpb = seq // tt
    mod_idx = (lambda i: (i // tpb, 0, 0)) if nb > 1 else (lambda i: (0, 0, 0))
    br = pl.BlockSpec((tt, 512), lambda i: (i, 0))
    wbr = pl.BlockSpec((512, D_MODEL), lambda i: (0, 0))
    return pl.pallas_call(
        _out_kernel,
        grid=(t // tt,),
        in_specs=[
            br, br, br, br,
            pl.BlockSpec((tt, N_BRANCH * D_MODEL), lambda i: (i, 0)),
            pl.BlockSpec((tt, D_MODEL), lambda i: (i, 0)),
            pl.BlockSpec((1, 6, D_MODEL), mod_idx),
            pl.BlockSpec((N_BRANCH, D_MODEL), lambda i: (0, 0)),
            pl.BlockSpec((1, D_MODEL), lambda i: (0, 0)),
            wbr, wbr, wbr, wbr,
            pl.BlockSpec((D_MODEL, D_MODEL), lambda i: (0, 0)),
        ],
        out_specs=[
            pl.BlockSpec((tt, D_MODEL), lambda i: (i, 0)),
            pl.BlockSpec((D_MODEL, tt), lambda i: (0, i)),
        ],
        out_shape=[jax.ShapeDtypeStruct((t, D_MODEL), F32), jax.ShapeDtypeStruct((D_MODEL, t), BF16)],
        compiler_params=_cparams("arbitrary"),
        name="out_proj",
    )(f, c, a, w, gl, x, mod, bg, g2, wf, wc, wa, ww, wo)


NOT_TOP = 99.0


def _bf16_twice(x):
    bits = pltpu.bitcast(x.astype(BF16).astype(F32), jnp.uint32)
    return bits | (bits >> 16)


def _route_kernel(h2t_ref, wq_ref, sk_ref, a0_ref, n0_ref, a1_ref, r1_ref,
                  qt_ref, sc_ref, srt_ref, rnk_ref, *, tt):
    ncol = tt // LANES
    qt_ref[...] = _dot(wq_ref[...], h2t_ref[...]).astype(BF16)
    for hp in range(2 * PEER_HEADS):
        sc_ref[hp] = _dot(sk_ref[hp % 2], qt_ref[hp * PEER_HALF:(hp + 1) * PEER_HALF, :])

    key_iota = lax.broadcasted_iota(jnp.int32, (N_KEYS, LANES), 0).astype(F32)

    def top16(s):
        rank = jnp.full((N_KEYS, LANES), NOT_TOP, F32)
        tops = []
        for it in range(PEER_TOPK):
            m = jnp.max(s, axis=0, keepdims=True)
            first = jnp.min(jnp.where(s == m, key_iota, float(N_KEYS)), axis=0, keepdims=True)
            hit = key_iota == first
            s = jnp.where(hit, -jnp.inf, s)
            rank = jnp.where(hit, float(it), rank)
            tops.append(m)
        return jnp.concatenate(tops, axis=0), rank

    def stage1(hp, carry):
        for col in range(ncol):
            ls = slice(col * LANES, (col + 1) * LANES)
            tops, rank = top16(sc_ref[hp, :, ls])
            srt_ref[hp, :, ls] = tops
            rnk_ref[hp, :, ls] = rank
        return carry

    lax.fori_loop(0, 2 * PEER_HEADS, stage1, 0)

    sub8 = lax.broadcasted_iota(jnp.int32, (8, LANES), 0).astype(F32)

    def _tree(op, xs):
        xs = list(xs)
        while len(xs) > 1:
            xs = [op(xs[i], xs[i + 1]) if i + 1 < len(xs) else xs[i] for i in range(0, len(xs), 2)]
        return xs[0]

    def combine(h, ls):
        s0 = srt_ref[2 * h, :, ls]
        s1 = srt_ref[2 * h + 1, :, ls]
        cands = [s0[0:1, :] + s1[0:8, :], s0[0:1, :] + s1[8:16, :]]
        poss = [sub8, sub8 + 8.0]
        for a in range(1, 8):
            c = s0[a:a + 1, :] + s1[0:8, :]
            cands.append(jnp.where(sub8 < float(16 // (a + 1)), c, -jnp.inf))
            poss.append(sub8 + float(16 * a))
        cands.append(s0[8:16, :] + s1[0:1, :])
        poss.append((sub8 + 8.0) * 16.0)
        sels = [jnp.zeros_like(c) for c in cands]
        tops = []
        for it in range(PEER_TOPK):
            m = jnp.max(_tree(jnp.maximum, cands), axis=0, keepdims=True)
            firsts = [jnp.where(c == m, p, 999.0) for c, p in zip(cands, poss)]
            first = jnp.min(_tree(jnp.minimum, firsts), axis=0, keepdims=True)
            for j in range(len(cands)):
                hit = poss[j] == first
                cands[j] = jnp.where(hit, -jnp.inf, cands[j])
                sels[j] = jnp.where(hit, 1.0, sels[j])
            tops.append(m)
        z = _tree(jnp.add, [jnp.exp(m - tops[0]) for m in tops])
        inv_z = 1.0 / z
        n_rows = [jnp.sum(sels[0] + sels[1], axis=0, keepdims=True)]
        n_rows += [jnp.sum(sels[a + 1], axis=0, keepdims=True) for a in range(1, 8)]
        n_hi = sels[9]
        rank0 = rnk_ref[2 * h, :, ls]
        rank1 = rnk_ref[2 * h + 1, :, ls]
        n0 = jnp.zeros((N_KEYS, LANES), F32)
        for a in range(8):
            n0 = jnp.where(rank0 == float(a), n_rows[a], n0)
            n0 = jnp.where(rank0 == float(a + 8), n_hi[a:a + 1, :], n0)
        sc0 = sc_ref[2 * h, :, ls]
        sc1 = sc_ref[2 * h + 1, :, ls]
        a0 = jnp.where(rank0 < float(PEER_TOPK), jnp.exp(sc0 - s0[0:1, :]) * inv_z, 0.0)
        a1 = jnp.where(rank1 < float(PEER_TOPK), jnp.exp(sc1 - s1[0:1, :]), 0.0)
        a0_ref[h, :, ls] = _bf16_twice(a0)
        n0_ref[h, :, ls] = _bf16_twice(n0)
        a1_ref[h, :, ls] = pltpu.bitcast(a1.astype(BF16), jnp.uint32)
        r1_ref[h, :, ls] = pltpu.bitcast(rank1.astype(BF16), jnp.uint32)

    def stage2(h, carry):
        for col in range(ncol):
            combine(h, slice(col * LANES, (col + 1) * LANES))
        return carry

    lax.fori_loop(0, PEER_HEADS, stage2, 0)


def _route(h2t, wq_t, sk, tt=256):
    t = h2t.shape[1]
    out0 = pl.BlockSpec((PEER_HEADS, N_KEYS, tt), lambda i: (0, 0, i))
    out1 = pl.BlockSpec((PEER_HEADS, N_KEYS // 2, tt), lambda i: (0, 0, i))
    shp0 = jax.ShapeDtypeStruct((PEER_HEADS, N_KEYS, t), jnp.uint32)
    shp1 = jax.ShapeDtypeStruct((PEER_HEADS, N_KEYS // 2, t), jnp.uint32)
    return pl.pallas_call(
        functools.partial(_route_kernel, tt=tt),
        grid=(t // tt,),
        in_specs=[
            pl.BlockSpec((D_MODEL, tt), lambda i: (0, i)),
            pl.BlockSpec((2 * PEER_HEADS * PEER_HALF, D_MODEL), lambda i: (0, 0)),
            pl.BlockSpec((2, N_KEYS, PEER_HALF), lambda i: (0, 0, 0)),
        ],
        out_specs=[out0, out0, out1, out1],
        out_shape=[shp0, shp0, shp1, shp1],
        scratch_shapes=[
            pltpu.VMEM((2 * PEER_HEADS * PEER_HALF, tt), BF16),
            pltpu.VMEM((2 * PEER_HEADS, N_KEYS, tt), F32),
            pltpu.VMEM((2 * PEER_HEADS, PEER_TOPK, tt), F32),
            pltpu.VMEM((2 * PEER_HEADS, N_KEYS, tt), F32),
        ],
        compiler_params=_cparams("arbitrary"),
        name="peer_route",
    )(h2t, wq_t, sk)


def _gelu_tanh(x):
    c = 2.0 * 0.7978845608028654
    neg_u = x * ((-c * 0.044715) * (x * x) - c)
    return x / (1.0 + jnp.exp(neg_u))


def _peer_step(h2t_ref, a0_ref, n0_ref, a1_ref, r1_ref, u_ref, vt_ref, acc_ref,
               s_new, s_cur, w_cur, w_done, i0, *, tt, n_i):
    ncol = tt // LANES
    zero = jnp.zeros((PACK_ROWS, LANES), BF16)
    for il in range(n_i):
        mr = slice(il * N_KEYS, (il + 1) * N_KEYS)
        s_new[mr, :] = _dot(u_ref[mr, :], h2t_ref[...])
        for col in range(ncol):
            if col == ncol // 2:
                acc_ref[mr, :] += _dot(vt_ref[mr, :], w_done[...])
            ls = slice(col * LANES, (col + 1) * LANES)
            n0 = [pltpu.bitcast(jnp.broadcast_to(n0_ref[h, pl.ds(i0, n_i), ls][il:il + 1, :], (8, LANES)), BF16)
                  for h in range(PEER_HEADS)]
            a0 = [pltpu.bitcast(jnp.broadcast_to(a0_ref[h, pl.ds(i0, n_i), ls][il:il + 1, :], (8, LANES)), BF16)
                  for h in range(PEER_HEADS)]
            for k in range(N_KEYS // PACK_ROWS):
                pr = slice(k * PACK_ROWS // 2, (k + 1) * PACK_ROWS // 2)
                gate = None
                for h in range(PEER_HEADS):
                    r1 = pltpu.bitcast(r1_ref[h, pr, ls], BF16)
                    a1 = pltpu.bitcast(a1_ref[h, pr, ls], BF16)
                    term = jnp.where(r1 < n0[h], a0[h] * a1, zero)
                    gate = term if gate is None else gate + term
                rows = slice(il * N_KEYS + k * PACK_ROWS, il * N_KEYS + (k + 1) * PACK_ROWS)
                act = _gelu_tanh(s_cur[rows, ls])
                w_cur[rows, ls] = gate * act.astype(BF16)


def _peer_kernel(h2t_ref, a0_ref, n0_ref, a1_ref, r1_ref, u_ref, vt_ref, x1_ref, mod_ref,
                 o_ref, s0_ref, s1_ref, w0_ref, w1_ref, acc_ref, *, tt, ec, ne, total):
    g = pl.program_id(0)
    n_i = ec // N_KEYS
    assert n_i == 8, "one expert chunk must cover exactly one sublane tile of half-0 keys"
    g_gate = jnp.clip(g - 1, 0, total - 1)
    c_out = jnp.clip(g - 2, 0, total - 1) % ne
    i0 = pl.multiple_of((g_gate % ne) * n_i, n_i)

    @pl.when(g == 0)
    def _():
        s1_ref[...] = jnp.zeros_like(s1_ref)
        w0_ref[...] = jnp.zeros_like(w0_ref)
        w1_ref[...] = jnp.zeros_like(w1_ref)

    @pl.when(c_out == 0)
    def _():
        acc_ref[...] = jnp.zeros_like(acc_ref)

    step = functools.partial(_peer_step, h2t_ref, a0_ref, n0_ref, a1_ref, r1_ref, u_ref, vt_ref, acc_ref,
                             tt=tt, n_i=n_i)

    @pl.when(g % 2 == 0)
    def _():
        step(s0_ref, s1_ref, w1_ref, w0_ref, i0)

    @pl.when(g % 2 == 1)
    def _():
        step(s1_ref, s0_ref, w0_ref, w1_ref, i0)

    @pl.when((g >= 2) & (c_out == ne - 1))
    def _():
        o_ref[...] = x1_ref[...] + mod_ref[0, 5:6, :] * acc_ref[...].T


def _peer(h2t, a0, n0, a1, r1, u, vt, x1, mod, seq, tt=512, ec=1024):
    t = x1.shape[0]
    nb = mod.shape[0]
    tpb = seq // tt
    ne = N_EXPERTS // ec
    total = (t // tt) * ne
    assert ec == D_MODEL, "the kernel walks U-chunk rows and out^T rows with one 128-row block index"

    def st_a(g):
        return jnp.minimum(g, total - 1)

    def st_b(g):
        return jnp.clip(g - 1, 0, total - 1)

    def st_c(g):
        return jnp.clip(g - 2, 0, total - 1)

    mod_idx = (lambda g: (st_c(g) // ne // tpb, 0, 0)) if nb > 1 else (lambda g: (0, 0, 0))
    rt0 = pl.BlockSpec((PEER_HEADS, N_KEYS, tt), lambda g: (0, 0, st_b(g) // ne))
    rt1 = pl.BlockSpec((PEER_HEADS, N_KEYS // 2, tt), lambda g: (0, 0, st_b(g) // ne))
    return pl.pallas_call(
        functools.partial(_peer_kernel, tt=tt, ec=ec, ne=ne, total=total),
        grid=(total + 2,),
        in_specs=[
            pl.BlockSpec((D_MODEL, tt), lambda g: (0, st_a(g) // ne)),
            rt0, rt0, rt1, rt1,
            pl.BlockSpec((ec, D_MODEL), lambda g: (st_a(g) % ne, 0)),
            pl.BlockSpec((D_MODEL, ec), lambda g: (0, st_c(g) % ne)),
            pl.BlockSpec((tt, D_MODEL), lambda g: (st_c(g) // ne, 0)),
            pl.BlockSpec((1, 6, D_MODEL), mod_idx),
        ],
        out_specs=pl.BlockSpec((tt, D_MODEL), lambda g: (st_c(g) // ne, 0)),
        out_shape=jax.ShapeDtypeStruct((t, D_MODEL), F32),
        scratch_shapes=[
            pltpu.VMEM((ec, tt), F32), pltpu.VMEM((ec, tt), F32),
            pltpu.VMEM((ec, tt), BF16), pltpu.VMEM((ec, tt), BF16),
            pltpu.VMEM((D_MODEL, tt), F32),
        ],
        compiler_params=_cparams("arbitrary"),
        name="peer_mix",
    )(h2t, a0, n0, a1, r1, u, vt, x1, mod)


def _final_norm_kernel(x_ref, g_ref, o_ref):
    x = x_ref[...]
    ms = jnp.mean(x * x, axis=-1, keepdims=True)
    o_ref[...] = x * lax.rsqrt(ms + EPS) * g_ref[...]


def _final_norm(x, g, tt=512):
    t = x.shape[0]
    return pl.pallas_call(
        _final_norm_kernel,
        grid=(t // tt,),
        in_specs=[pl.BlockSpec((tt, D_MODEL), lambda i: (i, 0)), pl.BlockSpec((1, D_MODEL), lambda i: (0, 0))],
        out_specs=pl.BlockSpec((tt, D_MODEL), lambda i: (i, 0)),
        out_shape=jax.ShapeDtypeStruct((t, D_MODEL), F32),
        compiler_params=_cparams("arbitrary"),
        name="final_norm",
    )(x, g)


def _rope_tables(seq):
    n_rows = seq // GRID_W
    row = jnp.repeat(jnp.arange(n_rows), GRID_W).astype(F32)
    col = jnp.tile(jnp.arange(GRID_W), n_rows).astype(F32)
    n_f = HEAD_DIM // 4
    inv = 1.0 / (ROPE_THETA ** (jnp.arange(n_f, dtype=F32) / n_f))
    ang = jnp.concatenate([row[:, None] * inv, col[:, None] * inv], axis=-1)
    cos, sin = jnp.cos(ang), jnp.sin(ang)
    cos_t = jnp.tile(jnp.concatenate([cos, cos], axis=-1), (1, N_HEADS))
    sin_t = jnp.tile(jnp.concatenate([-sin, sin], axis=-1), (1, N_HEADS))
    return cos_t, sin_t


def _pad_heads(kv):
    z = jnp.zeros_like(kv[:, :HEAD_DIM])
    h0, h1 = kv[:, :HEAD_DIM], kv[:, HEAD_DIM:]
    return jnp.concatenate([h0, z, z, h0, h1, z, z, h1], axis=-1).astype(BF16)


def _layer(x, mod, lp, consts, batch, seq, ctx_kv):
    is_ctx = ctx_kv is None
    g1 = lp["norm1_g"]
    w_in = lp["w_in"]
    f_in = _in_plain(x, mod, g1, w_in, seq, COL_F, 512, 512, BF16)
    c_in = _in_plain(x, mod, g1, w_in, seq, COL_C, 1024, 512, F32)
    gl = _in_plain(x, mod, g1, w_in, seq, COL_GL, 4096, 1024, F32)
    qa, qw, kxa, vxa, kxw, vxw, kvf = _in_qkv(
        x, mod, g1, w_in, seq, consts["bd"], lp["q_norm_g"], lp["k_norm_g"],
        None if is_ctx else consts["rope"])
    dft = consts["dft_ctx"] if is_ctx else consts["dft_lat"]
    f = _fourier(f_in, consts["ccs"], dft, batch, seq, min(seq, 512))
    cv = _conv(c_in, lp["conv_dw"], lp["conv_b"], lp["conv_ln_g"], lp["conv_ln_b"], batch, seq)
    no_sink = jnp.full((N_HEADS,), NEG_INF, F32)
    if is_ctx:
        a = _attention(qa, kxa, vxa, no_sink, batch, seq)
        w = _attention(qw, kxw, vxw, lp["win_sink"], batch, seq)
    else:
        ka_c, va_c, kw_c, vw_c = ctx_kv
        a = _attention(qa, kxa, vxa, no_sink, batch, seq, ctx=(ka_c, va_c))
        w = _attention(qw, kxw, vxw, lp["win_sink"], batch, seq, ctx=(kw_c, vw_c), windowed=True)
    x1, h2t = _out_proj(f, cv, a, w, gl, x, mod, lp["b_gate"], lp["norm2_g"],
                        lp["w_fourier_out"], lp["w_conv_out"], lp["w_attn_out"], lp["w_win_out"],
                        lp["w_out"], seq)
    a0, n0, a1, r1 = _route(h2t, lp["w_peer_q_t"], lp["peer_subkeys"])
    x2 = _peer(h2t, a0, n0, a1, r1, lp["peer_u"], lp["peer_v_t"], x1, mod, seq)
    return x2, kvf


def kernel(x_prompt, x_sample, cache_attn_k, cache_attn_v, cache_win_k, cache_win_v, c, c_ctx, norm1_g, norm2_g, w_ada, b_ada, w_in, b_gate, w_fourier_out, conv_dw, conv_b, conv_ln_g, conv_ln_b, w_conv_out, q_norm_g, k_norm_g, w_attn_out, win_sink, w_win_out, w_out, w_peer_q, peer_subkeys, peer_u, peer_v, final_norm_g):
    b_ctx, s_ctx, _ = x_prompt.shape
    b_lat, s_lat, _ = x_sample.shape
    past = cache_attn_k.shape[2]

    cond = jnp.concatenate([c_ctx[None, :], c, jnp.zeros((16 - 1 - b_lat, D_MODEL), F32)], axis=0)
    ada = _ada_all(cond, w_ada, b_ada).reshape(DEPTH, 16, 6, D_MODEL)

    consts = {
        "bd": jnp.asarray(np.kron(np.eye(N_HEADS), np.ones((HEAD_DIM, HEAD_DIM))), BF16),
        "ccs": jnp.asarray(_channel_dft_tables()).astype(BF16),
        "dft_ctx": jnp.asarray(_dft_tables(s_ctx)).astype(BF16),
        "dft_lat": jnp.asarray(_dft_tables(s_lat)).astype(BF16),
        "rope": _rope_tables(s_lat),
    }

    xp = x_prompt.reshape(b_ctx * s_ctx, D_MODEL)
    xs = x_sample.reshape(b_lat * s_lat, D_MODEL)
    kv_out = []
    for l in range(DEPTH):
        lp = {
            "norm1_g": norm1_g[l][None, :], "norm2_g": norm2_g[l][None, :],
            "w_in": w_in[l].astype(BF16), "b_gate": b_gate[l],
            "w_fourier_out": w_fourier_out[l].astype(BF16),
            "conv_dw": conv_dw[l], "conv_b": conv_b[l][None, :],
            "conv_ln_g": conv_ln_g[l][None, :], "conv_ln_b": conv_ln_b[l][None, :],
            "w_conv_out": w_conv_out[l].astype(BF16),
            "q_norm_g": jnp.tile(q_norm_g[l], N_HEADS)[None, :],
            "k_norm_g": jnp.tile(k_norm_g[l], 2)[None, :],
            "w_attn_out": w_attn_out[l].astype(BF16), "win_sink": win_sink[l],
            "w_win_out": w_win_out[l].astype(BF16), "w_out": w_out[l].astype(BF16),
            "w_peer_q_t": w_peer_q[l].T.astype(BF16),
            "peer_subkeys": peer_subkeys[l].astype(BF16),
            "peer_u": peer_u[l].astype(BF16),
            "peer_v_t": peer_v[l].T.astype(BF16),
        }
        mod_ctx = ada[l, 0:1]
        mod_lat = ada[l, 1:1 + b_lat]
        xp, kvf = _layer(xp, mod_ctx, lp, consts, b_ctx, s_ctx, None)
        kv_out.append(kvf)
        cached = tuple(
            _pad_heads(cc[:, l].reshape(b_lat * past, ATTN_KV))
            for cc in (cache_attn_k, cache_attn_v, cache_win_k, cache_win_v))
        xs, _ = _layer(xs, mod_lat, lp, consts, b_lat, s_lat, cached)

    y_prompt = _final_norm(xp, final_norm_g[None, :]).reshape(b_ctx, s_ctx, D_MODEL)
    y_sample = _final_norm(xs, final_norm_g[None, :]).reshape(b_lat, s_lat, D_MODEL)
    outs = []
    for j in range(4):
        per_layer = [kvf[:, j * 128:(j + 1) * 128].reshape(b_ctx, s_ctx, 2, HEAD_DIM) for kvf in kv_out]
        outs.append(jnp.stack(per_layer, axis=1))
    return (y_prompt, y_sample, outs[0], outs[1], outs[2], outs[3])
```

```python
import functools

import numpy as np
import jax
import jax.numpy as jnp
from jax import lax
from jax.experimental import pallas as pl
from jax.experimental.pallas import tpu as pltpu

F32 = jnp.float32
BF16 = jnp.bfloat16

D_MODEL = 1024
DEPTH = 4
GRID_W = 64
HEAD_DIM = 64
FNET_GROUP_DIM = 64
FNET_WIDTH = 512
CONV_WIDTH = 512
CONV_K = 31
N_HEADS = 8
ATTN_Q = 512
ATTN_KV = 128
WINDOW = 128
ROPE_THETA = 10000.0
N_BRANCH = 4
COL_F, COL_C, COL_QKV, COL_GL = 0, 512, 1536, 3072
IN_COLS = 7168
N_KEYS = 128
N_EXPERTS = N_KEYS * N_KEYS
PEER_HEADS = 8
PEER_HALF = 128
PEER_TOPK = 16
EPS = 1e-6
NEG_INF = -1e30
LANES = 128
PACK_ROWS = 16
MXU_PIECE_ROWS = 512

VMEM_LIMIT = 56 * 1024 * 1024


def _cparams(*sem):
    return pltpu.CompilerParams(dimension_semantics=sem, vmem_limit_bytes=VMEM_LIMIT)


def _dot(a, b):
    return jnp.dot(a, b, preferred_element_type=F32)


def _dot_nt(a, b):
    return lax.dot_general(a, b, (((1,), (1,)), ((), ())), preferred_element_type=F32)


def _normmod(x, g, sc, sh):
    ms = jnp.mean(x * x, axis=-1, keepdims=True)
    y = x * lax.rsqrt(ms + EPS)
    return (y * g) * (1.0 + sc) + sh


def _ada_kernel(c_ref, w_ref, b_ref, o_ref):
    c = c_ref[...]
    a = (c * jax.nn.sigmoid(c)).astype(BF16)
    o_ref[0] = _dot(a, w_ref[0].astype(BF16)) + b_ref[0]


def _ada_all(cond16, w_ada, b_ada):
    tn = 1536
    n = w_ada.shape[2]
    return pl.pallas_call(
        _ada_kernel,
        grid=(DEPTH, n // tn),
        in_specs=[
            pl.BlockSpec((16, D_MODEL), lambda l, j: (0, 0)),
            pl.BlockSpec((1, D_MODEL, tn), lambda l, j: (l, 0, j)),
            pl.BlockSpec((1, 1, tn), lambda l, j: (l, 0, j)),
        ],
        out_specs=pl.BlockSpec((1, 16, tn), lambda l, j: (l, 0, j)),
        out_shape=jax.ShapeDtypeStruct((DEPTH, 16, n), F32),
        compiler_params=_cparams("arbitrary", "arbitrary"),
        name="ada",
    )(cond16, w_ada, b_ada.reshape(DEPTH, 1, n))


def _in_plain_kernel(x_ref, mod_ref, g_ref, w_ref, o_ref, h_ref):
    @pl.when(pl.program_id(1) == 0)
    def _():
        h = _normmod(x_ref[...], g_ref[...], mod_ref[0, 1:2, :], mod_ref[0, 0:1, :])
        h_ref[...] = h.astype(BF16)

    o_ref[...] = _dot(h_ref[...], w_ref[...]).astype(o_ref.dtype)


def _in_plain(x, mod, g, w_in, seq, col_start, ncols, tn, out_dtype, tt=512):
    t = x.shape[0]
    nb = mod.shape[0]
    tiles_per_batch = seq // tt if nb > 1 else None
    mod_idx = (lambda i, j: (i // tiles_per_batch, 0, 0)) if nb > 1 else (lambda i, j: (0, 0, 0))
    c0 = col_start // tn
    return pl.pallas_call(
        _in_plain_kernel,
        grid=(t // tt, ncols // tn),
        in_specs=[
            pl.BlockSpec((tt, D_MODEL), lambda i, j: (i, 0)),
            pl.BlockSpec((1, 6, D_MODEL), mod_idx),
            pl.BlockSpec((1, D_MODEL), lambda i, j: (0, 0)),
            pl.BlockSpec((D_MODEL, tn), lambda i, j: (0, c0 + j)),
        ],
        out_specs=pl.BlockSpec((tt, tn), lambda i, j: (i, j)),
        out_shape=jax.ShapeDtypeStruct((t, ncols), out_dtype),
        scratch_shapes=[pltpu.VMEM((tt, D_MODEL), BF16)],
        compiler_params=_cparams("arbitrary", "arbitrary"),
        name="in_plain",
    )(x, mod, g, w_in)


def _seg_mean_sq(x, bd):
    sq = x * x
    hi = sq.astype(BF16)
    lo = (sq - hi.astype(F32)).astype(BF16)
    return (_dot(hi, bd) + _dot(lo, bd)) * (1.0 / HEAD_DIM)


def _rope(x, cos, sin_signed):
    w = x.shape[1]
    fwd = pltpu.roll(x, HEAD_DIM // 2, axis=1)
    bwd = pltpu.roll(x, w - HEAD_DIM // 2, axis=1)
    lane = lax.broadcasted_iota(jnp.int32, x.shape, 1)
    first = (lane % HEAD_DIM) < (HEAD_DIM // 2)
    partner = jnp.where(first, bwd, fwd)
    return x * cos + partner * sin_signed


def _store_head_padded(o_ref, k):
    kr = pltpu.roll(k, HEAD_DIM, axis=1)
    lane = lax.broadcasted_iota(jnp.int32, k.shape, 1)
    lo = lane < HEAD_DIM
    zero = jnp.zeros_like(k)
    o_ref[:, 0:128] = jnp.where(lo, k, zero).astype(BF16)
    o_ref[:, 128:256] = jnp.where(lo, zero, kr).astype(BF16)
    o_ref[:, 256:384] = jnp.where(lo, kr, zero).astype(BF16)
    o_ref[:, 384:512] = jnp.where(lo, zero, k).astype(BF16)


def _in_qkv_kernel(*refs, use_rope):
    if use_rope:
        (x_ref, mod_ref, g_ref, w_ref, bd_ref, qg_ref, kg_ref, cos_ref, sin_ref,
         qa_ref, qw_ref, kxa_ref, vxa_ref, kxw_ref, vxw_ref, kvf_ref) = refs
    else:
        (x_ref, mod_ref, g_ref, w_ref, bd_ref, qg_ref, kg_ref,
         qa_ref, qw_ref, kxa_ref, vxa_ref, kxw_ref, vxw_ref, kvf_ref) = refs
    h = _normmod(x_ref[...], g_ref[...], mod_ref[0, 1:2, :], mod_ref[0, 0:1, :]).astype(BF16)
    acc = _dot(h, w_ref[...])
    aq = acc[:, 0:512]
    ak = acc[:, 512:640]
    av = acc[:, 640:768]
    wq = acc[:, 768:1280]
    wk = acc[:, 1280:1408]
    wv = acc[:, 1408:1536]
    bd = bd_ref[...]
    aq = aq * lax.rsqrt(_seg_mean_sq(aq, bd) + EPS) * qg_ref[...]
    ak = ak * lax.rsqrt(_seg_mean_sq(ak, bd[0:128, 0:128]) + EPS) * kg_ref[...]
    if use_rope:
        cos = cos_ref[...]
        sin = sin_ref[...]
        aq = _rope(aq, cos, sin)
        wq = _rope(wq, cos, sin)
        ak = _rope(ak, cos[:, 0:128], sin[:, 0:128])
        wk = _rope(wk, cos[:, 0:128], sin[:, 0:128])
    scale = HEAD_DIM ** -0.5
    qa_ref[...] = (aq * scale).astype(BF16)
    qw_ref[...] = (wq * scale).astype(BF16)
    _store_head_padded(kxa_ref, ak)
    _store_head_padded(vxa_ref, av)
    _store_head_padded(kxw_ref, wk)
    _store_head_padded(vxw_ref, wv)
    kvf_ref[:, 0:128] = ak
    kvf_ref[:, 128:256] = av
    kvf_ref[:, 256:384] = wk
    kvf_ref[:, 384:512] = wv


def _in_qkv(x, mod, g, w_in, seq, bd, qg8, kg2, rope_tabs, tt=256):
    t = x.shape[0]
    nb = mod.shape[0]
    tpb = seq // tt
    mod_idx = (lambda i: (i // tpb, 0, 0)) if nb > 1 else (lambda i: (0, 0, 0))
    use_rope = rope_tabs is not None
    ncol = 1536
    in_specs = [
        pl.BlockSpec((tt, D_MODEL), lambda i: (i, 0)),
        pl.BlockSpec((1, 6, D_MODEL), mod_idx),
        pl.BlockSpec((1, D_MODEL), lambda i: (0, 0)),
        pl.BlockSpec((D_MODEL, ncol), lambda i: (0, COL_QKV // ncol)),
        pl.BlockSpec((512, 512), lambda i: (0, 0)),
        pl.BlockSpec((1, 512), lambda i: (0, 0)),
        pl.BlockSpec((1, 128), lambda i: (0, 0)),
    ]
    args = [x, mod, g, w_in, bd, qg8, kg2]
    if use_rope:
        in_specs += [pl.BlockSpec((tt, 512), lambda i: (i % tpb, 0))] * 2
        args += list(rope_tabs)
    tok = pl.BlockSpec((tt, 512), lambda i: (i, 0))
    return pl.pallas_call(
        functools.partial(_in_qkv_kernel, use_rope=use_rope),
        grid=(t // tt,),
        in_specs=in_specs,
        out_specs=[tok] * 7,
        out_shape=[jax.ShapeDtypeStruct((t, 512), BF16)] * 6 + [jax.ShapeDtypeStruct((t, 512), F32)],
        compiler_params=_cparams("arbitrary"),
        name="in_qkv",
    )(*args)


def _attn_kernel(*refs, tq, seq, n_ctx, windowed):
    if n_ctx:
        sink_ref, q_ref, kx_ref, vx_ref, kc_ref, vc_ref, o_ref = refs
    else:
        sink_ref, q_ref, kx_ref, vx_ref, o_ref = refs
    i = pl.program_id(1)
    if windowed:
        span = tq + 2 * WINDOW
        start = jnp.clip(i * tq - WINDOW, 0, seq - span)
        start = pl.multiple_of(start, WINDOW)
        qpos = i * tq + lax.broadcasted_iota(jnp.int32, (tq, span), 0)
        kpos = start + lax.broadcasted_iota(jnp.int32, (tq, span), 1)
        valid = jnp.abs(kpos - qpos) <= WINDOW
    for jb in range(4):
        q2 = q_ref[:, jb * 128:(jb + 1) * 128]
        acc = None
        for e in range(2):
            head = jb * 2 + e
            v_idx = (jb // 2) * 2 + e
            cols = slice(v_idx * 128, (v_idx + 1) * 128)
            if windowed:
                k_loc = kx_ref[pl.ds(start, span), cols]
                v_loc = vx_ref[pl.ds(start, span), cols]
            else:
                k_loc = kx_ref[:, cols]
                v_loc = vx_ref[:, cols]
            s = _dot_nt(q2, k_loc)
            if windowed:
                s = jnp.where(valid, s, NEG_INF)
            sink = sink_ref[head]
            m = jnp.maximum(jnp.max(s, axis=-1, keepdims=True), sink)
            if n_ctx:
                s_c = _dot_nt(q2, kc_ref[:, cols])
                m = jnp.maximum(m, jnp.max(s_c, axis=-1, keepdims=True))
            p = jnp.exp(s - m)
            l = jnp.sum(p, axis=-1, keepdims=True) + jnp.exp(sink - m)
            o = _dot(p.astype(BF16), v_loc)
            if n_ctx:
                p_c = jnp.exp(s_c - m)
                l = l + jnp.sum(p_c, axis=-1, keepdims=True)
                o = o + _dot(p_c.astype(BF16), vc_ref[:, cols])
            o = o / l
            acc = o if acc is None else acc + o
        o_ref[:, jb * 128:(jb + 1) * 128] = acc.astype(o_ref.dtype)


def _attention(q, kx, vx, sink, batch, seq, ctx=None, windowed=False, tq=256):
    n_ctx = 0 if ctx is None else ctx[0].shape[0] // batch
    nq = seq // tq
    in_specs = [
        pl.BlockSpec(memory_space=pltpu.SMEM),
        pl.BlockSpec((tq, 512), lambda b, i: (b * nq + i, 0)),
        pl.BlockSpec((seq, 512), lambda b, i: (b, 0)),
        pl.BlockSpec((seq, 512), lambda b, i: (b, 0)),
    ]
    args = [sink, q, kx, vx]
    if n_ctx:
        in_specs += [pl.BlockSpec((n_ctx, 512), lambda b, i: (b, 0))] * 2
        args += list(ctx)
    return pl.pallas_call(
        functools.partial(_attn_kernel, tq=tq, seq=seq, n_ctx=n_ctx, windowed=windowed),
        grid=(batch, nq),
        in_specs=in_specs,
        out_specs=pl.BlockSpec((tq, 512), lambda b, i: (b * nq + i, 0)),
        out_shape=jax.ShapeDtypeStruct((batch * seq, 512), BF16),
        compiler_params=_cparams("arbitrary", "arbitrary"),
        name="attn_win" if windowed else "attn",
    )(*args)


def _fourier_kernel(f_ref, ccs_ref, dft_ref, o_ref, z_ref, *, seq, scale):
    @pl.when(pl.program_id(1) == 0)
    def _():
        z = _dot(f_ref[...], ccs_ref[...])
        z_ref[0:seq, :] = z[:, 0:FNET_WIDTH].astype(BF16)
        z_ref[seq:2 * seq, :] = z[:, FNET_WIDTH:].astype(BF16)

    o_ref[...] = (_dot(dft_ref[...], z_ref[...]) * scale).astype(o_ref.dtype)


def _fourier(f_in, ccs, dft, batch, seq, tr):
    nr = seq // tr
    return pl.pallas_call(
        functools.partial(_fourier_kernel, seq=seq, scale=float((seq * FNET_GROUP_DIM) ** -0.5)),
        grid=(batch, nr),
        in_specs=[
            pl.BlockSpec((seq, FNET_WIDTH), lambda b, i: (b, 0)),
            pl.BlockSpec((FNET_WIDTH, 2 * FNET_WIDTH), lambda b, i: (0, 0)),
            pl.BlockSpec((tr, 2 * seq), lambda b, i: (i, 0)),
        ],
        out_specs=pl.BlockSpec((tr, FNET_WIDTH), lambda b, i: (b * nr + i, 0)),
        out_shape=jax.ShapeDtypeStruct((batch * seq, FNET_WIDTH), BF16),
        scratch_shapes=[pltpu.VMEM((2 * seq, FNET_WIDTH), BF16)],
        compiler_params=_cparams("arbitrary", "arbitrary"),
        name="fourier",
    )(f_in, ccs, dft)


def _dft_tables(seq):
    k = np.arange(seq, dtype=np.int64)
    ang = 2.0 * np.pi * ((k[:, None] * k[None, :]) % seq).astype(np.float64) / seq
    return np.concatenate([np.cos(ang), -np.sin(ang)], axis=1).astype(np.float32)


def _channel_dft_tables():
    g = FNET_GROUP_DIM
    k = np.arange(g, dtype=np.int64)
    ang = 2.0 * np.pi * ((k[:, None] * k[None, :]) % g).astype(np.float64) / g
    eye = np.eye(FNET_WIDTH // g)
    return np.concatenate([np.kron(eye, np.cos(ang)), np.kron(eye, np.sin(ang))], axis=1).astype(np.float32)


CONV_PAD = 16
CONV_ROWS = 128


def _conv_kernel(c_ref, dw_ref, b_ref, lg_ref, lb_ref, o_ref, hp_ref, *, seq):
    zeros = jnp.zeros((CONV_PAD, CONV_WIDTH), F32)
    hp_ref[0:CONV_PAD, :] = zeros
    hp_ref[seq + CONV_PAD:seq + 2 * CONV_PAD, :] = zeros

    def glu_body(r, carry):
        base = pl.multiple_of(r * CONV_ROWS, CONV_ROWS)
        a = c_ref[pl.ds(base, CONV_ROWS), 0:CONV_WIDTH]
        gt = c_ref[pl.ds(base, CONV_ROWS), CONV_WIDTH:2 * CONV_WIDTH]
        hp_ref[pl.ds(base + CONV_PAD, CONV_ROWS), :] = a * jax.nn.sigmoid(gt)
        return carry

    lax.fori_loop(0, seq // CONV_ROWS, glu_body, 0)

    blk_rows = CONV_ROWS + 2 * CONV_PAD

    def conv_body(r, carry):
        base = pl.multiple_of(r * CONV_ROWS, CONV_ROWS)
        outs = []
        for cch in range(CONV_WIDTH // LANES):
            ls = slice(cch * LANES, (cch + 1) * LANES)
            blk = hp_ref[pl.ds(base, blk_rows), ls]
            acc = jnp.zeros((CONV_ROWS, LANES), F32)
            for s in range(8):
                blk_s = blk if s == 0 else pltpu.roll(blk, blk_rows - s, axis=0)
                for m in range(4):
                    off = 8 * m + s
                    k = off - 1
                    if 0 <= k < CONV_K:
                        acc = acc + blk_s[8 * m:8 * m + CONV_ROWS, :] * dw_ref[k:k + 1, ls]
            outs.append(acc)
        y = jnp.concatenate(outs, axis=1) + b_ref[...]
        mu = jnp.mean(y, axis=-1, keepdims=True)
        yc = y - mu
        var = jnp.mean(yc * yc, axis=-1, keepdims=True)
        hn = yc * lax.rsqrt(var + EPS) * lg_ref[...] + lb_ref[...]
        o_ref[pl.ds(base, CONV_ROWS), :] = (hn * jax.nn.sigmoid(hn)).astype(o_ref.dtype)
        return carry

    lax.fori_loop(0, seq // CONV_ROWS, conv_body, 0)


def _conv(c_in, dw, b, lg, lb, batch, seq):
    vec = pl.BlockSpec((1, CONV_WIDTH), lambda bi: (0, 0))
    return pl.pallas_call(
        functools.partial(_conv_kernel, seq=seq),
        grid=(batch,),
        in_specs=[
            pl.BlockSpec((seq, 2 * CONV_WIDTH), lambda bi: (bi, 0)),
            pl.BlockSpec((CONV_K, CONV_WIDTH), lambda bi: (0, 0)),
            vec, vec, vec,
        ],
        out_specs=pl.BlockSpec((seq, CONV_WIDTH), lambda bi: (bi, 0)),
        out_shape=jax.ShapeDtypeStruct((batch * seq, CONV_WIDTH), BF16),
        scratch_shapes=[pltpu.VMEM((seq + 2 * CONV_PAD, CONV_WIDTH), F32)],
        compiler_params=_cparams("arbitrary"),
        name="conv",
    )(c_in, dw, b, lg, lb)


def _out_kernel(f_ref, c_ref, a_ref, w_ref, gl_ref, x_ref, mod_ref, bg_ref, g2_ref,
                wf_ref, wc_ref, wa_ref, ww_ref, wo_ref, x1_ref, h2t_ref):
    merged = None
    for i, (br, wt) in enumerate(((f_ref, wf_ref), (c_ref, wc_ref), (a_ref, wa_ref), (w_ref, ww_ref))):
        y = _dot(br[...], wt[...])
        gate = jax.nn.sigmoid(gl_ref[:, i * D_MODEL:(i + 1) * D_MODEL] + bg_ref[i:i + 1, :])
        merged = gate * y if merged is None else merged + gate * y
    mix = _dot(merged.astype(BF16), wo_ref[...])
    x1 = x_ref[...] + mod_ref[0, 2:3, :] * mix
    x1_ref[...] = x1
    h2 = _normmod(x1, g2_ref[...], mod_ref[0, 4:5, :], mod_ref[0, 3:4, :])
    h2t_ref[...] = h2.T.astype(BF16)


def _out_proj(f, c, a, w, gl, x, mod, bg, g2, wf, wc, wa, ww, wo, seq, tt=256):
    t = x.shape[0]
    nb = mod.shape[0]
    tpb = seq // tt
    mod_idx = (lambda i: (i // tpb, 0, 0)) if nb > 1 else (lambda i: (0, 0, 0))
    br = pl.BlockSpec((tt, 512), lambda i: (i, 0))
    wbr = pl.BlockSpec((512, D_MODEL), lambda i: (0, 0))
    return pl.pallas_call(
        _out_kernel,
        grid=(t // tt,),
        in_specs=[
            br, br, br, br,
            pl.BlockSpec((tt, N_BRANCH * D_MODEL), lambda i: (i, 0)),
            pl.BlockSpec((tt, D_MODEL), lambda i: (i, 0)),
            pl.BlockSpec((1, 6, D_MODEL), mod_idx),
            pl.BlockSpec((N_BRANCH, D_MODEL), lambda i: (0, 0)),
            pl.BlockSpec((1, D_MODEL), lambda i: (0, 0)),
            wbr, wbr, wbr, wbr,
            pl.BlockSpec((D_MODEL, D_MODEL), lambda i: (0, 0)),
        ],
        out_specs=[
            pl.BlockSpec((tt, D_MODEL), lambda i: (i, 0)),
            pl.BlockSpec((D_MODEL, tt), lambda i: (0, i)),
        ],
        out_shape=[jax.ShapeDtypeStruct((t, D_MODEL), F32), jax.ShapeDtypeStruct((D_MODEL, t), BF16)],
        compiler_params=_cparams("arbitrary"),
        name="out_proj",
    )(f, c, a, w, gl, x, mod, bg, g2, wf, wc, wa, ww, wo)


NOT_TOP = 99.0


def _bf16_twice(x):
    bits = pltpu.bitcast(x.astype(BF16).astype(F32), jnp.uint32)
    return bits | (bits >> 16)


def _route_kernel(h2t_ref, wq_ref, sk_ref, a0_ref, n0_ref, a1_ref, r1_ref,
                  qt_ref, sc_ref, srt_ref, rnk_ref, flag_ref, *, tt):
    ncol = tt // LANES
    qt_ref[...] = _dot(wq_ref[...], h2t_ref[...]).astype(BF16)
    for hp in range(2 * PEER_HEADS):
        sc_ref[hp] = _dot(sk_ref[hp % 2], qt_ref[hp * PEER_HALF:(hp + 1) * PEER_HALF, :])

    key_iota = lax.broadcasted_iota(jnp.int32, (N_KEYS, LANES), 0).astype(F32)

    sub8 = lax.broadcasted_iota(jnp.int32, (8, LANES), 0).astype(F32)

    def _tree(op, xs):
        xs = list(xs)
        while len(xs) > 1:
            xs = [op(xs[i], xs[i + 1]) if i + 1 < len(xs) else xs[i] for i in range(0, len(xs), 2)]
        return xs[0]

    def note_ties(count):
        flag_ref[...] = jnp.maximum(flag_ref[...], jnp.where(count != float(PEER_TOPK), 1.0, 0.0))

    def top16(s, exact):
        rank = jnp.full((N_KEYS, LANES), NOT_TOP, F32)
        tops = []
        for it in range(PEER_TOPK):
            m = jnp.max(s, axis=0, keepdims=True)
            if exact:
                first = jnp.min(jnp.where(s == m, key_iota, float(N_KEYS)), axis=0, keepdims=True)
                hit = key_iota == first
            else:
                hit = s == m
            s = jnp.where(hit, -jnp.inf, s)
            rank = jnp.where(hit, float(it), rank)
            tops.append(m)
        if not exact:
            note_ties(jnp.sum(jnp.where(rank < float(PEER_TOPK), 1.0, 0.0), axis=0, keepdims=True))
        return jnp.concatenate(tops, axis=0), rank

    def combine(h, ls, exact):
        s0 = srt_ref[2 * h, :, ls]
        s1 = srt_ref[2 * h + 1, :, ls]
        cands = [s0[0:1, :] + s1[0:8, :], s0[0:1, :] + s1[8:16, :]]
        poss = [sub8, sub8 + 8.0]
        for a in range(1, 8):
            c = s0[a:a + 1, :] + s1[0:8, :]
            cands.append(jnp.where(sub8 < float(16 // (a + 1)), c, -jnp.inf))
            poss.append(sub8 + float(16 * a))
        cands.append(s0[8:16, :] + s1[0:1, :])
        poss.append((sub8 + 8.0) * 16.0)
        sels = [jnp.zeros_like(c) for c in cands]
        tops = []
        for it in range(PEER_TOPK):
            m = jnp.max(_tree(jnp.maximum, cands), axis=0, keepdims=True)
            if exact:
                firsts = [jnp.where(c == m, p, 999.0) for c, p in zip(cands, poss)]
                first = jnp.min(_tree(jnp.minimum, firsts), axis=0, keepdims=True)
            for j in range(len(cands)):
                hit = (poss[j] == first) if exact else (cands[j] == m)
                cands[j] = jnp.where(hit, -jnp.inf, cands[j])
                sels[j] = jnp.where(hit, 1.0, sels[j])
            tops.append(m)
        z = _tree(jnp.add, [jnp.exp(m - tops[0]) for m in tops])
        inv_z = 1.0 / z
        n_rows = [jnp.sum(sels[0] + sels[1], axis=0, keepdims=True)]
        n_rows += [jnp.sum(sels[a + 1], axis=0, keepdims=True) for a in range(1, 8)]
        n_hi = sels[9]
        if not exact:
            note_ties(_tree(jnp.add, n_rows) + jnp.sum(n_hi, axis=0, keepdims=True))
        rank0 = rnk_ref[2 * h, :, ls]
        rank1 = rnk_ref[2 * h + 1, :, ls]
        n0 = jnp.zeros((N_KEYS, LANES), F32)
        for a in range(8):
            n0 = jnp.where(rank0 == float(a), n_rows[a], n0)
            n0 = jnp.where(rank0 == float(a + 8), n_hi[a:a + 1, :], n0)
        sc0 = sc_ref[2 * h, :, ls]
        sc1 = sc_ref[2 * h + 1, :, ls]
        a0 = jnp.where(rank0 < float(PEER_TOPK), jnp.exp(sc0 - s0[0:1, :]) * inv_z, 0.0)
        a1 = jnp.where(rank1 < float(PEER_TOPK), jnp.exp(sc1 - s1[0:1, :]), 0.0)
        a0_ref[h, :, ls] = _bf16_twice(a0)
        n0_ref[h, :, ls] = _bf16_twice(n0)
        a1_ref[h, :, ls] = pltpu.bitcast(a1.astype(BF16), jnp.uint32)
        r1_ref[h, :, ls] = pltpu.bitcast(rank1.astype(BF16), jnp.uint32)

    def run_stages(exact):
        def stage1(hp, carry):
            for col in range(ncol):
                ls = slice(col * LANES, (col + 1) * LANES)
                tops, rank = top16(sc_ref[hp, :, ls], exact)
                srt_ref[hp, :, ls] = tops
                rnk_ref[hp, :, ls] = rank
            return carry

        def stage2(h, carry):
            for col in range(ncol):
                combine(h, slice(col * LANES, (col + 1) * LANES), exact)
            return carry

        lax.fori_loop(0, 2 * PEER_HEADS, stage1, 0)
        lax.fori_loop(0, PEER_HEADS, stage2, 0)

    flag_ref[...] = jnp.zeros_like(flag_ref)
    run_stages(exact=False)

    @pl.when(jnp.max(flag_ref[...]) > 0.0)
    def _():
        run_stages(exact=True)


def _route(h2t, wq_t, sk, tt=256):
    t = h2t.shape[1]
    out0 = pl.BlockSpec((PEER_HEADS, N_KEYS, tt), lambda i: (0, 0, i))
    out1 = pl.BlockSpec((PEER_HEADS, N_KEYS // 2, tt), lambda i: (0, 0, i))
    shp0 = jax.ShapeDtypeStruct((PEER_HEADS, N_KEYS, t), jnp.uint32)
    shp1 = jax.ShapeDtypeStruct((PEER_HEADS, N_KEYS // 2, t), jnp.uint32)
    return pl.pallas_call(
        functools.partial(_route_kernel, tt=tt),
        grid=(t // tt,),
        in_specs=[
            pl.BlockSpec((D_MODEL, tt), lambda i: (0, i)),
            pl.BlockSpec((2 * PEER_HEADS * PEER_HALF, D_MODEL), lambda i: (0, 0)),
            pl.BlockSpec((2, N_KEYS, PEER_HALF), lambda i: (0, 0, 0)),
        ],
        out_specs=[out0, out0, out1, out1],
        out_shape=[shp0, shp0, shp1, shp1],
        scratch_shapes=[
            pltpu.VMEM((2 * PEER_HEADS * PEER_HALF, tt), BF16),
            pltpu.VMEM((2 * PEER_HEADS, N_KEYS, tt), F32),
            pltpu.VMEM((2 * PEER_HEADS, PEER_TOPK, tt), F32),
            pltpu.VMEM((2 * PEER_HEADS, N_KEYS, tt), F32),
            pltpu.VMEM((8, LANES), F32),
        ],
        compiler_params=_cparams("arbitrary"),
        name="peer_route",
    )(h2t, wq_t, sk)


def _gelu_tanh(x):
    c = 2.0 * 0.7978845608028654
    neg_u = x * ((-c * 0.044715) * (x * x) - c)
    return x / (1.0 + jnp.exp(neg_u))


def _peer_step(h2t_ref, a0_ref, n0_ref, a1_ref, r1_ref, u_ref, vt_ref, acc_ref,
               s_new, s_cur, w_cur, w_done, i0, *, tt, n_i):
    ncol = tt // LANES
    zero = jnp.zeros((PACK_ROWS, LANES), BF16)
    rows_per_piece = MXU_PIECE_ROWS // N_KEYS
    for il in range(n_i):
        mr = slice(il // rows_per_piece * MXU_PIECE_ROWS, (il // rows_per_piece + 1) * MXU_PIECE_ROWS)
        if il % rows_per_piece == 0:
            s_new[mr, :] = _dot(u_ref[mr, :], h2t_ref[...])
        for col in range(ncol):
            if il % rows_per_piece == rows_per_piece // 2 and col == 0:
                acc_ref[mr, :] += _dot(vt_ref[mr, :], w_done[...])
            ls = slice(col * LANES, (col + 1) * LANES)
            n0 = [pltpu.bitcast(jnp.broadcast_to(n0_ref[h, pl.ds(i0, n_i), ls][il:il + 1, :], (8, LANES)), BF16)
                  for h in range(PEER_HEADS)]
            a0 = [pltpu.bitcast(jnp.broadcast_to(a0_ref[h, pl.ds(i0, n_i), ls][il:il + 1, :], (8, LANES)), BF16)
                  for h in range(PEER_HEADS)]
            for k in range(N_KEYS // PACK_ROWS):
                pr = slice(k * PACK_ROWS // 2, (k + 1) * PACK_ROWS // 2)
                gate = None
                for h in range(PEER_HEADS):
                    r1 = pltpu.bitcast(r1_ref[h, pr, ls], BF16)
                    a1 = pltpu.bitcast(a1_ref[h, pr, ls], BF16)
                    term = jnp.where(r1 < n0[h], a0[h] * a1, zero)
                    gate = term if gate is None else gate + term
                rows = slice(il * N_KEYS + k * PACK_ROWS, il * N_KEYS + (k + 1) * PACK_ROWS)
                act = _gelu_tanh(s_cur[rows, ls])
                w_cur[rows, ls] = gate * act.astype(BF16)


def _peer_kernel(h2t_ref, a0_ref, n0_ref, a1_ref, r1_ref, u_ref, vt_ref, x1_ref, mod_ref,
                 o_ref, s0_ref, s1_ref, w0_ref, w1_ref, acc_ref, *, tt, ec, ne, total):
    g = pl.program_id(0)
    n_i = ec // N_KEYS
    assert n_i == 8, "one expert chunk must cover exactly one sublane tile of half-0 keys"
    g_gate = jnp.clip(g - 1, 0, total - 1)
    c_out = jnp.clip(g - 2, 0, total - 1) % ne
    i0 = pl.multiple_of((g_gate % ne) * n_i, n_i)

    @pl.when(g == 0)
    def _():
        s1_ref[...] = jnp.zeros_like(s1_ref)
        w0_ref[...] = jnp.zeros_like(w0_ref)
        w1_ref[...] = jnp.zeros_like(w1_ref)

    @pl.when(c_out == 0)
    def _():
        acc_ref[...] = jnp.zeros_like(acc_ref)

    step = functools.partial(_peer_step, h2t_ref, a0_ref, n0_ref, a1_ref, r1_ref, u_ref, vt_ref, acc_ref,
                             tt=tt, n_i=n_i)

    @pl.when(g % 2 == 0)
    def _():
        step(s0_ref, s1_ref, w1_ref, w0_ref, i0)

    @pl.when(g % 2 == 1)
    def _():
        step(s1_ref, s0_ref, w0_ref, w1_ref, i0)

    @pl.when((g >= 2) & (c_out == ne - 1))
    def _():
        o_ref[...] = x1_ref[...] + mod_ref[0, 5:6, :] * acc_ref[...].T


def _peer(h2t, a0, n0, a1, r1, u, vt, x1, mod, seq, tt=512, ec=1024):
    t = x1.shape[0]
    nb = mod.shape[0]
    tpb = seq // tt
    ne = N_EXPERTS // ec
    total = (t // tt) * ne
    assert ec == D_MODEL, "the kernel walks U-chunk rows and out^T rows with one 128-row block index"

    def st_a(g):
        return jnp.minimum(g, total - 1)

    def st_b(g):
        return jnp.clip(g - 1, 0, total - 1)

    def st_c(g):
        return jnp.clip(g - 2, 0, total - 1)

    mod_idx = (lambda g: (st_c(g) // ne // tpb, 0, 0)) if nb > 1 else (lambda g: (0, 0, 0))
    rt0 = pl.BlockSpec((PEER_HEADS, N_KEYS, tt), lambda g: (0, 0, st_b(g) // ne))
    rt1 = pl.BlockSpec((PEER_HEADS, N_KEYS // 2, tt), lambda g: (0, 0, st_b(g) // ne))
    return pl.pallas_call(
        functools.partial(_peer_kernel, tt=tt, ec=ec, ne=ne, total=total),
        grid=(total + 2,),
        in_specs=[
            pl.BlockSpec((D_MODEL, tt), lambda g: (0, st_a(g) // ne)),
            rt0, rt0, rt1, rt1,
            pl.BlockSpec((ec, D_MODEL), lambda g: (st_a(g) % ne, 0)),
            pl.BlockSpec((D_MODEL, ec), lambda g: (0, st_c(g) % ne)),
            pl.BlockSpec((tt, D_MODEL), lambda g: (st_c(g) // ne, 0)),
            pl.BlockSpec((1, 6, D_MODEL), mod_idx),
        ],
        out_specs=pl.BlockSpec((tt, D_MODEL), lambda g: (st_c(g) // ne, 0)),
        out_shape=jax.ShapeDtypeStruct((t, D_MODEL), F32),
        scratch_shapes=[
            pltpu.VMEM((ec, tt), F32), pltpu.VMEM((ec, tt), F32),
            pltpu.VMEM((ec, tt), BF16), pltpu.VMEM((ec, tt), BF16),
            pltpu.VMEM((D_MODEL, tt), F32),
        ],
        compiler_params=_cparams("arbitrary"),
        name="peer_mix",
    )(h2t, a0, n0, a1, r1, u, vt, x1, mod)


def _final_norm_kernel(x_ref, g_ref, o_ref):
    x = x_ref[...]
    ms = jnp.mean(x * x, axis=-1, keepdims=True)
    o_ref[...] = x * lax.rsqrt(ms + EPS) * g_ref[...]


def _final_norm(x, g, tt=512):
    t = x.shape[0]
    return pl.pallas_call(
        _final_norm_kernel,
        grid=(t // tt,),
        in_specs=[pl.BlockSpec((tt, D_MODEL), lambda i: (i, 0)), pl.BlockSpec((1, D_MODEL), lambda i: (0, 0))],
        out_specs=pl.BlockSpec((tt, D_MODEL), lambda i: (i, 0)),
        out_shape=jax.ShapeDtypeStruct((t, D_MODEL), F32),
        compiler_params=_cparams("arbitrary"),
        name="final_norm",
    )(x, g)


def _rope_tables(seq):
    n_rows = seq // GRID_W
    row = jnp.repeat(jnp.arange(n_rows), GRID_W).astype(F32)
    col = jnp.tile(jnp.arange(GRID_W), n_rows).astype(F32)
    n_f = HEAD_DIM // 4
    inv = 1.0 / (ROPE_THETA ** (jnp.arange(n_f, dtype=F32) / n_f))
    ang = jnp.concatenate([row[:, None] * inv, col[:, None] * inv], axis=-1)
    cos, sin = jnp.cos(ang), jnp.sin(ang)
    cos_t = jnp.tile(jnp.concatenate([cos, cos], axis=-1), (1, N_HEADS))
    sin_t = jnp.tile(jnp.concatenate([-sin, sin], axis=-1), (1, N_HEADS))
    return cos_t, sin_t


def _pad_heads(kv):
    z = jnp.zeros_like(kv[:, :HEAD_DIM])
    h0, h1 = kv[:, :HEAD_DIM], kv[:, HEAD_DIM:]
    return jnp.concatenate([h0, z, z, h0, h1, z, z, h1], axis=-1).astype(BF16)


def _layer(x, mod, lp, consts, batch, seq, ctx_kv):
    is_ctx = ctx_kv is None
    g1 = lp["norm1_g"]
    w_in = lp["w_in"]
    f_in = _in_plain(x, mod, g1, w_in, seq, COL_F, 512, 512, BF16)
    c_in = _in_plain(x, mod, g1, w_in, seq, COL_C, 1024, 512, F32)
    gl = _in_plain(x, mod, g1, w_in, seq, COL_GL, 4096, 1024, F32)
    qa, qw, kxa, vxa, kxw, vxw, kvf = _in_qkv(
        x, mod, g1, w_in, seq, consts["bd"], lp["q_norm_g"], lp["k_norm_g"],
        None if is_ctx else consts["rope"])
    dft = consts["dft_ctx"] if is_ctx else consts["dft_lat"]
    f = _fourier(f_in, consts["ccs"], dft, batch, seq, min(seq, 512))
    cv = _conv(c_in, lp["conv_dw"], lp["conv_b"], lp["conv_ln_g"], lp["conv_ln_b"], batch, seq)
    no_sink = jnp.full((N_HEADS,), NEG_INF, F32)
    if is_ctx:
        a = _attention(qa, kxa, vxa, no_sink, batch, seq)
        w = _attention(qw, kxw, vxw, lp["win_sink"], batch, seq)
    else:
        ka_c, va_c, kw_c, vw_c = ctx_kv
        a = _attention(qa, kxa, vxa, no_sink, batch, seq, ctx=(ka_c, va_c))
        w = _attention(qw, kxw, vxw, lp["win_sink"], batch, seq, ctx=(kw_c, vw_c), windowed=True)
    x1, h2t = _out_proj(f, cv, a, w, gl, x, mod, lp["b_gate"], lp["norm2_g"],
                        lp["w_fourier_out"], lp["w_conv_out"], lp["w_attn_out"], lp["w_win_out"],
                        lp["w_out"], seq)
    a0, n0, a1, r1 = _route(h2t, lp["w_peer_q_t"], lp["peer_subkeys"])
    x2 = _peer(h2t, a0, n0, a1, r1, lp["peer_u"], lp["peer_v_t"], x1, mod, seq)
    return x2, kvf


def kernel(x_prompt, x_sample, cache_attn_k, cache_attn_v, cache_win_k, cache_win_v, c, c_ctx, norm1_g, norm2_g, w_ada, b_ada, w_in, b_gate, w_fourier_out, conv_dw, conv_b, conv_ln_g, conv_ln_b, w_conv_out, q_norm_g, k_norm_g, w_attn_out, win_sink, w_win_out, w_out, w_peer_q, peer_subkeys, peer_u, peer_v, final_norm_g):
    b_ctx, s_ctx, _ = x_prompt.shape
    b_lat, s_lat, _ = x_sample.shape
    past = cache_attn_k.shape[2]

    cond = jnp.concatenate([c_ctx[None, :], c, jnp.zeros((16 - 1 - b_lat, D_MODEL), F32)], axis=0)
    ada = _ada_all(cond, w_ada, b_ada).reshape(DEPTH, 16, 6, D_MODEL)

    consts = {
        "bd": jnp.asarray(np.kron(np.eye(N_HEADS), np.ones((HEAD_DIM, HEAD_DIM))), BF16),
        "ccs": jnp.asarray(_channel_dft_tables()).astype(BF16),
        "dft_ctx": jnp.asarray(_dft_tables(s_ctx)).astype(BF16),
        "dft_lat": jnp.asarray(_dft_tables(s_lat)).astype(BF16),
        "rope": _rope_tables(s_lat),
    }

    xp = x_prompt.reshape(b_ctx * s_ctx, D_MODEL)
    xs = x_sample.reshape(b_lat * s_lat, D_MODEL)
    kv_out = []
    for l in range(DEPTH):
        lp = {
            "norm1_g": norm1_g[l][None, :], "norm2_g": norm2_g[l][None, :],
            "w_in": w_in[l].astype(BF16), "b_gate": b_gate[l],
            "w_fourier_out": w_fourier_out[l].astype(BF16),
            "conv_dw": conv_dw[l], "conv_b": conv_b[l][None, :],
            "conv_ln_g": conv_ln_g[l][None, :], "conv_ln_b": conv_ln_b[l][None, :],
            "w_conv_out": w_conv_out[l].astype(BF16),
            "q_norm_g": jnp.tile(q_norm_g[l], N_HEADS)[None, :],
            "k_norm_g": jnp.tile(k_norm_g[l], 2)[None, :],
            "w_attn_out": w_attn_out[l].astype(BF16), "win_sink": win_sink[l],
            "w_win_out": w_win_out[l].astype(BF16), "w_out": w_out[l].astype(BF16),
            "w_peer_q_t": w_peer_q[l].T.astype(BF16),
            "peer_subkeys": peer_subkeys[l].astype(BF16),
            "peer_u": peer_u[l].astype(BF16),
            "peer_v_t": peer_v[l].T.astype(BF16),
        }
        mod_ctx = ada[l, 0:1]
        mod_lat = ada[l, 1:1 + b_lat]
        xp, kvf = _layer(xp, mod_ctx, lp, consts, b_ctx, s_ctx, None)
        kv_out.append(kvf)
        cached = tuple(
            _pad_heads(cc[:, l].reshape(b_lat * past, ATTN_KV))
            for cc in (cache_attn_k, cache_attn_v, cache_win_k, cache_win_v))
        xs, _ = _layer(xs, mod_lat, lp, consts, b_lat, s_lat, cached)

    y_prompt = _final_norm(xp, final_norm_g[None, :]).reshape(b_ctx, s_ctx, D_MODEL)
    y_sample = _final_norm(xs, final_norm_g[None, :]).reshape(b_lat, s_lat, D_MODEL)
    outs = []
    for j in range(4):
        per_layer = [kvf[:, j * 128:(j + 1) * 128].reshape(b_ctx, s_ctx, 2, HEAD_DIM) for kvf in kv_out]
        outs.append(jnp.stack(per_layer, axis=1))
    return (y_prompt, y_sample, outs[0], outs[1], outs[2], outs[3])
```

```python
import functools

import numpy as np
import jax
import jax.numpy as jnp
from jax import lax
from jax.experimental import pallas as pl
from jax.experimental.pallas import tpu as pltpu

F32 = jnp.float32
BF16 = jnp.bfloat16

D_MODEL = 1024
DEPTH = 4
GRID_W = 64
HEAD_DIM = 64
FNET_GROUP_DIM = 64
FNET_WIDTH = 512
CONV_WIDTH = 512
CONV_K = 31
N_HEADS = 8
ATTN_Q = 512
ATTN_KV = 128
WINDOW = 128
ROPE_THETA = 10000.0
N_BRANCH = 4
COL_F, COL_C, COL_QKV, COL_GL = 0, 512, 1536, 3072
IN_COLS = 7168
N_KEYS = 128
N_EXPERTS = N_KEYS * N_KEYS
PEER_HEADS = 8
PEER_HALF = 128
PEER_TOPK = 16
EPS = 1e-6
NEG_INF = -1e30
LANES = 128
PACK_ROWS = 16
MXU_PIECE_ROWS = 512

VMEM_LIMIT = 56 * 1024 * 1024


def _cparams(*sem):
    return pltpu.CompilerParams(dimension_semantics=sem, vmem_limit_bytes=VMEM_LIMIT)


def _dot(a, b):
    return jnp.dot(a, b, preferred_element_type=F32)


def _dot_nt(a, b):
    return lax.dot_general(a, b, (((1,), (1,)), ((), ())), preferred_element_type=F32)


def _normmod(x, g, sc, sh):
    ms = jnp.mean(x * x, axis=-1, keepdims=True)
    y = x * lax.rsqrt(ms + EPS)
    return (y * g) * (1.0 + sc) + sh


def _ada_kernel(c_ref, w_ref, b_ref, o_ref):
    c = c_ref[...]
    a = (c * jax.nn.sigmoid(c)).astype(BF16)
    o_ref[0] = _dot(a, w_ref[0].astype(BF16)) + b_ref[0]


def _ada_all(cond16, w_ada, b_ada):
    tn = 1536
    n = w_ada.shape[2]
    return pl.pallas_call(
        _ada_kernel,
        grid=(DEPTH, n // tn),
        in_specs=[
            pl.BlockSpec((16, D_MODEL), lambda l, j: (0, 0)),
            pl.BlockSpec((1, D_MODEL, tn), lambda l, j: (l, 0, j)),
            pl.BlockSpec((1, 1, tn), lambda l, j: (l, 0, j)),
        ],
        out_specs=pl.BlockSpec((1, 16, tn), lambda l, j: (l, 0, j)),
        out_shape=jax.ShapeDtypeStruct((DEPTH, 16, n), F32),
        compiler_params=_cparams("arbitrary", "arbitrary"),
        name="ada",
    )(cond16, w_ada, b_ada.reshape(DEPTH, 1, n))


def _seg_mean_sq(x, bd):
    sq = x * x
    hi = sq.astype(BF16)
    lo = (sq - hi.astype(F32)).astype(BF16)
    return (_dot(hi, bd) + _dot(lo, bd)) * (1.0 / HEAD_DIM)


def _rope(x, cos, sin_signed):
    w = x.shape[1]
    fwd = pltpu.roll(x, HEAD_DIM // 2, axis=1)
    bwd = pltpu.roll(x, w - HEAD_DIM // 2, axis=1)
    lane = lax.broadcasted_iota(jnp.int32, x.shape, 1)
    first = (lane % HEAD_DIM) < (HEAD_DIM // 2)
    partner = jnp.where(first, bwd, fwd)
    return x * cos + partner * sin_signed


def _store_head_padded(o_ref, k):
    kr = pltpu.roll(k, HEAD_DIM, axis=1)
    lane = lax.broadcasted_iota(jnp.int32, k.shape, 1)
    lo = lane < HEAD_DIM
    zero = jnp.zeros_like(k)
    o_ref[:, 0:128] = jnp.where(lo, k, zero).astype(BF16)
    o_ref[:, 128:256] = jnp.where(lo, zero, kr).astype(BF16)
    o_ref[:, 256:384] = jnp.where(lo, kr, zero).astype(BF16)
    o_ref[:, 384:512] = jnp.where(lo, zero, k).astype(BF16)


def _in_qkv_kernel(*refs, use_rope):
    if use_rope:
        (x_ref, mod_ref, g_ref, w_ref, bd_ref, qg_ref, kg_ref, cos_ref, sin_ref,
         qa_ref, qw_ref, kxa_ref, vxa_ref, kxw_ref, vxw_ref, kvf_ref) = refs
    else:
        (x_ref, mod_ref, g_ref, w_ref, bd_ref, qg_ref, kg_ref,
         qa_ref, qw_ref, kxa_ref, vxa_ref, kxw_ref, vxw_ref, kvf_ref) = refs
    h = _normmod(x_ref[...], g_ref[...], mod_ref[0, 1:2, :], mod_ref[0, 0:1, :]).astype(BF16)
    acc = _dot(h, w_ref[...])
    aq = acc[:, 0:512]
    ak = acc[:, 512:640]
    av = acc[:, 640:768]
    wq = acc[:, 768:1280]
    wk = acc[:, 1280:1408]
    wv = acc[:, 1408:1536]
    bd = bd_ref[...]
    aq = aq * lax.rsqrt(_seg_mean_sq(aq, bd) + EPS) * qg_ref[...]
    ak = ak * lax.rsqrt(_seg_mean_sq(ak, bd[0:128, 0:128]) + EPS) * kg_ref[...]
    if use_rope:
        cos = cos_ref[...]
        sin = sin_ref[...]
        aq = _rope(aq, cos, sin)
        wq = _rope(wq, cos, sin)
        ak = _rope(ak, cos[:, 0:128], sin[:, 0:128])
        wk = _rope(wk, cos[:, 0:128], sin[:, 0:128])
    scale = HEAD_DIM ** -0.5
    qa_ref[...] = (aq * scale).astype(BF16)
    qw_ref[...] = (wq * scale).astype(BF16)
    _store_head_padded(kxa_ref, ak)
    _store_head_padded(vxa_ref, av)
    _store_head_padded(kxw_ref, wk)
    _store_head_padded(vxw_ref, wv)
    kvf_ref[:, 0:128] = ak
    kvf_ref[:, 128:256] = av
    kvf_ref[:, 256:384] = wk
    kvf_ref[:, 384:512] = wv


def _in_qkv(x, mod, g, w_in, seq, bd, qg8, kg2, rope_tabs, tt=256):
    t = x.shape[0]
    nb = mod.shape[0]
    tpb = seq // tt
    mod_idx = (lambda i: (i // tpb, 0, 0)) if nb > 1 else (lambda i: (0, 0, 0))
    use_rope = rope_tabs is not None
    ncol = 1536
    in_specs = [
        pl.BlockSpec((tt, D_MODEL), lambda i: (i, 0)),
        pl.BlockSpec((1, 6, D_MODEL), mod_idx),
        pl.BlockSpec((1, D_MODEL), lambda i: (0, 0)),
        pl.BlockSpec((D_MODEL, ncol), lambda i: (0, COL_QKV // ncol)),
        pl.BlockSpec((512, 512), lambda i: (0, 0)),
        pl.BlockSpec((1, 512), lambda i: (0, 0)),
        pl.BlockSpec((1, 128), lambda i: (0, 0)),
    ]
    args = [x, mod, g, w_in, bd, qg8, kg2]
    if use_rope:
        in_specs += [pl.BlockSpec((tt, 512), lambda i: (i % tpb, 0))] * 2
        args += list(rope_tabs)
    tok = pl.BlockSpec((tt, 512), lambda i: (i, 0))
    return pl.pallas_call(
        functools.partial(_in_qkv_kernel, use_rope=use_rope),
        grid=(t // tt,),
        in_specs=in_specs,
        out_specs=[tok] * 7,
        out_shape=[jax.ShapeDtypeStruct((t, 512), BF16)] * 6 + [jax.ShapeDtypeStruct((t, 512), F32)],
        compiler_params=_cparams("arbitrary"),
        name="in_qkv",
    )(*args)


def _attn_kernel(*refs, tq, seq, n_ctx, windowed):
    if n_ctx:
        sink_ref, q_ref, kx_ref, vx_ref, kc_ref, vc_ref, o_ref = refs
    else:
        sink_ref, q_ref, kx_ref, vx_ref, o_ref = refs
    i = pl.program_id(1)
    if windowed:
        span = tq + 2 * WINDOW
        start = jnp.clip(i * tq - WINDOW, 0, seq - span)
        start = pl.multiple_of(start, WINDOW)
        qpos = i * tq + lax.broadcasted_iota(jnp.int32, (tq, span), 0)
        kpos = start + lax.broadcasted_iota(jnp.int32, (tq, span), 1)
        valid = jnp.abs(kpos - qpos) <= WINDOW
    for jb in range(4):
        q2 = q_ref[:, jb * 128:(jb + 1) * 128]
        acc = None
        for e in range(2):
            head = jb * 2 + e
            v_idx = (jb // 2) * 2 + e
            cols = slice(v_idx * 128, (v_idx + 1) * 128)
            if windowed:
                k_loc = kx_ref[pl.ds(start, span), cols]
                v_loc = vx_ref[pl.ds(start, span), cols]
            else:
                k_loc = kx_ref[:, cols]
                v_loc = vx_ref[:, cols]
            s = _dot_nt(q2, k_loc)
            if windowed:
                s = jnp.where(valid, s, NEG_INF)
            sink = sink_ref[head]
            m = jnp.maximum(jnp.max(s, axis=-1, keepdims=True), sink)
            if n_ctx:
                s_c = _dot_nt(q2, kc_ref[:, cols])
                m = jnp.maximum(m, jnp.max(s_c, axis=-1, keepdims=True))
            p = jnp.exp(s - m)
            l = jnp.sum(p, axis=-1, keepdims=True) + jnp.exp(sink - m)
            o = _dot(p.astype(BF16), v_loc)
            if n_ctx:
                p_c = jnp.exp(s_c - m)
                l = l + jnp.sum(p_c, axis=-1, keepdims=True)
                o = o + _dot(p_c.astype(BF16), vc_ref[:, cols])
            o = o / l
            acc = o if acc is None else acc + o
        o_ref[:, jb * 128:(jb + 1) * 128] = acc.astype(o_ref.dtype)


def _attention(q, kx, vx, sink, batch, seq, ctx=None, windowed=False, tq=256):
    n_ctx = 0 if ctx is None else ctx[0].shape[0] // batch
    nq = seq // tq
    in_specs = [
        pl.BlockSpec(memory_space=pltpu.SMEM),
        pl.BlockSpec((tq, 512), lambda b, i: (b * nq + i, 0)),
        pl.BlockSpec((seq, 512), lambda b, i: (b, 0)),
        pl.BlockSpec((seq, 512), lambda b, i: (b, 0)),
    ]
    args = [sink, q, kx, vx]
    if n_ctx:
        in_specs += [pl.BlockSpec((n_ctx, 512), lambda b, i: (b, 0))] * 2
        args += list(ctx)
    return pl.pallas_call(
        functools.partial(_attn_kernel, tq=tq, seq=seq, n_ctx=n_ctx, windowed=windowed),
        grid=(batch, nq),
        in_specs=in_specs,
        out_specs=pl.BlockSpec((tq, 512), lambda b, i: (b * nq + i, 0)),
        out_shape=jax.ShapeDtypeStruct((batch * seq, 512), BF16),
        compiler_params=_cparams("arbitrary", "arbitrary"),
        name="attn_win" if windowed else "attn",
    )(*args)


def _fourier_kernel(x_ref, mod_ref, g_ref, w_ref, ccs_ref, dft_ref, o_ref, z_ref, *, seq, scale):
    @pl.when(pl.program_id(1) == 0)
    def _():
        def body(r, carry):
            base = pl.multiple_of(r * PROJ_ROWS, PROJ_ROWS)
            h = _normmod(x_ref[pl.ds(base, PROJ_ROWS), :], g_ref[...], mod_ref[0, 1:2, :], mod_ref[0, 0:1, :])
            f = _dot(h.astype(BF16), w_ref[...]).astype(BF16)
            z = _dot(f, ccs_ref[...])
            z_ref[pl.ds(base, PROJ_ROWS), :] = z[:, 0:FNET_WIDTH].astype(BF16)
            z_ref[pl.ds(seq + base, PROJ_ROWS), :] = z[:, FNET_WIDTH:].astype(BF16)
            return carry

        lax.fori_loop(0, seq // PROJ_ROWS, body, 0)

    o_ref[...] = (_dot(dft_ref[...], z_ref[...]) * scale).astype(o_ref.dtype)


def _fourier(x, mod, g, w_in, ccs, dft, batch, seq, tr):
    nr = seq // tr
    nb = mod.shape[0]
    mod_idx = (lambda b, i: (b, 0, 0)) if nb > 1 else (lambda b, i: (0, 0, 0))
    return pl.pallas_call(
        functools.partial(_fourier_kernel, seq=seq, scale=float((seq * FNET_GROUP_DIM) ** -0.5)),
        grid=(batch, nr),
        in_specs=[
            pl.BlockSpec((seq, D_MODEL), lambda b, i: (b, 0)),
            pl.BlockSpec((1, 6, D_MODEL), mod_idx),
            pl.BlockSpec((1, D_MODEL), lambda b, i: (0, 0)),
            pl.BlockSpec((D_MODEL, FNET_WIDTH), lambda b, i: (0, COL_F // FNET_WIDTH)),
            pl.BlockSpec((FNET_WIDTH, 2 * FNET_WIDTH), lambda b, i: (0, 0)),
            pl.BlockSpec((tr, 2 * seq), lambda b, i: (i, 0)),
        ],
        out_specs=pl.BlockSpec((tr, FNET_WIDTH), lambda b, i: (b * nr + i, 0)),
        out_shape=jax.ShapeDtypeStruct((batch * seq, FNET_WIDTH), BF16),
        scratch_shapes=[pltpu.VMEM((2 * seq, FNET_WIDTH), BF16)],
        compiler_params=_cparams("arbitrary", "arbitrary"),
        name="fourier",
    )(x, mod, g, w_in, ccs, dft)


def _dft_tables(seq):
    k = np.arange(seq, dtype=np.int64)
    ang = 2.0 * np.pi * ((k[:, None] * k[None, :]) % seq).astype(np.float64) / seq
    return np.concatenate([np.cos(ang), -np.sin(ang)], axis=1).astype(np.float32)


def _channel_dft_tables():
    g = FNET_GROUP_DIM
    k = np.arange(g, dtype=np.int64)
    ang = 2.0 * np.pi * ((k[:, None] * k[None, :]) % g).astype(np.float64) / g
    eye = np.eye(FNET_WIDTH // g)
    return np.concatenate([np.kron(eye, np.cos(ang)), np.kron(eye, np.sin(ang))], axis=1).astype(np.float32)


CONV_PAD = 16
CONV_ROWS = 128
PROJ_ROWS = 256


def _conv_kernel(x_ref, mod_ref, g_ref, wa_ref, wg_ref, dw_ref, b_ref, lg_ref, lb_ref, o_ref, hp_ref, *, seq):
    zeros = jnp.zeros((CONV_PAD, CONV_WIDTH), F32)
    hp_ref[0:CONV_PAD, :] = zeros
    hp_ref[seq + CONV_PAD:seq + 2 * CONV_PAD, :] = zeros

    def glu_body(r, carry):
        base = pl.multiple_of(r * PROJ_ROWS, PROJ_ROWS)
        h = _normmod(x_ref[pl.ds(base, PROJ_ROWS), :], g_ref[...], mod_ref[0, 1:2, :], mod_ref[0, 0:1, :])
        h = h.astype(BF16)
        hp_ref[pl.ds(base + CONV_PAD, PROJ_ROWS), :] = _dot(h, wa_ref[...]) * jax.nn.sigmoid(_dot(h, wg_ref[...]))
        return carry

    lax.fori_loop(0, seq // PROJ_ROWS, glu_body, 0)

    blk_rows = CONV_ROWS + 2 * CONV_PAD

    def conv_body(r, carry):
        base = pl.multiple_of(r * CONV_ROWS, CONV_ROWS)
        outs = []
        for cch in range(CONV_WIDTH // LANES):
            ls = slice(cch * LANES, (cch + 1) * LANES)
            blk = hp_ref[pl.ds(base, blk_rows), ls]
            acc = jnp.zeros((CONV_ROWS, LANES), F32)
            for s in range(8):
                blk_s = blk if s == 0 else pltpu.roll(blk, blk_rows - s, axis=0)
                for m in range(4):
                    off = 8 * m + s
                    k = off - 1
                    if 0 <= k < CONV_K:
                        acc = acc + blk_s[8 * m:8 * m + CONV_ROWS, :] * dw_ref[k:k + 1, ls]
            outs.append(acc)
        y = jnp.concatenate(outs, axis=1) + b_ref[...]
        mu = jnp.mean(y, axis=-1, keepdims=True)
        yc = y - mu
        var = jnp.mean(yc * yc, axis=-1, keepdims=True)
        hn = yc * lax.rsqrt(var + EPS) * lg_ref[...] + lb_ref[...]
        o_ref[pl.ds(base, CONV_ROWS), :] = (hn * jax.nn.sigmoid(hn)).astype(o_ref.dtype)
        return carry

    lax.fori_loop(0, seq // CONV_ROWS, conv_body, 0)


def _conv(x, mod, g, w_in, dw, b, lg, lb, batch, seq):
    vec = pl.BlockSpec((1, CONV_WIDTH), lambda bi: (0, 0))
    nb = mod.shape[0]
    mod_idx = (lambda bi: (bi, 0, 0)) if nb > 1 else (lambda bi: (0, 0, 0))
    return pl.pallas_call(
        functools.partial(_conv_kernel, seq=seq),
        grid=(batch,),
        in_specs=[
            pl.BlockSpec((seq, D_MODEL), lambda bi: (bi, 0)),
            pl.BlockSpec((1, 6, D_MODEL), mod_idx),
            pl.BlockSpec((1, D_MODEL), lambda bi: (0, 0)),
            pl.BlockSpec((D_MODEL, CONV_WIDTH), lambda bi: (0, COL_C // CONV_WIDTH)),
            pl.BlockSpec((D_MODEL, CONV_WIDTH), lambda bi: (0, COL_C // CONV_WIDTH + 1)),
            pl.BlockSpec((CONV_K, CONV_WIDTH), lambda bi: (0, 0)),
            vec, vec, vec,
        ],
        out_specs=pl.BlockSpec((seq, CONV_WIDTH), lambda bi: (bi, 0)),
        out_shape=jax.ShapeDtypeStruct((batch * seq, CONV_WIDTH), BF16),
        scratch_shapes=[pltpu.VMEM((seq + 2 * CONV_PAD, CONV_WIDTH), F32)],
        compiler_params=_cparams("arbitrary"),
        name="conv",
    )(x, mod, g, w_in, w_in, dw, b, lg, lb)


def _out_kernel(f_ref, c_ref, a_ref, w_ref, x_ref, mod_ref, bg_ref, g1_ref, g2_ref,
                wg0_ref, wg1_ref, wg2_ref, wg3_ref, wf_ref, wc_ref, wa_ref, ww_ref, wo_ref, x1_ref, h2t_ref):
    x = x_ref[...]
    h1 = _normmod(x, g1_ref[...], mod_ref[0, 1:2, :], mod_ref[0, 0:1, :]).astype(BF16)
    merged = None
    branches = ((f_ref, wf_ref, wg0_ref), (c_ref, wc_ref, wg1_ref), (a_ref, wa_ref, wg2_ref), (w_ref, ww_ref, wg3_ref))
    for i, (br, wt, wg) in enumerate(branches):
        y = _dot(br[...], wt[...])
        gate = jax.nn.sigmoid(_dot(h1, wg[...]) + bg_ref[i:i + 1, :])
        merged = gate * y if merged is None else merged + gate * y
    mix = _dot(merged.astype(BF16), wo_ref[...])
    x1 = x + mod_ref[0, 2:3, :] * mix
    x1_ref[...] = x1
    h2 = _normmod(x1, g2_ref[...], mod_ref[0, 4:5, :], mod_ref[0, 3:4, :])
    h2t_ref[...] = h2.T.astype(BF16)


def _out_proj(f, c, a, w, x, mod, bg, g1, g2, w_in, wf, wc, wa, ww, wo, seq, tt=256):
    t = x.shape[0]
    nb = mod.shape[0]
    tpb = seq // tt
    mod_idx = (lambda i: (i // tpb, 0, 0)) if nb > 1 else (lambda i: (0, 0, 0))
    br = pl.BlockSpec((tt, 512), lambda i: (i, 0))
    wbr = pl.BlockSpec((512, D_MODEL), lambda i: (0, 0))
    vec = pl.BlockSpec((1, D_MODEL), lambda i: (0, 0))
    gl0 = COL_GL // D_MODEL
    wgl = [pl.BlockSpec((D_MODEL, D_MODEL), functools.partial(lambda i, j: (0, j), j=gl0 + k)) for k in range(N_BRANCH)]
    return pl.pallas_call(
        _out_kernel,
        grid=(t // tt,),
        in_specs=[
            br, br, br, br,
            pl.BlockSpec((tt, D_MODEL), lambda i: (i, 0)),
            pl.BlockSpec((1, 6, D_MODEL), mod_idx),
            pl.BlockSpec((N_BRANCH, D_MODEL), lambda i: (0, 0)),
            vec, vec,
            *wgl,
            wbr, wbr, wbr, wbr,
            pl.BlockSpec((D_MODEL, D_MODEL), lambda i: (0, 0)),
        ],
        out_specs=[
            pl.BlockSpec((tt, D_MODEL), lambda i: (i, 0)),
            pl.BlockSpec((D_MODEL, tt), lambda i: (0, i)),
        ],
        out_shape=[jax.ShapeDtypeStruct((t, D_MODEL), F32), jax.ShapeDtypeStruct((D_MODEL, t), BF16)],
        compiler_params=_cparams("arbitrary"),
        name="out_proj",
    )(f, c, a, w, x, mod, bg, g1, g2, w_in, w_in, w_in, w_in, wf, wc, wa, ww, wo)


NOT_TOP = 99.0


def _bf16_twice(x):
    bits = pltpu.bitcast(x.astype(BF16).astype(F32), jnp.uint32)
    return bits | (bits >> 16)


def _route_kernel(h2t_ref, wq_ref, sk_ref, a0_ref, n0_ref, a1_ref, r1_ref,
                  qt_ref, sc_ref, srt_ref, rnk_ref, flag_ref, *, tt):
    ncol = tt // LANES
    qt_ref[...] = _dot(wq_ref[...], h2t_ref[...]).astype(BF16)
    for hp in range(2 * PEER_HEADS):
        sc_ref[hp] = _dot(sk_ref[hp % 2], qt_ref[hp * PEER_HALF:(hp + 1) * PEER_HALF, :])

    key_iota = lax.broadcasted_iota(jnp.int32, (N_KEYS, LANES), 0).astype(F32)

    sub8 = lax.broadcasted_iota(jnp.int32, (8, LANES), 0).astype(F32)

    def _tree(op, xs):
        xs = list(xs)
        while len(xs) > 1:
            xs = [op(xs[i], xs[i + 1]) if i + 1 < len(xs) else xs[i] for i in range(0, len(xs), 2)]
        return xs[0]

    def note_ties(count):
        flag_ref[...] = jnp.maximum(flag_ref[...], jnp.where(count != float(PEER_TOPK), 1.0, 0.0))

    def top16(s, exact):
        rank = jnp.full((N_KEYS, LANES), NOT_TOP, F32)
        tops = []
        for it in range(PEER_TOPK):
            m = jnp.max(s, axis=0, keepdims=True)
            if exact:
                first = jnp.min(jnp.where(s == m, key_iota, float(N_KEYS)), axis=0, keepdims=True)
                hit = key_iota == first
            else:
                hit = s == m
            s = jnp.where(hit, -jnp.inf, s)
            rank = jnp.where(hit, float(it), rank)
            tops.append(m)
        if not exact:
            note_ties(jnp.sum(jnp.where(rank < float(PEER_TOPK), 1.0, 0.0), axis=0, keepdims=True))
        return jnp.concatenate(tops, axis=0), rank

    def combine(h, ls, exact):
        s0 = srt_ref[2 * h, :, ls]
        s1 = srt_ref[2 * h + 1, :, ls]
        cands = [s0[0:1, :] + s1[0:8, :], s0[0:1, :] + s1[8:16, :]]
        poss = [sub8, sub8 + 8.0]
        for a in range(1, 8):
            c = s0[a:a + 1, :] + s1[0:8, :]
            cands.append(jnp.where(sub8 < float(16 // (a + 1)), c, -jnp.inf))
            poss.append(sub8 + float(16 * a))
        cands.append(s0[8:16, :] + s1[0:1, :])
        poss.append((sub8 + 8.0) * 16.0)
        sels = [jnp.zeros_like(c) for c in cands]
        tops = []
        for it in range(PEER_TOPK):
            m = jnp.max(_tree(jnp.maximum, cands), axis=0, keepdims=True)
            if exact:
                firsts = [jnp.where(c == m, p, 999.0) for c, p in zip(cands, poss)]
                first = jnp.min(_tree(jnp.minimum, firsts), axis=0, keepdims=True)
            for j in range(len(cands)):
                hit = (poss[j] == first) if exact else (cands[j] == m)
                cands[j] = jnp.where(hit, -jnp.inf, cands[j])
                sels[j] = jnp.where(hit, 1.0, sels[j])
            tops.append(m)
        z = _tree(jnp.add, [jnp.exp(m - tops[0]) for m in tops])
        inv_z = 1.0 / z
        n_rows = [jnp.sum(sels[0] + sels[1], axis=0, keepdims=True)]
        n_rows += [jnp.sum(sels[a + 1], axis=0, keepdims=True) for a in range(1, 8)]
        n_hi = sels[9]
        if not exact:
            note_ties(_tree(jnp.add, n_rows) + jnp.sum(n_hi, axis=0, keepdims=True))
        rank0 = rnk_ref[2 * h, :, ls]
        rank1 = rnk_ref[2 * h + 1, :, ls]
        n0 = jnp.zeros((N_KEYS, LANES), F32)
        for a in range(8):
            n0 = jnp.where(rank0 == float(a), n_rows[a], n0)
            n0 = jnp.where(rank0 == float(a + 8), n_hi[a:a + 1, :], n0)
        sc0 = sc_ref[2 * h, :, ls]
        sc1 = sc_ref[2 * h + 1, :, ls]
        a0 = jnp.where(rank0 < float(PEER_TOPK), jnp.exp(sc0 - s0[0:1, :]) * inv_z, 0.0)
        a1 = jnp.where(rank1 < float(PEER_TOPK), jnp.exp(sc1 - s1[0:1, :]), 0.0)
        a0_ref[h, :, ls] = _bf16_twice(a0)
        n0_ref[h, :, ls] = _bf16_twice(n0)
        a1_ref[h, :, ls] = pltpu.bitcast(a1.astype(BF16), jnp.uint32)
        r1_ref[h, :, ls] = pltpu.bitcast(rank1.astype(BF16), jnp.uint32)

    def stage1(hp, carry, exact):
        for col in range(ncol):
            ls = slice(col * LANES, (col + 1) * LANES)
            tops, rank = top16(sc_ref[hp, :, ls], exact)
            srt_ref[hp, :, ls] = tops
            rnk_ref[hp, :, ls] = rank
        return carry

    def stage2(h, carry):
        flag_ref[...] = jnp.zeros_like(flag_ref)
        for col in range(ncol):
            combine(h, slice(col * LANES, (col + 1) * LANES), False)

        @pl.when(jnp.max(flag_ref[...]) > 0.0)
        def _():
            for col in range(ncol):
                combine(h, slice(col * LANES, (col + 1) * LANES), True)

        return carry

    flag_ref[...] = jnp.zeros_like(flag_ref)
    lax.fori_loop(0, 2 * PEER_HEADS, functools.partial(stage1, exact=False), 0)

    @pl.when(jnp.max(flag_ref[...]) > 0.0)
    def _():
        lax.fori_loop(0, 2 * PEER_HEADS, functools.partial(stage1, exact=True), 0)

    lax.fori_loop(0, PEER_HEADS, stage2, 0)


def _route(h2t, wq_t, sk, tt=256):
    t = h2t.shape[1]
    out0 = pl.BlockSpec((PEER_HEADS, N_KEYS, tt), lambda i: (0, 0, i))
    out1 = pl.BlockSpec((PEER_HEADS, N_KEYS // 2, tt), lambda i: (0, 0, i))
    shp0 = jax.ShapeDtypeStruct((PEER_HEADS, N_KEYS, t), jnp.uint32)
    shp1 = jax.ShapeDtypeStruct((PEER_HEADS, N_KEYS // 2, t), jnp.uint32)
    return pl.pallas_call(
        functools.partial(_route_kernel, tt=tt),
        grid=(t // tt,),
        in_specs=[
            pl.BlockSpec((D_MODEL, tt), lambda i: (0, i)),
            pl.BlockSpec((2 * PEER_HEADS * PEER_HALF, D_MODEL), lambda i: (0, 0)),
            pl.BlockSpec((2, N_KEYS, PEER_HALF), lambda i: (0, 0, 0)),
        ],
        out_specs=[out0, out0, out1, out1],
        out_shape=[shp0, shp0, shp1, shp1],
        scratch_shapes=[
            pltpu.VMEM((2 * PEER_HEADS * PEER_HALF, tt), BF16),
            pltpu.VMEM((2 * PEER_HEADS, N_KEYS, tt), F32),
            pltpu.VMEM((2 * PEER_HEADS, PEER_TOPK, tt), F32),
            pltpu.VMEM((2 * PEER_HEADS, N_KEYS, tt), F32),
            pltpu.VMEM((8, LANES), F32),
        ],
        compiler_params=_cparams("arbitrary"),
        name="peer_route",
    )(h2t, wq_t, sk)


def _gelu_tanh(x):
    c = 2.0 * 0.7978845608028654
    neg_u = x * ((-c * 0.044715) * (x * x) - c)
    return x / (1.0 + jnp.exp(neg_u))


def _peer_step(h2t_ref, a0_ref, n0_ref, a1_ref, r1_ref, u_ref, vt_ref, acc_ref,
               s_new, s_cur, w_cur, w_done, i0, *, tt, n_i):
    ncol = tt // LANES
    zero = jnp.zeros((PACK_ROWS, LANES), BF16)
    rows_per_piece = MXU_PIECE_ROWS // N_KEYS
    for il in range(n_i):
        mr = slice(il // rows_per_piece * MXU_PIECE_ROWS, (il // rows_per_piece + 1) * MXU_PIECE_ROWS)
        if il % rows_per_piece == 0:
            s_new[mr, :] = _dot(u_ref[mr, :], h2t_ref[...])
        for col in range(ncol):
            if il % rows_per_piece == rows_per_piece // 2 and col == 0:
                acc_ref[mr, :] += _dot(vt_ref[mr, :], w_done[...])
            ls = slice(col * LANES, (col + 1) * LANES)
            n0 = [pltpu.bitcast(jnp.broadcast_to(n0_ref[h, pl.ds(i0, n_i), ls][il:il + 1, :], (8, LANES)), BF16)
                  for h in range(PEER_HEADS)]
            a0 = [pltpu.bitcast(jnp.broadcast_to(a0_ref[h, pl.ds(i0, n_i), ls][il:il + 1, :], (8, LANES)), BF16)
                  for h in range(PEER_HEADS)]
            for k in range(N_KEYS // PACK_ROWS):
                pr = slice(k * PACK_ROWS // 2, (k + 1) * PACK_ROWS // 2)
                gate = None
                for h in range(PEER_HEADS):
                    r1 = pltpu.bitcast(r1_ref[h, pr, ls], BF16)
                    a1 = pltpu.bitcast(a1_ref[h, pr, ls], BF16)
                    term = jnp.where(r1 < n0[h], a0[h] * a1, zero)
                    gate = term if gate is None else gate + term
                rows = slice(il * N_KEYS + k * PACK_ROWS, il * N_KEYS + (k + 1) * PACK_ROWS)
                act = _gelu_tanh(s_cur[rows, ls])
                w_cur[rows, ls] = gate * act.astype(BF16)


def _peer_kernel(h2t_ref, a0_ref, n0_ref, a1_ref, r1_ref, u_ref, vt_ref, x1_ref, mod_ref,
                 o_ref, s0_ref, s1_ref, w0_ref, w1_ref, acc_ref, *, tt, ec, ne, total):
    g = pl.program_id(0)
    n_i = ec // N_KEYS
    assert n_i == 8, "one expert chunk must cover exactly one sublane tile of half-0 keys"
    g_gate = jnp.clip(g - 1, 0, total - 1)
    c_out = jnp.clip(g - 2, 0, total - 1) % ne
    i0 = pl.multiple_of((g_gate % ne) * n_i, n_i)

    @pl.when(g == 0)
    def _():
        s1_ref[...] = jnp.zeros_like(s1_ref)
        w0_ref[...] = jnp.zeros_like(w0_ref)
        w1_ref[...] = jnp.zeros_like(w1_ref)

    @pl.when(c_out == 0)
    def _():
        acc_ref[...] = jnp.zeros_like(acc_ref)

    step = functools.partial(_peer_step, h2t_ref, a0_ref, n0_ref, a1_ref, r1_ref, u_ref, vt_ref, acc_ref,
                             tt=tt, n_i=n_i)

    @pl.when(g % 2 == 0)
    def _():
        step(s0_ref, s1_ref, w1_ref, w0_ref, i0)

    @pl.when(g % 2 == 1)
    def _():
        step(s1_ref, s0_ref, w0_ref, w1_ref, i0)

    @pl.when((g >= 2) & (c_out == ne - 1))
    def _():
        o_ref[...] = x1_ref[...] + mod_ref[0, 5:6, :] * acc_ref[...].T


def _peer(h2t, a0, n0, a1, r1, u, vt, x1, mod, seq, tt=512, ec=1024):
    t = x1.shape[0]
    nb = mod.shape[0]
    tpb = seq // tt
    ne = N_EXPERTS // ec
    total = (t // tt) * ne
    assert ec == D_MODEL, "the kernel walks U-chunk rows and out^T rows with one 128-row block index"

    def st_a(g):
        return jnp.minimum(g, total - 1)

    def st_b(g):
        return jnp.clip(g - 1, 0, total - 1)

    def st_c(g):
        return jnp.clip(g - 2, 0, total - 1)

    mod_idx = (lambda g: (st_c(g) // ne // tpb, 0, 0)) if nb > 1 else (lambda g: (0, 0, 0))
    rt0 = pl.BlockSpec((PEER_HEADS, N_KEYS, tt), lambda g: (0, 0, st_b(g) // ne))
    rt1 = pl.BlockSpec((PEER_HEADS, N_KEYS // 2, tt), lambda g: (0, 0, st_b(g) // ne))
    return pl.pallas_call(
        functools.partial(_peer_kernel, tt=tt, ec=ec, ne=ne, total=total),
        grid=(total + 2,),
        in_specs=[
            pl.BlockSpec((D_MODEL, tt), lambda g: (0, st_a(g) // ne)),
            rt0, rt0, rt1, rt1,
            pl.BlockSpec((ec, D_MODEL), lambda g: (st_a(g) % ne, 0)),
            pl.BlockSpec((D_MODEL, ec), lambda g: (0, st_c(g) % ne)),
            pl.BlockSpec((tt, D_MODEL), lambda g: (st_c(g) // ne, 0)),
            pl.BlockSpec((1, 6, D_MODEL), mod_idx),
        ],
        out_specs=pl.BlockSpec((tt, D_MODEL), lambda g: (st_c(g) // ne, 0)),
        out_shape=jax.ShapeDtypeStruct((t, D_MODEL), F32),
        scratch_shapes=[
            pltpu.VMEM((ec, tt), F32), pltpu.VMEM((ec, tt), F32),
            pltpu.VMEM((ec, tt), BF16), pltpu.VMEM((ec, tt), BF16),
            pltpu.VMEM((D_MODEL, tt), F32),
        ],
        compiler_params=_cparams("arbitrary"),
        name="peer_mix",
    )(h2t, a0, n0, a1, r1, u, vt, x1, mod)


def _final_norm_kernel(x_ref, g_ref, o_ref):
    x = x_ref[...]
    ms = jnp.mean(x * x, axis=-1, keepdims=True)
    o_ref[...] = x * lax.rsqrt(ms + EPS) * g_ref[...]


def _final_norm(x, g, tt=512):
    t = x.shape[0]
    return pl.pallas_call(
        _final_norm_kernel,
        grid=(t // tt,),
        in_specs=[pl.BlockSpec((tt, D_MODEL), lambda i: (i, 0)), pl.BlockSpec((1, D_MODEL), lambda i: (0, 0))],
        out_specs=pl.BlockSpec((tt, D_MODEL), lambda i: (i, 0)),
        out_shape=jax.ShapeDtypeStruct((t, D_MODEL), F32),
        compiler_params=_cparams("arbitrary"),
        name="final_norm",
    )(x, g)


def _rope_tables(seq):
    n_rows = seq // GRID_W
    row = jnp.repeat(jnp.arange(n_rows), GRID_W).astype(F32)
    col = jnp.tile(jnp.arange(GRID_W), n_rows).astype(F32)
    n_f = HEAD_DIM // 4
    inv = 1.0 / (ROPE_THETA ** (jnp.arange(n_f, dtype=F32) / n_f))
    ang = jnp.concatenate([row[:, None] * inv, col[:, None] * inv], axis=-1)
    cos, sin = jnp.cos(ang), jnp.sin(ang)
    cos_t = jnp.tile(jnp.concatenate([cos, cos], axis=-1), (1, N_HEADS))
    sin_t = jnp.tile(jnp.concatenate([-sin, sin], axis=-1), (1, N_HEADS))
    return cos_t, sin_t


def _pad_heads(kv):
    z = jnp.zeros_like(kv[:, :HEAD_DIM])
    h0, h1 = kv[:, :HEAD_DIM], kv[:, HEAD_DIM:]
    return jnp.concatenate([h0, z, z, h0, h1, z, z, h1], axis=-1).astype(BF16)


def _layer(x, mod, lp, consts, batch, seq, ctx_kv):
    is_ctx = ctx_kv is None
    g1 = lp["norm1_g"]
    w_in = lp["w_in"]
    qa, qw, kxa, vxa, kxw, vxw, kvf = _in_qkv(
        x, mod, g1, w_in, seq, consts["bd"], lp["q_norm_g"], lp["k_norm_g"],
        None if is_ctx else consts["rope"])
    dft = consts["dft_ctx"] if is_ctx else consts["dft_lat"]
    f = _fourier(x, mod, g1, w_in, consts["ccs"], dft, batch, seq, min(seq, 512))
    cv = _conv(x, mod, g1, w_in, lp["conv_dw"], lp["conv_b"], lp["conv_ln_g"], lp["conv_ln_b"], batch, seq)
    no_sink = jnp.full((N_HEADS,), NEG_INF, F32)
    if is_ctx:
        a = _attention(qa, kxa, vxa, no_sink, batch, seq)
        w = _attention(qw, kxw, vxw, lp["win_sink"], batch, seq)
    else:
        ka_c, va_c, kw_c, vw_c = ctx_kv
        a = _attention(qa, kxa, vxa, no_sink, batch, seq, ctx=(ka_c, va_c))
        w = _attention(qw, kxw, vxw, lp["win_sink"], batch, seq, ctx=(kw_c, vw_c), windowed=True)
    x1, h2t = _out_proj(f, cv, a, w, x, mod, lp["b_gate"], g1, lp["norm2_g"], w_in,
                        lp["w_fourier_out"], lp["w_conv_out"], lp["w_attn_out"], lp["w_win_out"],
                        lp["w_out"], seq)
    a0, n0, a1, r1 = _route(h2t, lp["w_peer_q_t"], lp["peer_subkeys"])
    x2 = _peer(h2t, a0, n0, a1, r1, lp["peer_u"], lp["peer_v_t"], x1, mod, seq)
    return x2, kvf


def kernel(x_prompt, x_sample, cache_attn_k, cache_attn_v, cache_win_k, cache_win_v, c, c_ctx, norm1_g, norm2_g, w_ada, b_ada, w_in, b_gate, w_fourier_out, conv_dw, conv_b, conv_ln_g, conv_ln_b, w_conv_out, q_norm_g, k_norm_g, w_attn_out, win_sink, w_win_out, w_out, w_peer_q, peer_subkeys, peer_u, peer_v, final_norm_g):
    b_ctx, s_ctx, _ = x_prompt.shape
    b_lat, s_lat, _ = x_sample.shape
    past = cache_attn_k.shape[2]

    cond = jnp.concatenate([c_ctx[None, :], c, jnp.zeros((16 - 1 - b_lat, D_MODEL), F32)], axis=0)
    ada = _ada_all(cond, w_ada, b_ada).reshape(DEPTH, 16, 6, D_MODEL)

    consts = {
        "bd": jnp.asarray(np.kron(np.eye(N_HEADS), np.ones((HEAD_DIM, HEAD_DIM))), BF16),
        "ccs": jnp.asarray(_channel_dft_tables()).astype(BF16),
        "dft_ctx": jnp.asarray(_dft_tables(s_ctx)).astype(BF16),
        "dft_lat": jnp.asarray(_dft_tables(s_lat)).astype(BF16),
        "rope": _rope_tables(s_lat),
    }

    xp = x_prompt.reshape(b_ctx * s_ctx, D_MODEL)
    xs = x_sample.reshape(b_lat * s_lat, D_MODEL)
    kv_out = []
    for l in range(DEPTH):
        lp = {
            "norm1_g": norm1_g[l][None, :], "norm2_g": norm2_g[l][None, :],
            "w_in": w_in[l].astype(BF16), "b_gate": b_gate[l],
            "w_fourier_out": w_fourier_out[l].astype(BF16),
            "conv_dw": conv_dw[l], "conv_b": conv_b[l][None, :],
            "conv_ln_g": conv_ln_g[l][None, :], "conv_ln_b": conv_ln_b[l][None, :],
            "w_conv_out": w_conv_out[l].astype(BF16),
            "q_norm_g": jnp.tile(q_norm_g[l], N_HEADS)[None, :],
            "k_norm_g": jnp.tile(k_norm_g[l], 2)[None, :],
            "w_attn_out": w_attn_out[l].astype(BF16), "win_sink": win_sink[l],
            "w_win_out": w_win_out[l].astype(BF16), "w_out": w_out[l].astype(BF16),
            "w_peer_q_t": w_peer_q[l].T.astype(BF16),
            "peer_subkeys": peer_subkeys[l].astype(BF16),
            "peer_u": peer_u[l].astype(BF16),
            "peer_v_t": peer_v[l].T.astype(BF16),
        }
        mod_ctx = ada[l, 0:1]
        mod_lat = ada[l, 1:1 + b_lat]
        xp, kvf = _layer(xp, mod_ctx, lp, consts, b_ctx, s_ctx, None)
        kv_out.append(kvf)
        cached = tuple(
            _pad_heads(cc[:, l].reshape(b_lat * past, ATTN_KV))
            for cc in (cache_attn_k, cache_attn_v, cache_win_k, cache_win_v))
        xs, _ = _layer(xs, mod_lat, lp, consts, b_lat, s_lat, cached)

    y_prompt = _final_norm(xp, final_norm_g[None, :]).reshape(b_ctx, s_ctx, D_MODEL)
    y_sample = _final_norm(xs, final_norm_g[None, :]).reshape(b_lat, s_lat, D_MODEL)
    outs = []
    for j in range(4):
        per_layer = [kvf[:, j * 128:(j + 1) * 128].reshape(b_ctx, s_ctx, 2, HEAD_DIM) for kvf in kv_out]
        outs.append(jnp.stack(per_layer, axis=1))
    return (y_prompt, y_sample, outs[0], outs[1], outs[2], outs[3])
```

```python
import functools

import numpy as np
import jax
import jax.numpy as jnp
from jax import lax
from jax.experimental import pallas as pl
from jax.experimental.pallas import tpu as pltpu

F32 = jnp.float32
BF16 = jnp.bfloat16

D_MODEL = 1024
DEPTH = 4
GRID_W = 64
HEAD_DIM = 64
FNET_GROUP_DIM = 64
FNET_WIDTH = 512
CONV_WIDTH = 512
CONV_K = 31
N_HEADS = 8
ATTN_Q = 512
ATTN_KV = 128
WINDOW = 128
ROPE_THETA = 10000.0
N_BRANCH = 4
COL_F, COL_C, COL_QKV, COL_GL = 0, 512, 1536, 3072
IN_COLS = 7168
N_KEYS = 128
N_EXPERTS = N_KEYS * N_KEYS
PEER_HEADS = 8
PEER_HALF = 128
PEER_TOPK = 16
EPS = 1e-6
NEG_INF = -1e30
LANES = 128
PACK_ROWS = 16
MXU_PIECE_ROWS = 512
ATTN_Q_ROWS = 512

VMEM_LIMIT = 56 * 1024 * 1024


def _cparams(*sem):
    return pltpu.CompilerParams(dimension_semantics=sem, vmem_limit_bytes=VMEM_LIMIT)


def _dot(a, b):
    return jnp.dot(a, b, preferred_element_type=F32)


def _dot_nt(a, b):
    return lax.dot_general(a, b, (((1,), (1,)), ((), ())), preferred_element_type=F32)


def _normmod(x, g, sc, sh):
    ms = jnp.mean(x * x, axis=-1, keepdims=True)
    y = x * lax.rsqrt(ms + EPS)
    return (y * g) * (1.0 + sc) + sh


def _ada_kernel(c_ref, w_ref, b_ref, o_ref):
    c = c_ref[...]
    a = (c * jax.nn.sigmoid(c)).astype(BF16)
    o_ref[0] = _dot(a, w_ref[0].astype(BF16)) + b_ref[0]


def _ada_all(cond16, w_ada, b_ada):
    tn = 1536
    n = w_ada.shape[2]
    return pl.pallas_call(
        _ada_kernel,
        grid=(DEPTH, n // tn),
        in_specs=[
            pl.BlockSpec((16, D_MODEL), lambda l, j: (0, 0)),
            pl.BlockSpec((1, D_MODEL, tn), lambda l, j: (l, 0, j)),
            pl.BlockSpec((1, 1, tn), lambda l, j: (l, 0, j)),
        ],
        out_specs=pl.BlockSpec((1, 16, tn), lambda l, j: (l, 0, j)),
        out_shape=jax.ShapeDtypeStruct((DEPTH, 16, n), F32),
        compiler_params=_cparams("arbitrary", "arbitrary"),
        name="ada",
    )(cond16, w_ada, b_ada.reshape(DEPTH, 1, n))


def _seg_mean_sq(x, bd):
    sq = x * x
    hi = sq.astype(BF16)
    lo = (sq - hi.astype(F32)).astype(BF16)
    return (_dot(hi, bd) + _dot(lo, bd)) * (1.0 / HEAD_DIM)


def _rope(x, cos, sin_signed):
    w = x.shape[1]
    fwd = pltpu.roll(x, HEAD_DIM // 2, axis=1)
    bwd = pltpu.roll(x, w - HEAD_DIM // 2, axis=1)
    lane = lax.broadcasted_iota(jnp.int32, x.shape, 1)
    first = (lane % HEAD_DIM) < (HEAD_DIM // 2)
    partner = jnp.where(first, bwd, fwd)
    return x * cos + partner * sin_signed


def _store_head_padded(o_ref, k):
    kr = pltpu.roll(k, HEAD_DIM, axis=1)
    lane = lax.broadcasted_iota(jnp.int32, k.shape, 1)
    lo = lane < HEAD_DIM
    zero = jnp.zeros_like(k)
    o_ref[:, 0:128] = jnp.where(lo, k, zero).astype(BF16)
    o_ref[:, 128:256] = jnp.where(lo, zero, kr).astype(BF16)
    o_ref[:, 256:384] = jnp.where(lo, kr, zero).astype(BF16)
    o_ref[:, 384:512] = jnp.where(lo, zero, k).astype(BF16)


def _in_qkv_kernel(*refs, use_rope):
    if use_rope:
        (x_ref, mod_ref, g_ref, w_ref, bd_ref, qg_ref, kg_ref, cos_ref, sin_ref,
         qa_ref, qw_ref, kxa_ref, vxa_ref, kxw_ref, vxw_ref, kvf_ref) = refs
    else:
        (x_ref, mod_ref, g_ref, w_ref, bd_ref, qg_ref, kg_ref,
         qa_ref, qw_ref, kxa_ref, vxa_ref, kxw_ref, vxw_ref, kvf_ref) = refs
    h = _normmod(x_ref[...], g_ref[...], mod_ref[0, 1:2, :], mod_ref[0, 0:1, :]).astype(BF16)
    acc = _dot(h, w_ref[...])
    aq = acc[:, 0:512]
    ak = acc[:, 512:640]
    av = acc[:, 640:768]
    wq = acc[:, 768:1280]
    wk = acc[:, 1280:1408]
    wv = acc[:, 1408:1536]
    bd = bd_ref[...]
    aq = aq * lax.rsqrt(_seg_mean_sq(aq, bd) + EPS) * qg_ref[...]
    ak = ak * lax.rsqrt(_seg_mean_sq(ak, bd[0:128, 0:128]) + EPS) * kg_ref[...]
    if use_rope:
        cos = cos_ref[...]
        sin = sin_ref[...]
        aq = _rope(aq, cos, sin)
        wq = _rope(wq, cos, sin)
        ak = _rope(ak, cos[:, 0:128], sin[:, 0:128])
        wk = _rope(wk, cos[:, 0:128], sin[:, 0:128])
    scale = HEAD_DIM ** -0.5
    qa_ref[...] = (aq * scale).astype(BF16)
    qw_ref[...] = (wq * scale).astype(BF16)
    _store_head_padded(kxa_ref, ak)
    _store_head_padded(vxa_ref, av)
    _store_head_padded(kxw_ref, wk)
    _store_head_padded(vxw_ref, wv)
    kvf_ref[:, 0:128] = ak
    kvf_ref[:, 128:256] = av
    kvf_ref[:, 256:384] = wk
    kvf_ref[:, 384:512] = wv


def _in_qkv(x, mod, g, w_in, seq, bd, qg8, kg2, rope_tabs, tt=512):
    t = x.shape[0]
    nb = mod.shape[0]
    tpb = seq // tt
    mod_idx = (lambda i: (i // tpb, 0, 0)) if nb > 1 else (lambda i: (0, 0, 0))
    use_rope = rope_tabs is not None
    ncol = 1536
    in_specs = [
        pl.BlockSpec((tt, D_MODEL), lambda i: (i, 0)),
        pl.BlockSpec((1, 6, D_MODEL), mod_idx),
        pl.BlockSpec((1, D_MODEL), lambda i: (0, 0)),
        pl.BlockSpec((D_MODEL, ncol), lambda i: (0, COL_QKV // ncol)),
        pl.BlockSpec((512, 512), lambda i: (0, 0)),
        pl.BlockSpec((1, 512), lambda i: (0, 0)),
        pl.BlockSpec((1, 128), lambda i: (0, 0)),
    ]
    args = [x, mod, g, w_in, bd, qg8, kg2]
    if use_rope:
        in_specs += [pl.BlockSpec((tt, 512), lambda i: (i % tpb, 0))] * 2
        args += list(rope_tabs)
    tok = pl.BlockSpec((tt, 512), lambda i: (i, 0))
    return pl.pallas_call(
        functools.partial(_in_qkv_kernel, use_rope=use_rope),
        grid=(t // tt,),
        in_specs=in_specs,
        out_specs=[tok] * 7,
        out_shape=[jax.ShapeDtypeStruct((t, 512), BF16)] * 6 + [jax.ShapeDtypeStruct((t, 512), F32)],
        compiler_params=_cparams("arbitrary"),
        name="in_qkv",
    )(*args)


def _attn_kernel(*refs, tq, seq, n_ctx, windowed):
    if n_ctx:
        sink_ref, q_ref, kx_ref, vx_ref, kc_ref, vc_ref, o_ref = refs
    else:
        sink_ref, q_ref, kx_ref, vx_ref, o_ref = refs
    i = pl.program_id(1)
    if windowed:
        span = tq + 2 * WINDOW
        start = jnp.clip(i * tq - WINDOW, 0, seq - span)
        start = pl.multiple_of(start, WINDOW)
        qpos = i * tq + lax.broadcasted_iota(jnp.int32, (tq, span), 0)
        kpos = start + lax.broadcasted_iota(jnp.int32, (tq, span), 1)
        valid = jnp.abs(kpos - qpos) <= WINDOW
    for jb in range(4):
        q2 = q_ref[:, jb * 128:(jb + 1) * 128]
        acc = None
        for e in range(2):
            head = jb * 2 + e
            v_idx = (jb // 2) * 2 + e
            cols = slice(v_idx * 128, (v_idx + 1) * 128)
            if windowed:
                k_loc = kx_ref[pl.ds(start, span), cols]
                v_loc = vx_ref[pl.ds(start, span), cols]
            else:
                k_loc = kx_ref[:, cols]
                v_loc = vx_ref[:, cols]
            s = _dot_nt(q2, k_loc)
            if windowed:
                s = jnp.where(valid, s, NEG_INF)
            sink = sink_ref[head]
            m = jnp.maximum(jnp.max(s, axis=-1, keepdims=True), sink)
            if n_ctx:
                s_c = _dot_nt(q2, kc_ref[:, cols])
                m = jnp.maximum(m, jnp.max(s_c, axis=-1, keepdims=True))
            p = jnp.exp(s - m)
            l = jnp.sum(p, axis=-1, keepdims=True) + jnp.exp(sink - m)
            o = _dot(p.astype(BF16), v_loc)
            if n_ctx:
                p_c = jnp.exp(s_c - m)
                l = l + jnp.sum(p_c, axis=-1, keepdims=True)
                o = o + _dot(p_c.astype(BF16), vc_ref[:, cols])
            o = o / l
            acc = o if acc is None else acc + o
        o_ref[:, jb * 128:(jb + 1) * 128] = acc.astype(o_ref.dtype)


def _attention(q, kx, vx, sink, batch, seq, ctx=None, windowed=False):
    tq = min(seq, ATTN_Q_ROWS)
    assert not windowed or seq >= tq + 2 * WINDOW, "the banded key span must fit inside one sequence"
    n_ctx = 0 if ctx is None else ctx[0].shape[0] // batch
    nq = seq // tq
    in_specs = [
        pl.BlockSpec(memory_space=pltpu.SMEM),
        pl.BlockSpec((tq, 512), lambda b, i: (b * nq + i, 0)),
        pl.BlockSpec((seq, 512), lambda b, i: (b, 0)),
        pl.BlockSpec((seq, 512), lambda b, i: (b, 0)),
    ]
    args = [sink, q, kx, vx]
    if n_ctx:
        in_specs += [pl.BlockSpec((n_ctx, 512), lambda b, i: (b, 0))] * 2
        args += list(ctx)
    return pl.pallas_call(
        functools.partial(_attn_kernel, tq=tq, seq=seq, n_ctx=n_ctx, windowed=windowed),
        grid=(batch, nq),
        in_specs=in_specs,
        out_specs=pl.BlockSpec((tq, 512), lambda b, i: (b * nq + i, 0)),
        out_shape=jax.ShapeDtypeStruct((batch * seq, 512), BF16),
        compiler_params=_cparams("arbitrary", "arbitrary"),
        name="attn_win" if windowed else "attn",
    )(*args)


def _fourier_kernel(x_ref, mod_ref, g_ref, w_ref, ccs_ref, dft_ref, o_ref, z_ref, *, seq, scale):
    @pl.when(pl.program_id(1) == 0)
    def _():
        def body(r, carry):
            base = pl.multiple_of(r * PROJ_ROWS, PROJ_ROWS)
            h = _normmod(x_ref[pl.ds(base, PROJ_ROWS), :], g_ref[...], mod_ref[0, 1:2, :], mod_ref[0, 0:1, :])
            f = _dot(h.astype(BF16), w_ref[...]).astype(BF16)
            z = _dot(f, ccs_ref[...])
            z_ref[pl.ds(base, PROJ_ROWS), :] = z[:, 0:FNET_WIDTH].astype(BF16)
            z_ref[pl.ds(seq + base, PROJ_ROWS), :] = z[:, FNET_WIDTH:].astype(BF16)
            return carry

        lax.fori_loop(0, seq // PROJ_ROWS, body, 0)

    o_ref[...] = (_dot(dft_ref[...], z_ref[...]) * scale).astype(o_ref.dtype)


def _fourier(x, mod, g, w_in, ccs, dft, batch, seq, tr):
    nr = seq // tr
    nb = mod.shape[0]
    mod_idx = (lambda b, i: (b, 0, 0)) if nb > 1 else (lambda b, i: (0, 0, 0))
    return pl.pallas_call(
        functools.partial(_fourier_kernel, seq=seq, scale=float((seq * FNET_GROUP_DIM) ** -0.5)),
        grid=(batch, nr),
        in_specs=[
            pl.BlockSpec((seq, D_MODEL), lambda b, i: (b, 0)),
            pl.BlockSpec((1, 6, D_MODEL), mod_idx),
            pl.BlockSpec((1, D_MODEL), lambda b, i: (0, 0)),
            pl.BlockSpec((D_MODEL, FNET_WIDTH), lambda b, i: (0, COL_F // FNET_WIDTH)),
            pl.BlockSpec((FNET_WIDTH, 2 * FNET_WIDTH), lambda b, i: (0, 0)),
            pl.BlockSpec((tr, 2 * seq), lambda b, i: (i, 0)),
        ],
        out_specs=pl.BlockSpec((tr, FNET_WIDTH), lambda b, i: (b * nr + i, 0)),
        out_shape=jax.ShapeDtypeStruct((batch * seq, FNET_WIDTH), BF16),
        scratch_shapes=[pltpu.VMEM((2 * seq, FNET_WIDTH), BF16)],
        compiler_params=_cparams("arbitrary", "arbitrary"),
        name="fourier",
    )(x, mod, g, w_in, ccs, dft)


def _dft_tables(seq):
    k = np.arange(seq, dtype=np.int64)
    ang = 2.0 * np.pi * ((k[:, None] * k[None, :]) % seq).astype(np.float64) / seq
    return np.concatenate([np.cos(ang), -np.sin(ang)], axis=1).astype(np.float32)


def _channel_dft_tables():
    g = FNET_GROUP_DIM
    k = np.arange(g, dtype=np.int64)
    ang = 2.0 * np.pi * ((k[:, None] * k[None, :]) % g).astype(np.float64) / g
    eye = np.eye(FNET_WIDTH // g)
    return np.concatenate([np.kron(eye, np.cos(ang)), np.kron(eye, np.sin(ang))], axis=1).astype(np.float32)


CONV_PAD = 16
CONV_ROWS = 128
PROJ_ROWS = 256


def _conv_kernel(x_ref, mod_ref, g_ref, wa_ref, wg_ref, dw_ref, b_ref, lg_ref, lb_ref, o_ref, hp_ref, *, seq):
    zeros = jnp.zeros((CONV_PAD, CONV_WIDTH), F32)
    hp_ref[0:CONV_PAD, :] = zeros
    hp_ref[seq + CONV_PAD:seq + 2 * CONV_PAD, :] = zeros

    def glu_body(r, carry):
        base = pl.multiple_of(r * PROJ_ROWS, PROJ_ROWS)
        h = _normmod(x_ref[pl.ds(base, PROJ_ROWS), :], g_ref[...], mod_ref[0, 1:2, :], mod_ref[0, 0:1, :])
        h = h.astype(BF16)
        hp_ref[pl.ds(base + CONV_PAD, PROJ_ROWS), :] = _dot(h, wa_ref[...]) * jax.nn.sigmoid(_dot(h, wg_ref[...]))
        return carry

    lax.fori_loop(0, seq // PROJ_ROWS, glu_body, 0)

    blk_rows = CONV_ROWS + 2 * CONV_PAD

    def conv_body(r, carry):
        base = pl.multiple_of(r * CONV_ROWS, CONV_ROWS)
        outs = []
        for cch in range(CONV_WIDTH // LANES):
            ls = slice(cch * LANES, (cch + 1) * LANES)
            blk = hp_ref[pl.ds(base, blk_rows), ls]
            acc = jnp.zeros((CONV_ROWS, LANES), F32)
            for s in range(8):
                blk_s = blk if s == 0 else pltpu.roll(blk, blk_rows - s, axis=0)
                for m in range(4):
                    off = 8 * m + s
                    k = off - 1
                    if 0 <= k < CONV_K:
                        acc = acc + blk_s[8 * m:8 * m + CONV_ROWS, :] * dw_ref[k:k + 1, ls]
            outs.append(acc)
        y = jnp.concatenate(outs, axis=1) + b_ref[...]
        mu = jnp.mean(y, axis=-1, keepdims=True)
        yc = y - mu
        var = jnp.mean(yc * yc, axis=-1, keepdims=True)
        hn = yc * lax.rsqrt(var + EPS) * lg_ref[...] + lb_ref[...]
        o_ref[pl.ds(base, CONV_ROWS), :] = (hn * jax.nn.sigmoid(hn)).astype(o_ref.dtype)
        return carry

    lax.fori_loop(0, seq // CONV_ROWS, conv_body, 0)


def _conv(x, mod, g, w_in, dw, b, lg, lb, batch, seq):
    vec = pl.BlockSpec((1, CONV_WIDTH), lambda bi: (0, 0))
    nb = mod.shape[0]
    mod_idx = (lambda bi: (bi, 0, 0)) if nb > 1 else (lambda bi: (0, 0, 0))
    return pl.pallas_call(
        functools.partial(_conv_kernel, seq=seq),
        grid=(batch,),
        in_specs=[
            pl.BlockSpec((seq, D_MODEL), lambda bi: (bi, 0)),
            pl.BlockSpec((1, 6, D_MODEL), mod_idx),
            pl.BlockSpec((1, D_MODEL), lambda bi: (0, 0)),
            pl.BlockSpec((D_MODEL, CONV_WIDTH), lambda bi: (0, COL_C // CONV_WIDTH)),
            pl.BlockSpec((D_MODEL, CONV_WIDTH), lambda bi: (0, COL_C // CONV_WIDTH + 1)),
            pl.BlockSpec((CONV_K, CONV_WIDTH), lambda bi: (0, 0)),
            vec, vec, vec,
        ],
        out_specs=pl.BlockSpec((seq, CONV_WIDTH), lambda bi: (bi, 0)),
        out_shape=jax.ShapeDtypeStruct((batch * seq, CONV_WIDTH), BF16),
        scratch_shapes=[pltpu.VMEM((seq + 2 * CONV_PAD, CONV_WIDTH), F32)],
        compiler_params=_cparams("arbitrary"),
        name="conv",
    )(x, mod, g, w_in, w_in, dw, b, lg, lb)


def _out_kernel(f_ref, c_ref, a_ref, w_ref, x_ref, mod_ref, bg_ref, g1_ref, g2_ref,
                wg0_ref, wg1_ref, wg2_ref, wg3_ref, wf_ref, wc_ref, wa_ref, ww_ref, wo_ref, x1_ref, h2t_ref):
    x = x_ref[...]
    h1 = _normmod(x, g1_ref[...], mod_ref[0, 1:2, :], mod_ref[0, 0:1, :]).astype(BF16)
    merged = None
    branches = ((f_ref, wf_ref, wg0_ref), (c_ref, wc_ref, wg1_ref), (a_ref, wa_ref, wg2_ref), (w_ref, ww_ref, wg3_ref))
    for i, (br, wt, wg) in enumerate(branches):
        y = _dot(br[...], wt[...])
        gate = jax.nn.sigmoid(_dot(h1, wg[...]) + bg_ref[i:i + 1, :])
        merged = gate * y if merged is None else merged + gate * y
    mix = _dot(merged.astype(BF16), wo_ref[...])
    x1 = x + mod_ref[0, 2:3, :] * mix
    x1_ref[...] = x1
    h2 = _normmod(x1, g2_ref[...], mod_ref[0, 4:5, :], mod_ref[0, 3:4, :])
    h2t_ref[...] = h2.T.astype(BF16)


def _out_proj(f, c, a, w, x, mod, bg, g1, g2, w_in, wf, wc, wa, ww, wo, seq, tt=256):
    t = x.shape[0]
    nb = mod.shape[0]
    tpb = seq // tt
    mod_idx = (lambda i: (i // tpb, 0, 0)) if nb > 1 else (lambda i: (0, 0, 0))
    br = pl.BlockSpec((tt, 512), lambda i: (i, 0))
    wbr = pl.BlockSpec((512, D_MODEL), lambda i: (0, 0))
    vec = pl.BlockSpec((1, D_MODEL), lambda i: (0, 0))
    gl0 = COL_GL // D_MODEL
    wgl = [pl.BlockSpec((D_MODEL, D_MODEL), functools.partial(lambda i, j: (0, j), j=gl0 + k)) for k in range(N_BRANCH)]
    return pl.pallas_call(
        _out_kernel,
        grid=(t // tt,),
        in_specs=[
            br, br, br, br,
            pl.BlockSpec((tt, D_MODEL), lambda i: (i, 0)),
            pl.BlockSpec((1, 6, D_MODEL), mod_idx),
            pl.BlockSpec((N_BRANCH, D_MODEL), lambda i: (0, 0)),
            vec, vec,
            *wgl,
            wbr, wbr, wbr, wbr,
            pl.BlockSpec((D_MODEL, D_MODEL), lambda i: (0, 0)),
        ],
        out_specs=[
            pl.BlockSpec((tt, D_MODEL), lambda i: (i, 0)),
            pl.BlockSpec((D_MODEL, tt), lambda i: (0, i)),
        ],
        out_shape=[jax.ShapeDtypeStruct((t, D_MODEL), F32), jax.ShapeDtypeStruct((D_MODEL, t), BF16)],
        compiler_params=_cparams("arbitrary"),
        name="out_proj",
    )(f, c, a, w, x, mod, bg, g1, g2, w_in, w_in, w_in, w_in, wf, wc, wa, ww, wo)


NOT_TOP = 99.0


def _bf16_twice(x):
    bits = pltpu.bitcast(x.astype(BF16).astype(F32), jnp.uint32)
    return bits | (bits >> 16)


def _route_kernel(h2t_ref, wq_ref, sk_ref, a0_ref, n0_ref, a1_ref, r1_ref,
                  qt_ref, sc_ref, srt_ref, rnk_ref, flag_ref, *, tt):
    ncol = tt // LANES
    qt_ref[...] = _dot(wq_ref[...], h2t_ref[...]).astype(BF16)
    for hp in range(2 * PEER_HEADS):
        sc_ref[hp] = _dot(sk_ref[hp % 2], qt_ref[hp * PEER_HALF:(hp + 1) * PEER_HALF, :])

    key_iota = lax.broadcasted_iota(jnp.int32, (N_KEYS, LANES), 0).astype(F32)

    sub8 = lax.broadcasted_iota(jnp.int32, (8, LANES), 0).astype(F32)

    def _tree(op, xs):
        xs = list(xs)
        while len(xs) > 1:
            xs = [op(xs[i], xs[i + 1]) if i + 1 < len(xs) else xs[i] for i in range(0, len(xs), 2)]
        return xs[0]

    def note_ties(count):
        flag_ref[...] = jnp.maximum(flag_ref[...], jnp.where(count != float(PEER_TOPK), 1.0, 0.0))

    def top16(s, exact):
        rank = jnp.full((N_KEYS, LANES), NOT_TOP, F32)
        tops = []
        for it in range(PEER_TOPK):
            m = jnp.max(s, axis=0, keepdims=True)
            if exact:
                first = jnp.min(jnp.where(s == m, key_iota, float(N_KEYS)), axis=0, keepdims=True)
                hit = key_iota == first
            else:
                hit = s == m
            s = jnp.where(hit, -jnp.inf, s)
            rank = jnp.where(hit, float(it), rank)
            tops.append(m)
        if not exact:
            note_ties(jnp.sum(jnp.where(rank < float(PEER_TOPK), 1.0, 0.0), axis=0, keepdims=True))
        return jnp.concatenate(tops, axis=0), rank

    def combine(h, ls, exact):
        s0 = srt_ref[2 * h, :, ls]
        s1 = srt_ref[2 * h + 1, :, ls]
        cands = [s0[0:1, :] + s1[0:8, :], s0[0:1, :] + s1[8:16, :]]
        poss = [sub8, sub8 + 8.0]
        for a in range(1, 8):
            c = s0[a:a + 1, :] + s1[0:8, :]
            cands.append(jnp.where(sub8 < float(16 // (a + 1)), c, -jnp.inf))
            poss.append(sub8 + float(16 * a))
        cands.append(s0[8:16, :] + s1[0:1, :])
        poss.append((sub8 + 8.0) * 16.0)
        sels = [jnp.zeros_like(c) for c in cands]
        tops = []
        for it in range(PEER_TOPK):
            m = jnp.max(_tree(jnp.maximum, cands), axis=0, keepdims=True)
            if exact:
                firsts = [jnp.where(c == m, p, 999.0) for c, p in zip(cands, poss)]
                first = jnp.min(_tree(jnp.minimum, firsts), axis=0, keepdims=True)
            for j in range(len(cands)):
                hit = (poss[j] == first) if exact else (cands[j] == m)
                cands[j] = jnp.where(hit, -jnp.inf, cands[j])
                sels[j] = jnp.where(hit, 1.0, sels[j])
            tops.append(m)
        z = _tree(jnp.add, [jnp.exp(m - tops[0]) for m in tops])
        inv_z = 1.0 / z
        n_rows = [jnp.sum(sels[0] + sels[1], axis=0, keepdims=True)]
        n_rows += [jnp.sum(sels[a + 1], axis=0, keepdims=True) for a in range(1, 8)]
        n_hi = sels[9]
        if not exact:
            note_ties(_tree(jnp.add, n_rows) + jnp.sum(n_hi, axis=0, keepdims=True))
        rank0 = rnk_ref[2 * h, :, ls]
        rank1 = rnk_ref[2 * h + 1, :, ls]
        n0 = jnp.zeros((N_KEYS, LANES), F32)
        for a in range(8):
            n0 = jnp.where(rank0 == float(a), n_rows[a], n0)
            n0 = jnp.where(rank0 == float(a + 8), n_hi[a:a + 1, :], n0)
        sc0 = sc_ref[2 * h, :, ls]
        sc1 = sc_ref[2 * h + 1, :, ls]
        a0 = jnp.where(rank0 < float(PEER_TOPK), jnp.exp(sc0 - s0[0:1, :]) * inv_z, 0.0)
        a1 = jnp.where(rank1 < float(PEER_TOPK), jnp.exp(sc1 - s1[0:1, :]), 0.0)
        a0_ref[h, :, ls] = _bf16_twice(a0)
        n0_ref[h, :, ls] = _bf16_twice(n0)
        a1_ref[h, :, ls] = pltpu.bitcast(a1.astype(BF16), jnp.uint32)
        r1_ref[h, :, ls] = pltpu.bitcast(rank1.astype(BF16), jnp.uint32)

    def stage1(hp, carry, exact):
        for col in range(ncol):
            ls = slice(col * LANES, (col + 1) * LANES)
            tops, rank = top16(sc_ref[hp, :, ls], exact)
            srt_ref[hp, :, ls] = tops
            rnk_ref[hp, :, ls] = rank
        return carry

    def stage2(h, carry):
        flag_ref[...] = jnp.zeros_like(flag_ref)
        for col in range(ncol):
            combine(h, slice(col * LANES, (col + 1) * LANES), False)

        @pl.when(jnp.max(flag_ref[...]) > 0.0)
        def _():
            for col in range(ncol):
                combine(h, slice(col * LANES, (col + 1) * LANES), True)

        return carry

    flag_ref[...] = jnp.zeros_like(flag_ref)
    lax.fori_loop(0, 2 * PEER_HEADS, functools.partial(stage1, exact=False), 0)

    @pl.when(jnp.max(flag_ref[...]) > 0.0)
    def _():
        lax.fori_loop(0, 2 * PEER_HEADS, functools.partial(stage1, exact=True), 0)

    lax.fori_loop(0, PEER_HEADS, stage2, 0)


def _route(h2t, wq_t, sk, tt=512):
    t = h2t.shape[1]
    out0 = pl.BlockSpec((PEER_HEADS, N_KEYS, tt), lambda i: (0, 0, i))
    out1 = pl.BlockSpec((PEER_HEADS, N_KEYS // 2, tt), lambda i: (0, 0, i))
    shp0 = jax.ShapeDtypeStruct((PEER_HEADS, N_KEYS, t), jnp.uint32)
    shp1 = jax.ShapeDtypeStruct((PEER_HEADS, N_KEYS // 2, t), jnp.uint32)
    return pl.pallas_call(
        functools.partial(_route_kernel, tt=tt),
        grid=(t // tt,),
        in_specs=[
            pl.BlockSpec((D_MODEL, tt), lambda i: (0, i)),
            pl.BlockSpec((2 * PEER_HEADS * PEER_HALF, D_MODEL), lambda i: (0, 0)),
            pl.BlockSpec((2, N_KEYS, PEER_HALF), lambda i: (0, 0, 0)),
        ],
        out_specs=[out0, out0, out1, out1],
        out_shape=[shp0, shp0, shp1, shp1],
        scratch_shapes=[
            pltpu.VMEM((2 * PEER_HEADS * PEER_HALF, tt), BF16),
            pltpu.VMEM((2 * PEER_HEADS, N_KEYS, tt), F32),
            pltpu.VMEM((2 * PEER_HEADS, PEER_TOPK, tt), F32),
            pltpu.VMEM((2 * PEER_HEADS, N_KEYS, tt), F32),
            pltpu.VMEM((8, LANES), F32),
        ],
        compiler_params=_cparams("arbitrary"),
        name="peer_route",
    )(h2t, wq_t, sk)


def _gelu_tanh(x):
    c = 2.0 * 0.7978845608028654
    neg_u = x * ((-c * 0.044715) * (x * x) - c)
    return x / (1.0 + jnp.exp(neg_u))


def _peer_step(h2t_ref, a0_ref, n0_ref, a1_ref, r1_ref, u_ref, vt_ref, acc_ref,
               s_new, s_cur, w_cur, w_done, i0, *, tt, n_i):
    ncol = tt // LANES
    zero = jnp.zeros((PACK_ROWS, LANES), BF16)
    rows_per_piece = MXU_PIECE_ROWS // N_KEYS
    for il in range(n_i):
        mr = slice(il // rows_per_piece * MXU_PIECE_ROWS, (il // rows_per_piece + 1) * MXU_PIECE_ROWS)
        if il % rows_per_piece == 0:
            s_new[mr, :] = _dot(u_ref[mr, :], h2t_ref[...])
        for col in range(ncol):
            if il % rows_per_piece == rows_per_piece // 2 and col == 0:
                acc_ref[mr, :] += _dot(vt_ref[mr, :], w_done[...])
            ls = slice(col * LANES, (col + 1) * LANES)
            n0 = [pltpu.bitcast(jnp.broadcast_to(n0_ref[h, pl.ds(i0, n_i), ls][il:il + 1, :], (8, LANES)), BF16)
                  for h in range(PEER_HEADS)]
            a0 = [pltpu.bitcast(jnp.broadcast_to(a0_ref[h, pl.ds(i0, n_i), ls][il:il + 1, :], (8, LANES)), BF16)
                  for h in range(PEER_HEADS)]
            for k in range(N_KEYS // PACK_ROWS):
                pr = slice(k * PACK_ROWS // 2, (k + 1) * PACK_ROWS // 2)
                gate = None
                for h in range(PEER_HEADS):
                    r1 = pltpu.bitcast(r1_ref[h, pr, ls], BF16)
                    a1 = pltpu.bitcast(a1_ref[h, pr, ls], BF16)
                    term = jnp.where(r1 < n0[h], a0[h] * a1, zero)
                    gate = term if gate is None else gate + term
                rows = slice(il * N_KEYS + k * PACK_ROWS, il * N_KEYS + (k + 1) * PACK_ROWS)
                act = _gelu_tanh(s_cur[rows, ls])
                w_cur[rows, ls] = gate * act.astype(BF16)


def _peer_kernel(h2t_ref, a0_ref, n0_ref, a1_ref, r1_ref, u_ref, vt_ref, x1_ref, mod_ref,
                 o_ref, s0_ref, s1_ref, w0_ref, w1_ref, acc_ref, *, tt, ec, ne, total):
    g = pl.program_id(0)
    n_i = ec // N_KEYS
    assert n_i == 8, "one expert chunk must cover exactly one sublane tile of half-0 keys"
    g_gate = jnp.clip(g - 1, 0, total - 1)
    c_out = jnp.clip(g - 2, 0, total - 1) % ne
    i0 = pl.multiple_of((g_gate % ne) * n_i, n_i)

    @pl.when(g == 0)
    def _():
        s1_ref[...] = jnp.zeros_like(s1_ref)
        w0_ref[...] = jnp.zeros_like(w0_ref)
        w1_ref[...] = jnp.zeros_like(w1_ref)

    @pl.when(c_out == 0)
    def _():
        acc_ref[...] = jnp.zeros_like(acc_ref)

    step = functools.partial(_peer_step, h2t_ref, a0_ref, n0_ref, a1_ref, r1_ref, u_ref, vt_ref, acc_ref,
                             tt=tt, n_i=n_i)

    @pl.when(g % 2 == 0)
    def _():
        step(s0_ref, s1_ref, w1_ref, w0_ref, i0)

    @pl.when(g % 2 == 1)
    def _():
        step(s1_ref, s0_ref, w0_ref, w1_ref, i0)

    @pl.when((g >= 2) & (c_out == ne - 1))
    def _():
        o_ref[...] = x1_ref[...] + mod_ref[0, 5:6, :] * acc_ref[...].T


def _peer(h2t, a0, n0, a1, r1, u, vt, x1, mod, seq, tt=512, ec=1024):
    t = x1.shape[0]
    nb = mod.shape[0]
    tpb = seq // tt
    ne = N_EXPERTS // ec
    total = (t // tt) * ne
    assert ec == D_MODEL, "the kernel walks U-chunk rows and out^T rows with one 128-row block index"

    def st_a(g):
        return jnp.minimum(g, total - 1)

    def st_b(g):
        return jnp.clip(g - 1, 0, total - 1)

    def st_c(g):
        return jnp.clip(g - 2, 0, total - 1)

    mod_idx = (lambda g: (st_c(g) // ne // tpb, 0, 0)) if nb > 1 else (lambda g: (0, 0, 0))
    rt0 = pl.BlockSpec((PEER_HEADS, N_KEYS, tt), lambda g: (0, 0, st_b(g) // ne))
    rt1 = pl.BlockSpec((PEER_HEADS, N_KEYS // 2, tt), lambda g: (0, 0, st_b(g) // ne))
    return pl.pallas_call(
        functools.partial(_peer_kernel, tt=tt, ec=ec, ne=ne, total=total),
        grid=(total + 2,),
        in_specs=[
            pl.BlockSpec((D_MODEL, tt), lambda g: (0, st_a(g) // ne)),
            rt0, rt0, rt1, rt1,
            pl.BlockSpec((ec, D_MODEL), lambda g: (st_a(g) % ne, 0)),
            pl.BlockSpec((D_MODEL, ec), lambda g: (0, st_c(g) % ne)),
            pl.BlockSpec((tt, D_MODEL), lambda g: (st_c(g) // ne, 0)),
            pl.BlockSpec((1, 6, D_MODEL), mod_idx),
        ],
        out_specs=pl.BlockSpec((tt, D_MODEL), lambda g: (st_c(g) // ne, 0)),
        out_shape=jax.ShapeDtypeStruct((t, D_MODEL), F32),
        scratch_shapes=[
            pltpu.VMEM((ec, tt), F32), pltpu.VMEM((ec, tt), F32),
            pltpu.VMEM((ec, tt), BF16), pltpu.VMEM((ec, tt), BF16),
            pltpu.VMEM((D_MODEL, tt), F32),
        ],
        compiler_params=_cparams("arbitrary"),
        name="peer_mix",
    )(h2t, a0, n0, a1, r1, u, vt, x1, mod)


def _final_norm_kernel(x_ref, g_ref, o_ref):
    x = x_ref[...]
    ms = jnp.mean(x * x, axis=-1, keepdims=True)
    o_ref[...] = x * lax.rsqrt(ms + EPS) * g_ref[...]


def _final_norm(x, g, tt=512):
    t = x.shape[0]
    return pl.pallas_call(
        _final_norm_kernel,
        grid=(t // tt,),
        in_specs=[pl.BlockSpec((tt, D_MODEL), lambda i: (i, 0)), pl.BlockSpec((1, D_MODEL), lambda i: (0, 0))],
        out_specs=pl.BlockSpec((tt, D_MODEL), lambda i: (i, 0)),
        out_shape=jax.ShapeDtypeStruct((t, D_MODEL), F32),
        compiler_params=_cparams("arbitrary"),
        name="final_norm",
    )(x, g)


def _rope_tables(seq):
    n_rows = seq // GRID_W
    row = jnp.repeat(jnp.arange(n_rows), GRID_W).astype(F32)
    col = jnp.tile(jnp.arange(GRID_W), n_rows).astype(F32)
    n_f = HEAD_DIM // 4
    inv = 1.0 / (ROPE_THETA ** (jnp.arange(n_f, dtype=F32) / n_f))
    ang = jnp.concatenate([row[:, None] * inv, col[:, None] * inv], axis=-1)
    cos, sin = jnp.cos(ang), jnp.sin(ang)
    cos_t = jnp.tile(jnp.concatenate([cos, cos], axis=-1), (1, N_HEADS))
    sin_t = jnp.tile(jnp.concatenate([-sin, sin], axis=-1), (1, N_HEADS))
    return cos_t, sin_t


def _pad_heads(kv):
    z = jnp.zeros_like(kv[:, :HEAD_DIM])
    h0, h1 = kv[:, :HEAD_DIM], kv[:, HEAD_DIM:]
    return jnp.concatenate([h0, z, z, h0, h1, z, z, h1], axis=-1).astype(BF16)


def _layer(x, mod, lp, consts, batch, seq, ctx_kv):
    is_ctx = ctx_kv is None
    g1 = lp["norm1_g"]
    w_in = lp["w_in"]
    qa, qw, kxa, vxa, kxw, vxw, kvf = _in_qkv(
        x, mod, g1, w_in, seq, consts["bd"], lp["q_norm_g"], lp["k_norm_g"],
        None if is_ctx else consts["rope"])
    dft = consts["dft_ctx"] if is_ctx else consts["dft_lat"]
    f = _fourier(x, mod, g1, w_in, consts["ccs"], dft, batch, seq, min(seq, 512))
    cv = _conv(x, mod, g1, w_in, lp["conv_dw"], lp["conv_b"], lp["conv_ln_g"], lp["conv_ln_b"], batch, seq)
    no_sink = jnp.full((N_HEADS,), NEG_INF, F32)
    if is_ctx:
        a = _attention(qa, kxa, vxa, no_sink, batch, seq)
        w = _attention(qw, kxw, vxw, lp["win_sink"], batch, seq)
    else:
        ka_c, va_c, kw_c, vw_c = ctx_kv
        a = _attention(qa, kxa, vxa, no_sink, batch, seq, ctx=(ka_c, va_c))
        w = _attention(qw, kxw, vxw, lp["win_sink"], batch, seq, ctx=(kw_c, vw_c), windowed=True)
    x1, h2t = _out_proj(f, cv, a, w, x, mod, lp["b_gate"], g1, lp["norm2_g"], w_in,
                        lp["w_fourier_out"], lp["w_conv_out"], lp["w_attn_out"], lp["w_win_out"],
                        lp["w_out"], seq)
    a0, n0, a1, r1 = _route(h2t, lp["w_peer_q_t"], lp["peer_subkeys"])
    x2 = _peer(h2t, a0, n0, a1, r1, lp["peer_u"], lp["peer_v_t"], x1, mod, seq)
    return x2, kvf


def kernel(x_prompt, x_sample, cache_attn_k, cache_attn_v, cache_win_k, cache_win_v, c, c_ctx, norm1_g, norm2_g, w_ada, b_ada, w_in, b_gate, w_fourier_out, conv_dw, conv_b, conv_ln_g, conv_ln_b, w_conv_out, q_norm_g, k_norm_g, w_attn_out, win_sink, w_win_out, w_out, w_peer_q, peer_subkeys, peer_u, peer_v, final_norm_g):
    b_ctx, s_ctx, _ = x_prompt.shape
    b_lat, s_lat, _ = x_sample.shape
    past = cache_attn_k.shape[2]

    cond = jnp.concatenate([c_ctx[None, :], c, jnp.zeros((16 - 1 - b_lat, D_MODEL), F32)], axis=0)
    ada = _ada_all(cond, w_ada, b_ada).reshape(DEPTH, 16, 6, D_MODEL)

    consts = {
        "bd": jnp.asarray(np.kron(np.eye(N_HEADS), np.ones((HEAD_DIM, HEAD_DIM))), BF16),
        "ccs": jnp.asarray(_channel_dft_tables()).astype(BF16),
        "dft_ctx": jnp.asarray(_dft_tables(s_ctx)).astype(BF16),
        "dft_lat": jnp.asarray(_dft_tables(s_lat)).astype(BF16),
        "rope": _rope_tables(s_lat),
    }

    xp = x_prompt.reshape(b_ctx * s_ctx, D_MODEL)
    xs = x_sample.reshape(b_lat * s_lat, D_MODEL)
    kv_out = []
    for l in range(DEPTH):
        lp = {
            "norm1_g": norm1_g[l][None, :], "norm2_g": norm2_g[l][None, :],
            "w_in": w_in[l].astype(BF16), "b_gate": b_gate[l],
            "w_fourier_out": w_fourier_out[l].astype(BF16),
            "conv_dw": conv_dw[l], "conv_b": conv_b[l][None, :],
            "conv_ln_g": conv_ln_g[l][None, :], "conv_ln_b": conv_ln_b[l][None, :],
            "w_conv_out": w_conv_out[l].astype(BF16),
            "q_norm_g": jnp.tile(q_norm_g[l], N_HEADS)[None, :],
            "k_norm_g": jnp.tile(k_norm_g[l], 2)[None, :],
            "w_attn_out": w_attn_out[l].astype(BF16), "win_sink": win_sink[l],
            "w_win_out": w_win_out[l].astype(BF16), "w_out": w_out[l].astype(BF16),
            "w_peer_q_t": w_peer_q[l].T.astype(BF16),
            "peer_subkeys": peer_subkeys[l].astype(BF16),
            "peer_u": peer_u[l].astype(BF16),
            "peer_v_t": peer_v[l].T.astype(BF16),
        }
        mod_ctx = ada[l, 0:1]
        mod_lat = ada[l, 1:1 + b_lat]
        xp, kvf = _layer(xp, mod_ctx, lp, consts, b_ctx, s_ctx, None)
        kv_out.append(kvf)
        cached = tuple(
            _pad_heads(cc[:, l].reshape(b_lat * past, ATTN_KV))
            for cc in (cache_attn_k, cache_attn_v, cache_win_k, cache_win_v))
        xs, _ = _layer(xs, mod_lat, lp, consts, b_lat, s_lat, cached)

    y_prompt = _final_norm(xp, final_norm_g[None, :]).reshape(b_ctx, s_ctx, D_MODEL)
    y_sample = _final_norm(xs, final_norm_g[None, :]).reshape(b_lat, s_lat, D_MODEL)
    outs = []
    for j in range(4):
        per_layer = [kvf[:, j * 128:(j + 1) * 128].reshape(b_ctx, s_ctx, 2, HEAD_DIM) for kvf in kv_out]
        outs.append(jnp.stack(per_layer, axis=1))
    return (y_prompt, y_sample, outs[0], outs[1], outs[2], outs[3])
```

```python
import functools

import numpy as np
import jax
import jax.numpy as jnp
from jax import lax
from jax.experimental import pallas as pl
from jax.experimental.pallas import tpu as pltpu

F32 = jnp.float32
BF16 = jnp.bfloat16

D_MODEL = 1024
DEPTH = 4
GRID_W = 64
HEAD_DIM = 64
FNET_GROUP_DIM = 64
FNET_WIDTH = 512
CONV_WIDTH = 512
CONV_K = 31
N_HEADS = 8
ATTN_Q = 512
ATTN_KV = 128
WINDOW = 128
ROPE_THETA = 10000.0
N_BRANCH = 4
COL_F, COL_C, COL_QKV, COL_GL = 0, 512, 1536, 3072
IN_COLS = 7168
N_KEYS = 128
N_EXPERTS = N_KEYS * N_KEYS
PEER_HEADS = 8
PEER_HALF = 128
PEER_TOPK = 16
EPS = 1e-6
NEG_INF = -1e30
LANES = 128
PACK_ROWS = 16
MXU_PIECE_ROWS = 512
FOURIER_ROWS = 1024
ATTN_Q_ROWS = 512

VMEM_LIMIT = 56 * 1024 * 1024


def _cparams(*sem):
    return pltpu.CompilerParams(dimension_semantics=sem, vmem_limit_bytes=VMEM_LIMIT)


def _dot(a, b):
    return jnp.dot(a, b, preferred_element_type=F32)


def _dot_nt(a, b):
    return lax.dot_general(a, b, (((1,), (1,)), ((), ())), preferred_element_type=F32)


def _normmod(x, g, sc, sh):
    ms = jnp.mean(x * x, axis=-1, keepdims=True)
    y = x * lax.rsqrt(ms + EPS)
    return (y * g) * (1.0 + sc) + sh


def _ada_kernel(c_ref, w_ref, b_ref, o_ref):
    c = c_ref[...]
    a = (c * jax.nn.sigmoid(c)).astype(BF16)
    o_ref[0] = _dot(a, w_ref[0].astype(BF16)) + b_ref[0]


def _ada_all(cond16, w_ada, b_ada):
    tn = 1536
    n = w_ada.shape[2]
    return pl.pallas_call(
        _ada_kernel,
        grid=(DEPTH, n // tn),
        in_specs=[
            pl.BlockSpec((16, D_MODEL), lambda l, j: (0, 0)),
            pl.BlockSpec((1, D_MODEL, tn), lambda l, j: (l, 0, j)),
            pl.BlockSpec((1, 1, tn), lambda l, j: (l, 0, j)),
        ],
        out_specs=pl.BlockSpec((1, 16, tn), lambda l, j: (l, 0, j)),
        out_shape=jax.ShapeDtypeStruct((DEPTH, 16, n), F32),
        compiler_params=_cparams("arbitrary", "arbitrary"),
        name="ada",
    )(cond16, w_ada, b_ada.reshape(DEPTH, 1, n))


def _seg_mean_sq(x, bd):
    sq = x * x
    hi = sq.astype(BF16)
    lo = (sq - hi.astype(F32)).astype(BF16)
    return (_dot(hi, bd) + _dot(lo, bd)) * (1.0 / HEAD_DIM)


def _rope(x, cos, sin_signed):
    w = x.shape[1]
    fwd = pltpu.roll(x, HEAD_DIM // 2, axis=1)
    bwd = pltpu.roll(x, w - HEAD_DIM // 2, axis=1)
    lane = lax.broadcasted_iota(jnp.int32, x.shape, 1)
    first = (lane % HEAD_DIM) < (HEAD_DIM // 2)
    partner = jnp.where(first, bwd, fwd)
    return x * cos + partner * sin_signed


def _store_head_padded(o_ref, k):
    kr = pltpu.roll(k, HEAD_DIM, axis=1)
    lane = lax.broadcasted_iota(jnp.int32, k.shape, 1)
    lo = lane < HEAD_DIM
    zero = jnp.zeros_like(k)
    o_ref[:, 0:128] = jnp.where(lo, k, zero).astype(BF16)
    o_ref[:, 128:256] = jnp.where(lo, zero, kr).astype(BF16)
    o_ref[:, 256:384] = jnp.where(lo, kr, zero).astype(BF16)
    o_ref[:, 384:512] = jnp.where(lo, zero, k).astype(BF16)


def _in_qkv_kernel(*refs, use_rope):
    if use_rope:
        (x_ref, mod_ref, g_ref, w_ref, bd_ref, qg_ref, kg_ref, cos_ref, sin_ref,
         qa_ref, qw_ref, kxa_ref, vxa_ref, kxw_ref, vxw_ref, kvf_ref) = refs
    else:
        (x_ref, mod_ref, g_ref, w_ref, bd_ref, qg_ref, kg_ref,
         qa_ref, qw_ref, kxa_ref, vxa_ref, kxw_ref, vxw_ref, kvf_ref) = refs
    h = _normmod(x_ref[...], g_ref[...], mod_ref[0, 1:2, :], mod_ref[0, 0:1, :]).astype(BF16)
    acc = _dot(h, w_ref[...])
    aq = acc[:, 0:512]
    ak = acc[:, 512:640]
    av = acc[:, 640:768]
    wq = acc[:, 768:1280]
    wk = acc[:, 1280:1408]
    wv = acc[:, 1408:1536]
    bd = bd_ref[...]
    aq = aq * lax.rsqrt(_seg_mean_sq(aq, bd) + EPS) * qg_ref[...]
    ak = ak * lax.rsqrt(_seg_mean_sq(ak, bd[0:128, 0:128]) + EPS) * kg_ref[...]
    if use_rope:
        cos = cos_ref[...]
        sin = sin_ref[...]
        aq = _rope(aq, cos, sin)
        wq = _rope(wq, cos, sin)
        ak = _rope(ak, cos[:, 0:128], sin[:, 0:128])
        wk = _rope(wk, cos[:, 0:128], sin[:, 0:128])
    scale = HEAD_DIM ** -0.5
    qa_ref[...] = (aq * scale).astype(BF16)
    qw_ref[...] = (wq * scale).astype(BF16)
    _store_head_padded(kxa_ref, ak)
    _store_head_padded(vxa_ref, av)
    _store_head_padded(kxw_ref, wk)
    _store_head_padded(vxw_ref, wv)
    kvf_ref[:, 0:128] = ak
    kvf_ref[:, 128:256] = av
    kvf_ref[:, 256:384] = wk
    kvf_ref[:, 384:512] = wv


def _in_qkv(x, mod, g, w_in, seq, bd, qg8, kg2, rope_tabs, tt=512):
    t = x.shape[0]
    nb = mod.shape[0]
    tpb = seq // tt
    mod_idx = (lambda i: (i // tpb, 0, 0)) if nb > 1 else (lambda i: (0, 0, 0))
    use_rope = rope_tabs is not None
    ncol = 1536
    in_specs = [
        pl.BlockSpec((tt, D_MODEL), lambda i: (i, 0)),
        pl.BlockSpec((1, 6, D_MODEL), mod_idx),
        pl.BlockSpec((1, D_MODEL), lambda i: (0, 0)),
        pl.BlockSpec((D_MODEL, ncol), lambda i: (0, COL_QKV // ncol)),
        pl.BlockSpec((512, 512), lambda i: (0, 0)),
        pl.BlockSpec((1, 512), lambda i: (0, 0)),
        pl.BlockSpec((1, 128), lambda i: (0, 0)),
    ]
    args = [x, mod, g, w_in, bd, qg8, kg2]
    if use_rope:
        in_specs += [pl.BlockSpec((tt, 512), lambda i: (i % tpb, 0))] * 2
        args += list(rope_tabs)
    tok = pl.BlockSpec((tt, 512), lambda i: (i, 0))
    return pl.pallas_call(
        functools.partial(_in_qkv_kernel, use_rope=use_rope),
        grid=(t // tt,),
        in_specs=in_specs,
        out_specs=[tok] * 7,
        out_shape=[jax.ShapeDtypeStruct((t, 512), BF16)] * 6 + [jax.ShapeDtypeStruct((t, 512), F32)],
        compiler_params=_cparams("arbitrary"),
        name="in_qkv",
    )(*args)


def _attn_kernel(*refs, tq, seq, n_ctx, windowed):
    if n_ctx:
        sink_ref, q_ref, kx_ref, vx_ref, kc_ref, vc_ref, o_ref = refs
    else:
        sink_ref, q_ref, kx_ref, vx_ref, o_ref = refs
    i = pl.program_id(1)
    if windowed:
        span = tq + 2 * WINDOW
        start = jnp.clip(i * tq - WINDOW, 0, seq - span)
        start = pl.multiple_of(start, WINDOW)
        qpos = i * tq + lax.broadcasted_iota(jnp.int32, (tq, span), 0)
        kpos = start + lax.broadcasted_iota(jnp.int32, (tq, span), 1)
        valid = jnp.abs(kpos - qpos) <= WINDOW
    for jb in range(4):
        q2 = q_ref[:, jb * 128:(jb + 1) * 128]
        acc = None
        for e in range(2):
            head = jb * 2 + e
            v_idx = (jb // 2) * 2 + e
            cols = slice(v_idx * 128, (v_idx + 1) * 128)
            if windowed:
                k_loc = kx_ref[pl.ds(start, span), cols]
                v_loc = vx_ref[pl.ds(start, span), cols]
            else:
                k_loc = kx_ref[:, cols]
                v_loc = vx_ref[:, cols]
            s = _dot_nt(q2, k_loc)
            if windowed:
                s = jnp.where(valid, s, NEG_INF)
            sink = sink_ref[head]
            m = jnp.maximum(jnp.max(s, axis=-1, keepdims=True), sink)
            if n_ctx:
                s_c = _dot_nt(q2, kc_ref[:, cols])
                m = jnp.maximum(m, jnp.max(s_c, axis=-1, keepdims=True))
            p = jnp.exp(s - m)
            l = jnp.sum(p, axis=-1, keepdims=True) + jnp.exp(sink - m)
            o = _dot(p.astype(BF16), v_loc)
            if n_ctx:
                p_c = jnp.exp(s_c - m)
                l = l + jnp.sum(p_c, axis=-1, keepdims=True)
                o = o + _dot(p_c.astype(BF16), vc_ref[:, cols])
            o = o / l
            acc = o if acc is None else acc + o
        o_ref[:, jb * 128:(jb + 1) * 128] = acc.astype(o_ref.dtype)


def _attention(q, kx, vx, sink, batch, seq, ctx=None, windowed=False):
    tq = min(seq, ATTN_Q_ROWS)
    assert not windowed or seq >= tq + 2 * WINDOW, "the banded key span must fit inside one sequence"
    n_ctx = 0 if ctx is None else ctx[0].shape[0] // batch
    nq = seq // tq
    in_specs = [
        pl.BlockSpec(memory_space=pltpu.SMEM),
        pl.BlockSpec((tq, 512), lambda b, i: (b * nq + i, 0)),
        pl.BlockSpec((seq, 512), lambda b, i: (b, 0)),
        pl.BlockSpec((seq, 512), lambda b, i: (b, 0)),
    ]
    args = [sink, q, kx, vx]
    if n_ctx:
        in_specs += [pl.BlockSpec((n_ctx, 512), lambda b, i: (b, 0))] * 2
        args += list(ctx)
    return pl.pallas_call(
        functools.partial(_attn_kernel, tq=tq, seq=seq, n_ctx=n_ctx, windowed=windowed),
        grid=(batch, nq),
        in_specs=in_specs,
        out_specs=pl.BlockSpec((tq, 512), lambda b, i: (b * nq + i, 0)),
        out_shape=jax.ShapeDtypeStruct((batch * seq, 512), BF16),
        compiler_params=_cparams("arbitrary", "arbitrary"),
        name="attn_win" if windowed else "attn",
    )(*args)


def _fourier_kernel(x_ref, mod_ref, g_ref, w_ref, ccs_ref, dft_ref, o_ref, z_ref, *, seq, scale):
    @pl.when(pl.program_id(1) == 0)
    def _():
        rows = min(seq, PROJ_ROWS)

        def body(r, carry):
            base = pl.multiple_of(r * rows, rows)
            h = _normmod(x_ref[pl.ds(base, rows), :], g_ref[...], mod_ref[0, 1:2, :], mod_ref[0, 0:1, :])
            f = _dot(h.astype(BF16), w_ref[...]).astype(BF16)
            z = _dot(f, ccs_ref[...])
            z_ref[pl.ds(base, rows), :] = z[:, 0:FNET_WIDTH].astype(BF16)
            z_ref[pl.ds(seq + base, rows), :] = z[:, FNET_WIDTH:].astype(BF16)
            return carry

        lax.fori_loop(0, seq // rows, body, 0)

    o_ref[...] = (_dot(dft_ref[...], z_ref[...]) * scale).astype(o_ref.dtype)


def _fourier(x, mod, g, w_in, ccs, dft, batch, seq, tr):
    nr = seq // tr
    nb = mod.shape[0]
    mod_idx = (lambda b, i: (b, 0, 0)) if nb > 1 else (lambda b, i: (0, 0, 0))
    return pl.pallas_call(
        functools.partial(_fourier_kernel, seq=seq, scale=float((seq * FNET_GROUP_DIM) ** -0.5)),
        grid=(batch, nr),
        in_specs=[
            pl.BlockSpec((seq, D_MODEL), lambda b, i: (b, 0)),
            pl.BlockSpec((1, 6, D_MODEL), mod_idx),
            pl.BlockSpec((1, D_MODEL), lambda b, i: (0, 0)),
            pl.BlockSpec((D_MODEL, FNET_WIDTH), lambda b, i: (0, COL_F // FNET_WIDTH)),
            pl.BlockSpec((FNET_WIDTH, 2 * FNET_WIDTH), lambda b, i: (0, 0)),
            pl.BlockSpec((tr, 2 * seq), lambda b, i: (i, 0)),
        ],
        out_specs=pl.BlockSpec((tr, FNET_WIDTH), lambda b, i: (b * nr + i, 0)),
        out_shape=jax.ShapeDtypeStruct((batch * seq, FNET_WIDTH), BF16),
        scratch_shapes=[pltpu.VMEM((2 * seq, FNET_WIDTH), BF16)],
        compiler_params=_cparams("arbitrary", "arbitrary"),
        name="fourier",
    )(x, mod, g, w_in, ccs, dft)


def _dft_tables(seq):
    k = np.arange(seq, dtype=np.int64)
    ang = 2.0 * np.pi * ((k[:, None] * k[None, :]) % seq).astype(np.float64) / seq
    return np.concatenate([np.cos(ang), -np.sin(ang)], axis=1).astype(np.float32)


def _channel_dft_tables():
    g = FNET_GROUP_DIM
    k = np.arange(g, dtype=np.int64)
    ang = 2.0 * np.pi * ((k[:, None] * k[None, :]) % g).astype(np.float64) / g
    eye = np.eye(FNET_WIDTH // g)
    return np.concatenate([np.kron(eye, np.cos(ang)), np.kron(eye, np.sin(ang))], axis=1).astype(np.float32)


CONV_PAD = 16
CONV_ROWS = 128
PROJ_ROWS = 512


def _conv_kernel(x_ref, mod_ref, g_ref, wa_ref, wg_ref, dw_ref, b_ref, lg_ref, lb_ref, o_ref, hp_ref, *, seq):
    zeros = jnp.zeros((CONV_PAD, CONV_WIDTH), F32)
    hp_ref[0:CONV_PAD, :] = zeros
    hp_ref[seq + CONV_PAD:seq + 2 * CONV_PAD, :] = zeros

    rows = min(seq, PROJ_ROWS)

    def glu_body(r, carry):
        base = pl.multiple_of(r * rows, rows)
        h = _normmod(x_ref[pl.ds(base, rows), :], g_ref[...], mod_ref[0, 1:2, :], mod_ref[0, 0:1, :])
        h = h.astype(BF16)
        hp_ref[pl.ds(base + CONV_PAD, rows), :] = _dot(h, wa_ref[...]) * jax.nn.sigmoid(_dot(h, wg_ref[...]))
        return carry

    lax.fori_loop(0, seq // rows, glu_body, 0)

    blk_rows = CONV_ROWS + 2 * CONV_PAD

    def conv_body(r, carry):
        base = pl.multiple_of(r * CONV_ROWS, CONV_ROWS)
        outs = []
        for cch in range(CONV_WIDTH // LANES):
            ls = slice(cch * LANES, (cch + 1) * LANES)
            blk = hp_ref[pl.ds(base, blk_rows), ls]
            acc = jnp.zeros((CONV_ROWS, LANES), F32)
            for s in range(8):
                blk_s = blk if s == 0 else pltpu.roll(blk, blk_rows - s, axis=0)
                for m in range(4):
                    off = 8 * m + s
                    k = off - 1
                    if 0 <= k < CONV_K:
                        acc = acc + blk_s[8 * m:8 * m + CONV_ROWS, :] * dw_ref[k:k + 1, ls]
            outs.append(acc)
        y = jnp.concatenate(outs, axis=1) + b_ref[...]
        mu = jnp.mean(y, axis=-1, keepdims=True)
        yc = y - mu
        var = jnp.mean(yc * yc, axis=-1, keepdims=True)
        hn = yc * lax.rsqrt(var + EPS) * lg_ref[...] + lb_ref[...]
        o_ref[pl.ds(base, CONV_ROWS), :] = (hn * jax.nn.sigmoid(hn)).astype(o_ref.dtype)
        return carry

    lax.fori_loop(0, seq // CONV_ROWS, conv_body, 0)


def _conv(x, mod, g, w_in, dw, b, lg, lb, batch, seq):
    vec = pl.BlockSpec((1, CONV_WIDTH), lambda bi: (0, 0))
    nb = mod.shape[0]
    mod_idx = (lambda bi: (bi, 0, 0)) if nb > 1 else (lambda bi: (0, 0, 0))
    return pl.pallas_call(
        functools.partial(_conv_kernel, seq=seq),
        grid=(batch,),
        in_specs=[
            pl.BlockSpec((seq, D_MODEL), lambda bi: (bi, 0)),
            pl.BlockSpec((1, 6, D_MODEL), mod_idx),
            pl.BlockSpec((1, D_MODEL), lambda bi: (0, 0)),
            pl.BlockSpec((D_MODEL, CONV_WIDTH), lambda bi: (0, COL_C // CONV_WIDTH)),
            pl.BlockSpec((D_MODEL, CONV_WIDTH), lambda bi: (0, COL_C // CONV_WIDTH + 1)),
            pl.BlockSpec((CONV_K, CONV_WIDTH), lambda bi: (0, 0)),
            vec, vec, vec,
        ],
        out_specs=pl.BlockSpec((seq, CONV_WIDTH), lambda bi: (bi, 0)),
        out_shape=jax.ShapeDtypeStruct((batch * seq, CONV_WIDTH), BF16),
        scratch_shapes=[pltpu.VMEM((seq + 2 * CONV_PAD, CONV_WIDTH), F32)],
        compiler_params=_cparams("arbitrary"),
        name="conv",
    )(x, mod, g, w_in, w_in, dw, b, lg, lb)


def _out_kernel(f_ref, c_ref, a_ref, w_ref, x_ref, mod_ref, bg_ref, g1_ref, g2_ref,
                wg0_ref, wg1_ref, wg2_ref, wg3_ref, wf_ref, wc_ref, wa_ref, ww_ref, wo_ref, x1_ref, h2t_ref):
    x = x_ref[...]
    h1 = _normmod(x, g1_ref[...], mod_ref[0, 1:2, :], mod_ref[0, 0:1, :]).astype(BF16)
    merged = None
    branches = ((f_ref, wf_ref, wg0_ref), (c_ref, wc_ref, wg1_ref), (a_ref, wa_ref, wg2_ref), (w_ref, ww_ref, wg3_ref))
    for i, (br, wt, wg) in enumerate(branches):
        y = _dot(br[...], wt[...])
        gate = jax.nn.sigmoid(_dot(h1, wg[...]) + bg_ref[i:i + 1, :])
        merged = gate * y if merged is None else merged + gate * y
    mix = _dot(merged.astype(BF16), wo_ref[...])
    x1 = x + mod_ref[0, 2:3, :] * mix
    x1_ref[...] = x1
    h2 = _normmod(x1, g2_ref[...], mod_ref[0, 4:5, :], mod_ref[0, 3:4, :])
    h2t_ref[...] = h2.T.astype(BF16)


def _out_proj(f, c, a, w, x, mod, bg, g1, g2, w_in, wf, wc, wa, ww, wo, seq, tt=256):
    t = x.shape[0]
    nb = mod.shape[0]
    tpb = seq // tt
    mod_idx = (lambda i: (i // tpb, 0, 0)) if nb > 1 else (lambda i: (0, 0, 0))
    br = pl.BlockSpec((tt, 512), lambda i: (i, 0))
    wbr = pl.BlockSpec((512, D_MODEL), lambda i: (0, 0))
    vec = pl.BlockSpec((1, D_MODEL), lambda i: (0, 0))
    gl0 = COL_GL // D_MODEL
    wgl = [pl.BlockSpec((D_MODEL, D_MODEL), functools.partial(lambda i, j: (0, j), j=gl0 + k)) for k in range(N_BRANCH)]
    return pl.pallas_call(
        _out_kernel,
        grid=(t // tt,),
        in_specs=[
            br, br, br, br,
            pl.BlockSpec((tt, D_MODEL), lambda i: (i, 0)),
            pl.BlockSpec((1, 6, D_MODEL), mod_idx),
            pl.BlockSpec((N_BRANCH, D_MODEL), lambda i: (0, 0)),
            vec, vec,
            *wgl,
            wbr, wbr, wbr, wbr,
            pl.BlockSpec((D_MODEL, D_MODEL), lambda i: (0, 0)),
        ],
        out_specs=[
            pl.BlockSpec((tt, D_MODEL), lambda i: (i, 0)),
            pl.BlockSpec((D_MODEL, tt), lambda i: (0, i)),
        ],
        out_shape=[jax.ShapeDtypeStruct((t, D_MODEL), F32), jax.ShapeDtypeStruct((D_MODEL, t), BF16)],
        compiler_params=_cparams("arbitrary"),
        name="out_proj",
    )(f, c, a, w, x, mod, bg, g1, g2, w_in, w_in, w_in, w_in, wf, wc, wa, ww, wo)


NOT_TOP = 99.0


def _bf16_twice(x):
    bits = pltpu.bitcast(x.astype(BF16).astype(F32), jnp.uint32)
    return bits | (bits >> 16)


def _route_kernel(h2t_ref, wq_ref, sk_ref, a0_ref, n0_ref, a1_ref, r1_ref,
                  qt_ref, sc_ref, srt_ref, rnk_ref, flag_ref, *, tt):
    ncol = tt // LANES
    qt_ref[...] = _dot(wq_ref[...], h2t_ref[...]).astype(BF16)
    for hp in range(2 * PEER_HEADS):
        sc_ref[hp] = _dot(sk_ref[hp % 2], qt_ref[hp * PEER_HALF:(hp + 1) * PEER_HALF, :])

    key_iota = lax.broadcasted_iota(jnp.int32, (N_KEYS, LANES), 0).astype(F32)

    sub8 = lax.broadcasted_iota(jnp.int32, (8, LANES), 0).astype(F32)

    def _tree(op, xs):
        xs = list(xs)
        while len(xs) > 1:
            xs = [op(xs[i], xs[i + 1]) if i + 1 < len(xs) else xs[i] for i in range(0, len(xs), 2)]
        return xs[0]

    def note_ties(count):
        flag_ref[...] = jnp.maximum(flag_ref[...], jnp.where(count != float(PEER_TOPK), 1.0, 0.0))

    def top16(s, exact):
        rank = jnp.full((N_KEYS, LANES), NOT_TOP, F32)
        tops = []
        for it in range(PEER_TOPK):
            m = jnp.max(s, axis=0, keepdims=True)
            if exact:
                first = jnp.min(jnp.where(s == m, key_iota, float(N_KEYS)), axis=0, keepdims=True)
                hit = key_iota == first
            else:
                hit = s == m
            s = jnp.where(hit, -jnp.inf, s)
            rank = jnp.where(hit, float(it), rank)
            tops.append(m)
        if not exact:
            note_ties(jnp.sum(jnp.where(rank < float(PEER_TOPK), 1.0, 0.0), axis=0, keepdims=True))
        return jnp.concatenate(tops, axis=0), rank

    def combine(h, ls, exact):
        s0 = srt_ref[2 * h, :, ls]
        s1 = srt_ref[2 * h + 1, :, ls]
        cands = [s0[0:1, :] + s1[0:8, :], s0[0:1, :] + s1[8:16, :]]
        poss = [sub8, sub8 + 8.0]
        for a in range(1, 8):
            c = s0[a:a + 1, :] + s1[0:8, :]
            cands.append(jnp.where(sub8 < float(16 // (a + 1)), c, -jnp.inf))
            poss.append(sub8 + float(16 * a))
        cands.append(s0[8:16, :] + s1[0:1, :])
        poss.append((sub8 + 8.0) * 16.0)
        sels = [jnp.zeros_like(c) for c in cands]
        tops = []
        for it in range(PEER_TOPK):
            m = jnp.max(_tree(jnp.maximum, cands), axis=0, keepdims=True)
            if exact:
                firsts = [jnp.where(c == m, p, 999.0) for c, p in zip(cands, poss)]
                first = jnp.min(_tree(jnp.minimum, firsts), axis=0, keepdims=True)
            for j in range(len(cands)):
                hit = (poss[j] == first) if exact else (cands[j] == m)
                cands[j] = jnp.where(hit, -jnp.inf, cands[j])
                sels[j] = jnp.where(hit, 1.0, sels[j])
            tops.append(m)
        z = _tree(jnp.add, [jnp.exp(m - tops[0]) for m in tops])
        inv_z = 1.0 / z
        n_rows = [jnp.sum(sels[0] + sels[1], axis=0, keepdims=True)]
        n_rows += [jnp.sum(sels[a + 1], axis=0, keepdims=True) for a in range(1, 8)]
        n_hi = sels[9]
        if not exact:
            note_ties(_tree(jnp.add, n_rows) + jnp.sum(n_hi, axis=0, keepdims=True))
        rank0 = rnk_ref[2 * h, :, ls]
        rank1 = rnk_ref[2 * h + 1, :, ls]
        n0 = jnp.zeros((N_KEYS, LANES), F32)
        for a in range(8):
            n0 = jnp.where(rank0 == float(a), n_rows[a], n0)
            n0 = jnp.where(rank0 == float(a + 8), n_hi[a:a + 1, :], n0)
        sc0 = sc_ref[2 * h, :, ls]
        sc1 = sc_ref[2 * h + 1, :, ls]
        a0 = jnp.where(rank0 < float(PEER_TOPK), jnp.exp(sc0 - s0[0:1, :]) * inv_z, 0.0)
        a1 = jnp.where(rank1 < float(PEER_TOPK), jnp.exp(sc1 - s1[0:1, :]), 0.0)
        a0_ref[h, :, ls] = _bf16_twice(a0)
        n0_ref[h, :, ls] = _bf16_twice(n0)
        a1_ref[h, :, ls] = pltpu.bitcast(a1.astype(BF16), jnp.uint32)
        r1_ref[h, :, ls] = pltpu.bitcast(rank1.astype(BF16), jnp.uint32)

    def stage1(hp, carry, exact):
        for col in range(ncol):
            ls = slice(col * LANES, (col + 1) * LANES)
            tops, rank = top16(sc_ref[hp, :, ls], exact)
            srt_ref[hp, :, ls] = tops
            rnk_ref[hp, :, ls] = rank
        return carry

    def stage2(h, carry):
        flag_ref[...] = jnp.zeros_like(flag_ref)
        for col in range(ncol):
            combine(h, slice(col * LANES, (col + 1) * LANES), False)

        @pl.when(jnp.max(flag_ref[...]) > 0.0)
        def _():
            for col in range(ncol):
                combine(h, slice(col * LANES, (col + 1) * LANES), True)

        return carry

    flag_ref[...] = jnp.zeros_like(flag_ref)
    lax.fori_loop(0, 2 * PEER_HEADS, functools.partial(stage1, exact=False), 0)

    @pl.when(jnp.max(flag_ref[...]) > 0.0)
    def _():
        lax.fori_loop(0, 2 * PEER_HEADS, functools.partial(stage1, exact=True), 0)

    lax.fori_loop(0, PEER_HEADS, stage2, 0)


def _route(h2t, wq_t, sk, tt=512):
    t = h2t.shape[1]
    out0 = pl.BlockSpec((PEER_HEADS, N_KEYS, tt), lambda i: (0, 0, i))
    out1 = pl.BlockSpec((PEER_HEADS, N_KEYS // 2, tt), lambda i: (0, 0, i))
    shp0 = jax.ShapeDtypeStruct((PEER_HEADS, N_KEYS, t), jnp.uint32)
    shp1 = jax.ShapeDtypeStruct((PEER_HEADS, N_KEYS // 2, t), jnp.uint32)
    return pl.pallas_call(
        functools.partial(_route_kernel, tt=tt),
        grid=(t // tt,),
        in_specs=[
            pl.BlockSpec((D_MODEL, tt), lambda i: (0, i)),
            pl.BlockSpec((2 * PEER_HEADS * PEER_HALF, D_MODEL), lambda i: (0, 0)),
            pl.BlockSpec((2, N_KEYS, PEER_HALF), lambda i: (0, 0, 0)),
        ],
        out_specs=[out0, out0, out1, out1],
        out_shape=[shp0, shp0, shp1, shp1],
        scratch_shapes=[
            pltpu.VMEM((2 * PEER_HEADS * PEER_HALF, tt), BF16),
            pltpu.VMEM((2 * PEER_HEADS, N_KEYS, tt), F32),
            pltpu.VMEM((2 * PEER_HEADS, PEER_TOPK, tt), F32),
            pltpu.VMEM((2 * PEER_HEADS, N_KEYS, tt), F32),
            pltpu.VMEM((8, LANES), F32),
        ],
        compiler_params=_cparams("arbitrary"),
        name="peer_route",
    )(h2t, wq_t, sk)


def _gelu_tanh(x):
    c = 2.0 * 0.7978845608028654
    neg_u = x * ((-c * 0.044715) * (x * x) - c)
    return x / (1.0 + jnp.exp(neg_u))


def _peer_step(h2t_ref, a0_ref, n0_ref, a1_ref, r1_ref, u_ref, vt_ref, acc_ref,
               s_new, s_cur, w_cur, w_done, i0, *, tt, n_i):
    ncol = tt // LANES
    zero = jnp.zeros((PACK_ROWS, LANES), BF16)
    rows_per_piece = MXU_PIECE_ROWS // N_KEYS
    for il in range(n_i):
        mr = slice(il // rows_per_piece * MXU_PIECE_ROWS, (il // rows_per_piece + 1) * MXU_PIECE_ROWS)
        if il % rows_per_piece == 0:
            s_new[mr, :] = _dot(u_ref[mr, :], h2t_ref[...])
        for col in range(ncol):
            if il % rows_per_piece == rows_per_piece // 2 and col == 0:
                acc_ref[mr, :] += _dot(vt_ref[mr, :], w_done[...])
            ls = slice(col * LANES, (col + 1) * LANES)
            n0 = [pltpu.bitcast(jnp.broadcast_to(n0_ref[h, pl.ds(i0, n_i), ls][il:il + 1, :], (8, LANES)), BF16)
                  for h in range(PEER_HEADS)]
            a0 = [pltpu.bitcast(jnp.broadcast_to(a0_ref[h, pl.ds(i0, n_i), ls][il:il + 1, :], (8, LANES)), BF16)
                  for h in range(PEER_HEADS)]
            for k in range(N_KEYS // PACK_ROWS):
                pr = slice(k * PACK_ROWS // 2, (k + 1) * PACK_ROWS // 2)
                gate = None
                for h in range(PEER_HEADS):
                    r1 = pltpu.bitcast(r1_ref[h, pr, ls], BF16)
                    a1 = pltpu.bitcast(a1_ref[h, pr, ls], BF16)
                    term = jnp.where(r1 < n0[h], a0[h] * a1, zero)
                    gate = term if gate is None else gate + term
                rows = slice(il * N_KEYS + k * PACK_ROWS, il * N_KEYS + (k + 1) * PACK_ROWS)
                act = _gelu_tanh(s_cur[rows, ls])
                w_cur[rows, ls] = gate * act.astype(BF16)


def _peer_kernel(h2t_ref, a0_ref, n0_ref, a1_ref, r1_ref, u_ref, vt_ref, x1_ref, mod_ref, fg_ref,
                 o_ref, s0_ref, s1_ref, w0_ref, w1_ref, acc_ref, *, tt, ec, ne, total, final):
    g = pl.program_id(0)
    n_i = ec // N_KEYS
    assert n_i == 8, "one expert chunk must cover exactly one sublane tile of half-0 keys"
    g_gate = jnp.clip(g - 1, 0, total - 1)
    c_out = jnp.clip(g - 2, 0, total - 1) % ne
    i0 = pl.multiple_of((g_gate % ne) * n_i, n_i)

    @pl.when(g == 0)
    def _():
        s1_ref[...] = jnp.zeros_like(s1_ref)
        w0_ref[...] = jnp.zeros_like(w0_ref)
        w1_ref[...] = jnp.zeros_like(w1_ref)

    @pl.when(c_out == 0)
    def _():
        acc_ref[...] = jnp.zeros_like(acc_ref)

    step = functools.partial(_peer_step, h2t_ref, a0_ref, n0_ref, a1_ref, r1_ref, u_ref, vt_ref, acc_ref,
                             tt=tt, n_i=n_i)

    @pl.when(g % 2 == 0)
    def _():
        step(s0_ref, s1_ref, w1_ref, w0_ref, i0)

    @pl.when(g % 2 == 1)
    def _():
        step(s1_ref, s0_ref, w0_ref, w1_ref, i0)

    @pl.when((g >= 2) & (c_out == ne - 1))
    def _():
        y = x1_ref[...] + mod_ref[0, 5:6, :] * acc_ref[...].T
        if final:
            y = y * lax.rsqrt(jnp.mean(y * y, axis=-1, keepdims=True) + EPS) * fg_ref[...]
        o_ref[...] = y


def _peer(h2t, a0, n0, a1, r1, u, vt, x1, mod, final_g, final, seq, tt=512, ec=1024):
    t = x1.shape[0]
    nb = mod.shape[0]
    tpb = seq // tt
    ne = N_EXPERTS // ec
    total = (t // tt) * ne
    assert ec == D_MODEL, "the kernel walks U-chunk rows and out^T rows with one 128-row block index"

    def st_a(g):
        return jnp.minimum(g, total - 1)

    def st_b(g):
        return jnp.clip(g - 1, 0, total - 1)

    def st_c(g):
        return jnp.clip(g - 2, 0, total - 1)

    mod_idx = (lambda g: (st_c(g) // ne // tpb, 0, 0)) if nb > 1 else (lambda g: (0, 0, 0))
    rt0 = pl.BlockSpec((PEER_HEADS, N_KEYS, tt), lambda g: (0, 0, st_b(g) // ne))
    rt1 = pl.BlockSpec((PEER_HEADS, N_KEYS // 2, tt), lambda g: (0, 0, st_b(g) // ne))
    return pl.pallas_call(
        functools.partial(_peer_kernel, tt=tt, ec=ec, ne=ne, total=total, final=final),
        grid=(total + 2,),
        in_specs=[
            pl.BlockSpec((D_MODEL, tt), lambda g: (0, st_a(g) // ne)),
            rt0, rt0, rt1, rt1,
            pl.BlockSpec((ec, D_MODEL), lambda g: (st_a(g) % ne, 0)),
            pl.BlockSpec((D_MODEL, ec), lambda g: (0, st_c(g) % ne)),
            pl.BlockSpec((tt, D_MODEL), lambda g: (st_c(g) // ne, 0)),
            pl.BlockSpec((1, 6, D_MODEL), mod_idx),
            pl.BlockSpec((1, D_MODEL), lambda g: (0, 0)),
        ],
        out_specs=pl.BlockSpec((tt, D_MODEL), lambda g: (st_c(g) // ne, 0)),
        out_shape=jax.ShapeDtypeStruct((t, D_MODEL), F32),
        scratch_shapes=[
            pltpu.VMEM((ec, tt), F32), pltpu.VMEM((ec, tt), F32),
            pltpu.VMEM((ec, tt), BF16), pltpu.VMEM((ec, tt), BF16),
            pltpu.VMEM((D_MODEL, tt), F32),
        ],
        compiler_params=_cparams("arbitrary"),
        name="peer_mix",
    )(h2t, a0, n0, a1, r1, u, vt, x1, mod, final_g)


def _rope_tables(seq):
    n_rows = seq // GRID_W
    row = jnp.repeat(jnp.arange(n_rows), GRID_W).astype(F32)
    col = jnp.tile(jnp.arange(GRID_W), n_rows).astype(F32)
    n_f = HEAD_DIM // 4
    inv = 1.0 / (ROPE_THETA ** (jnp.arange(n_f, dtype=F32) / n_f))
    ang = jnp.concatenate([row[:, None] * inv, col[:, None] * inv], axis=-1)
    cos, sin = jnp.cos(ang), jnp.sin(ang)
    cos_t = jnp.tile(jnp.concatenate([cos, cos], axis=-1), (1, N_HEADS))
    sin_t = jnp.tile(jnp.concatenate([-sin, sin], axis=-1), (1, N_HEADS))
    return cos_t, sin_t


def _pad_heads(kv):
    z = jnp.zeros_like(kv[:, :HEAD_DIM])
    h0, h1 = kv[:, :HEAD_DIM], kv[:, HEAD_DIM:]
    return jnp.concatenate([h0, z, z, h0, h1, z, z, h1], axis=-1).astype(BF16)


def _layer(x, mod, lp, consts, batch, seq, ctx_kv):
    is_ctx = ctx_kv is None
    g1 = lp["norm1_g"]
    w_in = lp["w_in"]
    qa, qw, kxa, vxa, kxw, vxw, kvf = _in_qkv(
        x, mod, g1, w_in, seq, consts["bd"], lp["q_norm_g"], lp["k_norm_g"],
        None if is_ctx else consts["rope"])
    dft = consts["dft_ctx"] if is_ctx else consts["dft_lat"]
    f = _fourier(x, mod, g1, w_in, consts["ccs"], dft, batch, seq, min(seq, FOURIER_ROWS))
    cv = _conv(x, mod, g1, w_in, lp["conv_dw"], lp["conv_b"], lp["conv_ln_g"], lp["conv_ln_b"], batch, seq)
    no_sink = jnp.full((N_HEADS,), NEG_INF, F32)
    if is_ctx:
        a = _attention(qa, kxa, vxa, no_sink, batch, seq)
        w = _attention(qw, kxw, vxw, lp["win_sink"], batch, seq)
    else:
        ka_c, va_c, kw_c, vw_c = ctx_kv
        a = _attention(qa, kxa, vxa, no_sink, batch, seq, ctx=(ka_c, va_c))
        w = _attention(qw, kxw, vxw, lp["win_sink"], batch, seq, ctx=(kw_c, vw_c), windowed=True)
    x1, h2t = _out_proj(f, cv, a, w, x, mod, lp["b_gate"], g1, lp["norm2_g"], w_in,
                        lp["w_fourier_out"], lp["w_conv_out"], lp["w_attn_out"], lp["w_win_out"],
                        lp["w_out"], seq)
    a0, n0, a1, r1 = _route(h2t, lp["w_peer_q_t"], lp["peer_subkeys"])
    x2 = _peer(h2t, a0, n0, a1, r1, lp["peer_u"], lp["peer_v_t"], x1, mod, lp["final_norm_g"], lp["is_last"], seq)
    return x2, kvf


def kernel(x_prompt, x_sample, cache_attn_k, cache_attn_v, cache_win_k, cache_win_v, c, c_ctx, norm1_g, norm2_g, w_ada, b_ada, w_in, b_gate, w_fourier_out, conv_dw, conv_b, conv_ln_g, conv_ln_b, w_conv_out, q_norm_g, k_norm_g, w_attn_out, win_sink, w_win_out, w_out, w_peer_q, peer_subkeys, peer_u, peer_v, final_norm_g):
    b_ctx, s_ctx, _ = x_prompt.shape
    b_lat, s_lat, _ = x_sample.shape
    past = cache_attn_k.shape[2]

    cond = jnp.concatenate([c_ctx[None, :], c, jnp.zeros((16 - 1 - b_lat, D_MODEL), F32)], axis=0)
    ada = _ada_all(cond, w_ada, b_ada).reshape(DEPTH, 16, 6, D_MODEL)

    consts = {
        "bd": jnp.asarray(np.kron(np.eye(N_HEADS), np.ones((HEAD_DIM, HEAD_DIM))), BF16),
        "ccs": jnp.asarray(_channel_dft_tables()).astype(BF16),
        "dft_ctx": jnp.asarray(_dft_tables(s_ctx)).astype(BF16),
        "dft_lat": jnp.asarray(_dft_tables(s_lat)).astype(BF16),
        "rope": _rope_tables(s_lat),
    }

    xp = x_prompt.reshape(b_ctx * s_ctx, D_MODEL)
    xs = x_sample.reshape(b_lat * s_lat, D_MODEL)
    kv_out = []
    for l in range(DEPTH):
        lp = {
            "norm1_g": norm1_g[l][None, :], "norm2_g": norm2_g[l][None, :],
            "w_in": w_in[l].astype(BF16), "b_gate": b_gate[l],
            "w_fourier_out": w_fourier_out[l].astype(BF16),
            "conv_dw": conv_dw[l], "conv_b": conv_b[l][None, :],
            "conv_ln_g": conv_ln_g[l][None, :], "conv_ln_b": conv_ln_b[l][None, :],
            "w_conv_out": w_conv_out[l].astype(BF16),
            "q_norm_g": jnp.tile(q_norm_g[l], N_HEADS)[None, :],
            "k_norm_g": jnp.tile(k_norm_g[l], 2)[None, :],
            "w_attn_out": w_attn_out[l].astype(BF16), "win_sink": win_sink[l],
            "w_win_out": w_win_out[l].astype(BF16), "w_out": w_out[l].astype(BF16),
            "w_peer_q_t": w_peer_q[l].T.astype(BF16),
            "peer_subkeys": peer_subkeys[l].astype(BF16),
            "peer_u": peer_u[l].astype(BF16),
            "peer_v_t": peer_v[l].T.astype(BF16),
            "final_norm_g": final_norm_g[None, :], "is_last": l == DEPTH - 1,
        }
        mod_ctx = ada[l, 0:1]
        mod_lat = ada[l, 1:1 + b_lat]
        xp, kvf = _layer(xp, mod_ctx, lp, consts, b_ctx, s_ctx, None)
        kv_out.append(kvf)
        cached = tuple(
            _pad_heads(cc[:, l].reshape(b_lat * past, ATTN_KV))
            for cc in (cache_attn_k, cache_attn_v, cache_win_k, cache_win_v))
        xs, _ = _layer(xs, mod_lat, lp, consts, b_lat, s_lat, cached)

    y_prompt = xp.reshape(b_ctx, s_ctx, D_MODEL)
    y_sample = xs.reshape(b_lat, s_lat, D_MODEL)
    outs = []
    for j in range(4):
        per_layer = [kvf[:, j * 128:(j + 1) * 128].reshape(b_ctx, s_ctx, 2, HEAD_DIM) for kvf in kv_out]
        outs.append(jnp.stack(per_layer, axis=1))
    return (y_prompt, y_sample, outs[0], outs[1], outs[2], outs[3])
```

```python
import functools

import numpy as np
import jax
import jax.numpy as jnp
from jax import lax
from jax.experimental import pallas as pl
from jax.experimental.pallas import tpu as pltpu

F32 = jnp.float32
BF16 = jnp.bfloat16

D_MODEL = 1024
DEPTH = 4
GRID_W = 64
HEAD_DIM = 64
FNET_GROUP_DIM = 64
FNET_WIDTH = 512
CONV_WIDTH = 512
CONV_K = 31
N_HEADS = 8
ATTN_Q = 512
ATTN_KV = 128
WINDOW = 128
ROPE_THETA = 10000.0
N_BRANCH = 4
COL_F, COL_C, COL_QKV, COL_GL = 0, 512, 1536, 3072
IN_COLS = 7168
N_KEYS = 128
N_EXPERTS = N_KEYS * N_KEYS
PEER_HEADS = 8
PEER_HALF = 128
PEER_TOPK = 16
EPS = 1e-6
NEG_INF = -1e30
LANES = 128
PACK_ROWS = 16
MXU_PIECE_ROWS = 512
FOURIER_ROWS = 1024
ATTN_Q_ROWS = 512

VMEM_LIMIT = 56 * 1024 * 1024


def _cparams(*sem):
    return pltpu.CompilerParams(dimension_semantics=sem, vmem_limit_bytes=VMEM_LIMIT)


def _dot(a, b):
    return jnp.dot(a, b, preferred_element_type=F32)


def _dot_nt(a, b):
    return lax.dot_general(a, b, (((1,), (1,)), ((), ())), preferred_element_type=F32)


def _normmod(x, g, sc, sh):
    ms = jnp.mean(x * x, axis=-1, keepdims=True)
    y = x * lax.rsqrt(ms + EPS)
    return (y * g) * (1.0 + sc) + sh


def _ada_kernel(c_ref, w_ref, b_ref, o_ref):
    c = c_ref[...]
    a = (c * jax.nn.sigmoid(c)).astype(BF16)
    o_ref[0] = _dot(a, w_ref[0].astype(BF16)) + b_ref[0]


def _ada_all(cond16, w_ada, b_ada):
    tn = 1536
    n = w_ada.shape[2]
    return pl.pallas_call(
        _ada_kernel,
        grid=(DEPTH, n // tn),
        in_specs=[
            pl.BlockSpec((16, D_MODEL), lambda l, j: (0, 0)),
            pl.BlockSpec((1, D_MODEL, tn), lambda l, j: (l, 0, j)),
            pl.BlockSpec((1, 1, tn), lambda l, j: (l, 0, j)),
        ],
        out_specs=pl.BlockSpec((1, 16, tn), lambda l, j: (l, 0, j)),
        out_shape=jax.ShapeDtypeStruct((DEPTH, 16, n), F32),
        compiler_params=_cparams("arbitrary", "arbitrary"),
        name="ada",
    )(cond16, w_ada, b_ada.reshape(DEPTH, 1, n))


def _seg_mean_sq(x, bd):
    sq = x * x
    hi = sq.astype(BF16)
    lo = (sq - hi.astype(F32)).astype(BF16)
    return (_dot(hi, bd) + _dot(lo, bd)) * (1.0 / HEAD_DIM)


def _rope(x, cos, sin_signed):
    w = x.shape[1]
    fwd = pltpu.roll(x, HEAD_DIM // 2, axis=1)
    bwd = pltpu.roll(x, w - HEAD_DIM // 2, axis=1)
    lane = lax.broadcasted_iota(jnp.int32, x.shape, 1)
    first = (lane % HEAD_DIM) < (HEAD_DIM // 2)
    partner = jnp.where(first, bwd, fwd)
    return x * cos + partner * sin_signed


def _store_head_padded(o_ref, k):
    kr = pltpu.roll(k, HEAD_DIM, axis=1)
    lane = lax.broadcasted_iota(jnp.int32, k.shape, 1)
    lo = lane < HEAD_DIM
    zero = jnp.zeros_like(k)
    o_ref[:, 0:128] = jnp.where(lo, k, zero).astype(BF16)
    o_ref[:, 128:256] = jnp.where(lo, zero, kr).astype(BF16)
    o_ref[:, 256:384] = jnp.where(lo, kr, zero).astype(BF16)
    o_ref[:, 384:512] = jnp.where(lo, zero, k).astype(BF16)


def _in_qkv_kernel(*refs, use_rope):
    if use_rope:
        (x_ref, mod_ref, g_ref, w_ref, bd_ref, qg_ref, kg_ref, cos_ref, sin_ref,
         qa_ref, qw_ref, kxa_ref, vxa_ref, kxw_ref, vxw_ref, kvf_ref) = refs
    else:
        (x_ref, mod_ref, g_ref, w_ref, bd_ref, qg_ref, kg_ref,
         qa_ref, qw_ref, kxa_ref, vxa_ref, kxw_ref, vxw_ref, kvf_ref) = refs
    h = _normmod(x_ref[...], g_ref[...], mod_ref[0, 1:2, :], mod_ref[0, 0:1, :]).astype(BF16)
    acc = _dot(h, w_ref[...])
    aq = acc[:, 0:512]
    ak = acc[:, 512:640]
    av = acc[:, 640:768]
    wq = acc[:, 768:1280]
    wk = acc[:, 1280:1408]
    wv = acc[:, 1408:1536]
    bd = bd_ref[...]
    aq = aq * lax.rsqrt(_seg_mean_sq(aq, bd) + EPS) * qg_ref[...]
    ak = ak * lax.rsqrt(_seg_mean_sq(ak, bd[0:128, 0:128]) + EPS) * kg_ref[...]
    if use_rope:
        cos = cos_ref[...]
        sin = sin_ref[...]
        aq = _rope(aq, cos, sin)
        wq = _rope(wq, cos, sin)
        ak = _rope(ak, cos[:, 0:128], sin[:, 0:128])
        wk = _rope(wk, cos[:, 0:128], sin[:, 0:128])
    scale = HEAD_DIM ** -0.5
    qa_ref[...] = (aq * scale).astype(BF16)
    qw_ref[...] = (wq * scale).astype(BF16)
    _store_head_padded(kxa_ref, ak)
    _store_head_padded(vxa_ref, av)
    _store_head_padded(kxw_ref, wk)
    _store_head_padded(vxw_ref, wv)
    kvf_ref[:, 0:128] = ak
    kvf_ref[:, 128:256] = av
    kvf_ref[:, 256:384] = wk
    kvf_ref[:, 384:512] = wv


def _in_qkv(x, mod, g, w_in, seq, bd, qg8, kg2, rope_tabs, tt=512):
    t = x.shape[0]
    nb = mod.shape[0]
    tpb = seq // tt
    mod_idx = (lambda i: (i // tpb, 0, 0)) if nb > 1 else (lambda i: (0, 0, 0))
    use_rope = rope_tabs is not None
    ncol = 1536
    in_specs = [
        pl.BlockSpec((tt, D_MODEL), lambda i: (i, 0)),
        pl.BlockSpec((1, 6, D_MODEL), mod_idx),
        pl.BlockSpec((1, D_MODEL), lambda i: (0, 0)),
        pl.BlockSpec((D_MODEL, ncol), lambda i: (0, COL_QKV // ncol)),
        pl.BlockSpec((512, 512), lambda i: (0, 0)),
        pl.BlockSpec((1, 512), lambda i: (0, 0)),
        pl.BlockSpec((1, 128), lambda i: (0, 0)),
    ]
    args = [x, mod, g, w_in, bd, qg8, kg2]
    if use_rope:
        in_specs += [pl.BlockSpec((tt, 512), lambda i: (i % tpb, 0))] * 2
        args += list(rope_tabs)
    tok = pl.BlockSpec((tt, 512), lambda i: (i, 0))
    return pl.pallas_call(
        functools.partial(_in_qkv_kernel, use_rope=use_rope),
        grid=(t // tt,),
        in_specs=in_specs,
        out_specs=[tok] * 7,
        out_shape=[jax.ShapeDtypeStruct((t, 512), BF16)] * 6 + [jax.ShapeDtypeStruct((t, 512), F32)],
        compiler_params=_cparams("arbitrary"),
        name="in_qkv",
    )(*args)


def _attn_kernel(*refs, tq, seq, n_ctx, windowed):
    if n_ctx:
        sink_ref, q_ref, kx_ref, vx_ref, kc_ref, vc_ref, o_ref = refs
    else:
        sink_ref, q_ref, kx_ref, vx_ref, o_ref = refs
    i = pl.program_id(1)
    if windowed:
        span = tq + 2 * WINDOW
        start = jnp.clip(i * tq - WINDOW, 0, seq - span)
        start = pl.multiple_of(start, WINDOW)
        qpos = i * tq + lax.broadcasted_iota(jnp.int32, (tq, span), 0)
        kpos = start + lax.broadcasted_iota(jnp.int32, (tq, span), 1)
        valid = jnp.abs(kpos - qpos) <= WINDOW
    for jb in range(4):
        q2 = q_ref[:, jb * 128:(jb + 1) * 128]
        acc = None
        for e in range(2):
            head = jb * 2 + e
            v_idx = (jb // 2) * 2 + e
            cols = slice(v_idx * 128, (v_idx + 1) * 128)
            if windowed:
                k_loc = kx_ref[pl.ds(start, span), cols]
                v_loc = vx_ref[pl.ds(start, span), cols]
            else:
                k_loc = kx_ref[:, cols]
                v_loc = vx_ref[:, cols]
            s = _dot_nt(q2, k_loc)
            if windowed:
                s = jnp.where(valid, s, NEG_INF)
            sink = sink_ref[head]
            m = jnp.maximum(jnp.max(s, axis=-1, keepdims=True), sink)
            if n_ctx:
                s_c = _dot_nt(q2, kc_ref[:, cols])
                m = jnp.maximum(m, jnp.max(s_c, axis=-1, keepdims=True))
            p = jnp.exp(s - m)
            l = jnp.sum(p, axis=-1, keepdims=True) + jnp.exp(sink - m)
            o = _dot(p.astype(BF16), v_loc)
            if n_ctx:
                p_c = jnp.exp(s_c - m)
                l = l + jnp.sum(p_c, axis=-1, keepdims=True)
                o = o + _dot(p_c.astype(BF16), vc_ref[:, cols])
            o = o / l
            acc = o if acc is None else acc + o
        o_ref[:, jb * 128:(jb + 1) * 128] = acc.astype(o_ref.dtype)


def _attention(q, kx, vx, sink, batch, seq, ctx=None, windowed=False):
    tq = min(seq, ATTN_Q_ROWS)
    assert not windowed or seq >= tq + 2 * WINDOW, "the banded key span must fit inside one sequence"
    n_ctx = 0 if ctx is None else ctx[0].shape[0] // batch
    nq = seq // tq
    in_specs = [
        pl.BlockSpec(memory_space=pltpu.SMEM),
        pl.BlockSpec((tq, 512), lambda b, i: (b * nq + i, 0)),
        pl.BlockSpec((seq, 512), lambda b, i: (b, 0)),
        pl.BlockSpec((seq, 512), lambda b, i: (b, 0)),
    ]
    args = [sink, q, kx, vx]
    if n_ctx:
        in_specs += [pl.BlockSpec((n_ctx, 512), lambda b, i: (b, 0))] * 2
        args += list(ctx)
    return pl.pallas_call(
        functools.partial(_attn_kernel, tq=tq, seq=seq, n_ctx=n_ctx, windowed=windowed),
        grid=(batch, nq),
        in_specs=in_specs,
        out_specs=pl.BlockSpec((tq, 512), lambda b, i: (b * nq + i, 0)),
        out_shape=jax.ShapeDtypeStruct((batch * seq, 512), BF16),
        compiler_params=_cparams("arbitrary", "arbitrary"),
        name="attn_win" if windowed else "attn",
    )(*args)


def _fourier_kernel(x_ref, mod_ref, g_ref, w_ref, ccs_ref, dft_ref, o_ref, z_ref, *, seq, scale):
    @pl.when(pl.program_id(1) == 0)
    def _():
        rows = min(seq, PROJ_ROWS)

        def body(r, carry):
            base = pl.multiple_of(r * rows, rows)
            h = _normmod(x_ref[pl.ds(base, rows), :], g_ref[...], mod_ref[0, 1:2, :], mod_ref[0, 0:1, :])
            f = _dot(h.astype(BF16), w_ref[...]).astype(BF16)
            z = _dot(f, ccs_ref[...])
            z_ref[pl.ds(base, rows), :] = z[:, 0:FNET_WIDTH].astype(BF16)
            z_ref[pl.ds(seq + base, rows), :] = z[:, FNET_WIDTH:].astype(BF16)
            return carry

        lax.fori_loop(0, seq // rows, body, 0)

    o_ref[...] = (_dot(dft_ref[...], z_ref[...]) * scale).astype(o_ref.dtype)


def _fourier(x, mod, g, w_in, ccs, dft, batch, seq, tr):
    nr = seq // tr
    nb = mod.shape[0]
    mod_idx = (lambda b, i: (b, 0, 0)) if nb > 1 else (lambda b, i: (0, 0, 0))
    return pl.pallas_call(
        functools.partial(_fourier_kernel, seq=seq, scale=float((seq * FNET_GROUP_DIM) ** -0.5)),
        grid=(batch, nr),
        in_specs=[
            pl.BlockSpec((seq, D_MODEL), lambda b, i: (b, 0)),
            pl.BlockSpec((1, 6, D_MODEL), mod_idx),
            pl.BlockSpec((1, D_MODEL), lambda b, i: (0, 0)),
            pl.BlockSpec((D_MODEL, FNET_WIDTH), lambda b, i: (0, COL_F // FNET_WIDTH)),
            pl.BlockSpec((FNET_WIDTH, 2 * FNET_WIDTH), lambda b, i: (0, 0)),
            pl.BlockSpec((tr, 2 * seq), lambda b, i: (i, 0)),
        ],
        out_specs=pl.BlockSpec((tr, FNET_WIDTH), lambda b, i: (b * nr + i, 0)),
        out_shape=jax.ShapeDtypeStruct((batch * seq, FNET_WIDTH), BF16),
        scratch_shapes=[pltpu.VMEM((2 * seq, FNET_WIDTH), BF16)],
        compiler_params=_cparams("arbitrary", "arbitrary"),
        name="fourier",
    )(x, mod, g, w_in, ccs, dft)


def _dft_tables(seq):
    k = np.arange(seq, dtype=np.int64)
    ang = 2.0 * np.pi * ((k[:, None] * k[None, :]) % seq).astype(np.float64) / seq
    return np.concatenate([np.cos(ang), -np.sin(ang)], axis=1).astype(np.float32)


def _channel_dft_tables():
    g = FNET_GROUP_DIM
    k = np.arange(g, dtype=np.int64)
    ang = 2.0 * np.pi * ((k[:, None] * k[None, :]) % g).astype(np.float64) / g
    eye = np.eye(FNET_WIDTH // g)
    return np.concatenate([np.kron(eye, np.cos(ang)), np.kron(eye, np.sin(ang))], axis=1).astype(np.float32)


CONV_PAD = 16
CONV_ROWS = 128
PROJ_ROWS = 512


def _conv_kernel(x_ref, mod_ref, g_ref, wa_ref, wg_ref, dw_ref, b_ref, lg_ref, lb_ref, o_ref, hp_ref, *, seq):
    zeros = jnp.zeros((CONV_PAD, CONV_WIDTH), F32)
    hp_ref[0:CONV_PAD, :] = zeros
    hp_ref[seq + CONV_PAD:seq + 2 * CONV_PAD, :] = zeros

    rows = min(seq, PROJ_ROWS)

    def glu_body(r, carry):
        base = pl.multiple_of(r * rows, rows)
        h = _normmod(x_ref[pl.ds(base, rows), :], g_ref[...], mod_ref[0, 1:2, :], mod_ref[0, 0:1, :])
        h = h.astype(BF16)
        hp_ref[pl.ds(base + CONV_PAD, rows), :] = _dot(h, wa_ref[...]) * jax.nn.sigmoid(_dot(h, wg_ref[...]))
        return carry

    lax.fori_loop(0, seq // rows, glu_body, 0)

    blk_rows = CONV_ROWS + 2 * CONV_PAD

    def conv_body(r, carry):
        base = pl.multiple_of(r * CONV_ROWS, CONV_ROWS)
        outs = []
        for cch in range(CONV_WIDTH // LANES):
            ls = slice(cch * LANES, (cch + 1) * LANES)
            blk = hp_ref[pl.ds(base, blk_rows), ls]
            acc = jnp.zeros((CONV_ROWS, LANES), F32)
            for s in range(8):
                blk_s = blk if s == 0 else pltpu.roll(blk, blk_rows - s, axis=0)
                for m in range(4):
                    off = 8 * m + s
                    k = off - 1
                    if 0 <= k < CONV_K:
                        acc = acc + blk_s[8 * m:8 * m + CONV_ROWS, :] * dw_ref[k:k + 1, ls]
            outs.append(acc)
        y = jnp.concatenate(outs, axis=1) + b_ref[...]
        mu = jnp.mean(y, axis=-1, keepdims=True)
        yc = y - mu
        var = jnp.mean(yc * yc, axis=-1, keepdims=True)
        hn = yc * lax.rsqrt(var + EPS) * lg_ref[...] + lb_ref[...]
        o_ref[pl.ds(base, CONV_ROWS), :] = (hn * jax.nn.sigmoid(hn)).astype(o_ref.dtype)
        return carry

    lax.fori_loop(0, seq // CONV_ROWS, conv_body, 0)


def _conv(x, mod, g, w_in, dw, b, lg, lb, batch, seq):
    vec = pl.BlockSpec((1, CONV_WIDTH), lambda bi: (0, 0))
    nb = mod.shape[0]
    mod_idx = (lambda bi: (bi, 0, 0)) if nb > 1 else (lambda bi: (0, 0, 0))
    return pl.pallas_call(
        functools.partial(_conv_kernel, seq=seq),
        grid=(batch,),
        in_specs=[
            pl.BlockSpec((seq, D_MODEL), lambda bi: (bi, 0)),
            pl.BlockSpec((1, 6, D_MODEL), mod_idx),
            pl.BlockSpec((1, D_MODEL), lambda bi: (0, 0)),
            pl.BlockSpec((D_MODEL, CONV_WIDTH), lambda bi: (0, COL_C // CONV_WIDTH)),
            pl.BlockSpec((D_MODEL, CONV_WIDTH), lambda bi: (0, COL_C // CONV_WIDTH + 1)),
            pl.BlockSpec((CONV_K, CONV_WIDTH), lambda bi: (0, 0)),
            vec, vec, vec,
        ],
        out_specs=pl.BlockSpec((seq, CONV_WIDTH), lambda bi: (bi, 0)),
        out_shape=jax.ShapeDtypeStruct((batch * seq, CONV_WIDTH), BF16),
        scratch_shapes=[pltpu.VMEM((seq + 2 * CONV_PAD, CONV_WIDTH), F32)],
        compiler_params=_cparams("arbitrary"),
        name="conv",
    )(x, mod, g, w_in, w_in, dw, b, lg, lb)


def _out_kernel(f_ref, c_ref, a_ref, w_ref, x_ref, mod_ref, bg_ref, g1_ref, g2_ref,
                wg0_ref, wg1_ref, wg2_ref, wg3_ref, wf_ref, wc_ref, wa_ref, ww_ref, wo_ref, x1_ref, h2t_ref):
    x = x_ref[...]
    h1 = _normmod(x, g1_ref[...], mod_ref[0, 1:2, :], mod_ref[0, 0:1, :]).astype(BF16)
    merged = None
    branches = ((f_ref, wf_ref, wg0_ref), (c_ref, wc_ref, wg1_ref), (a_ref, wa_ref, wg2_ref), (w_ref, ww_ref, wg3_ref))
    for i, (br, wt, wg) in enumerate(branches):
        y = _dot(br[...], wt[...])
        gate = jax.nn.sigmoid(_dot(h1, wg[...]) + bg_ref[i:i + 1, :])
        merged = gate * y if merged is None else merged + gate * y
    mix = _dot(merged.astype(BF16), wo_ref[...])
    x1 = x + mod_ref[0, 2:3, :] * mix
    x1_ref[...] = x1
    h2 = _normmod(x1, g2_ref[...], mod_ref[0, 4:5, :], mod_ref[0, 3:4, :])
    h2t_ref[...] = h2.T.astype(BF16)


def _out_proj(f, c, a, w, x, mod, bg, g1, g2, w_in, wf, wc, wa, ww, wo, seq, tt=256):
    t = x.shape[0]
    nb = mod.shape[0]
    tpb = seq // tt
    mod_idx = (lambda i: (i // tpb, 0, 0)) if nb > 1 else (lambda i: (0, 0, 0))
    br = pl.BlockSpec((tt, 512), lambda i: (i, 0))
    wbr = pl.BlockSpec((512, D_MODEL), lambda i: (0, 0))
    vec = pl.BlockSpec((1, D_MODEL), lambda i: (0, 0))
    gl0 = COL_GL // D_MODEL
    wgl = [pl.BlockSpec((D_MODEL, D_MODEL), functools.partial(lambda i, j: (0, j), j=gl0 + k)) for k in range(N_BRANCH)]
    return pl.pallas_call(
        _out_kernel,
        grid=(t // tt,),
        in_specs=[
            br, br, br, br,
            pl.BlockSpec((tt, D_MODEL), lambda i: (i, 0)),
            pl.BlockSpec((1, 6, D_MODEL), mod_idx),
            pl.BlockSpec((N_BRANCH, D_MODEL), lambda i: (0, 0)),
            vec, vec,
            *wgl,
            wbr, wbr, wbr, wbr,
            pl.BlockSpec((D_MODEL, D_MODEL), lambda i: (0, 0)),
        ],
        out_specs=[
            pl.BlockSpec((tt, D_MODEL), lambda i: (i, 0)),
            pl.BlockSpec((D_MODEL, tt), lambda i: (0, i)),
        ],
        out_shape=[jax.ShapeDtypeStruct((t, D_MODEL), F32), jax.ShapeDtypeStruct((D_MODEL, t), BF16)],
        compiler_params=_cparams("arbitrary"),
        name="out_proj",
    )(f, c, a, w, x, mod, bg, g1, g2, w_in, w_in, w_in, w_in, wf, wc, wa, ww, wo)


NOT_TOP = 99.0


def _bf16_twice(x):
    bits = pltpu.bitcast(x.astype(BF16).astype(F32), jnp.uint32)
    return bits | (bits >> 16)


def _route_kernel(h2t_ref, wq_ref, sk_ref, a0_ref, n0_ref, a1_ref, r1_ref,
                  qt_ref, sc_ref, srt_ref, rnk_ref, flag_ref, *, tt):
    ncol = tt // LANES
    qt_ref[...] = _dot(wq_ref[...], h2t_ref[...]).astype(BF16)
    for hp in range(2 * PEER_HEADS):
        sc_ref[hp] = _dot(sk_ref[hp % 2], qt_ref[hp * PEER_HALF:(hp + 1) * PEER_HALF, :])

    key_iota = lax.broadcasted_iota(jnp.int32, (N_KEYS, LANES), 0).astype(F32)

    sub8 = lax.broadcasted_iota(jnp.int32, (8, LANES), 0).astype(F32)

    def _tree(op, xs):
        xs = list(xs)
        while len(xs) > 1:
            xs = [op(xs[i], xs[i + 1]) if i + 1 < len(xs) else xs[i] for i in range(0, len(xs), 2)]
        return xs[0]

    def note_ties(count):
        flag_ref[...] = jnp.maximum(flag_ref[...], jnp.where(count != float(PEER_TOPK), 1.0, 0.0))

    def top16(s, exact):
        rank = jnp.full((N_KEYS, LANES), NOT_TOP, F32)
        tops = []
        for it in range(PEER_TOPK):
            m = jnp.max(s, axis=0, keepdims=True)
            if exact:
                first = jnp.min(jnp.where(s == m, key_iota, float(N_KEYS)), axis=0, keepdims=True)
                hit = key_iota == first
            else:
                hit = s == m
            s = jnp.where(hit, -jnp.inf, s)
            rank = jnp.where(hit, float(it), rank)
            tops.append(m)
        if not exact:
            note_ties(jnp.sum(jnp.where(rank < float(PEER_TOPK), 1.0, 0.0), axis=0, keepdims=True))
        return jnp.concatenate(tops, axis=0), rank

    def combine(h, ls, exact):
        s0 = srt_ref[2 * h, :, ls]
        s1 = srt_ref[2 * h + 1, :, ls]
        cands = [s0[0:1, :] + s1[0:8, :], s0[0:1, :] + s1[8:16, :]]
        poss = [sub8, sub8 + 8.0]
        for a in range(1, 8):
            c = s0[a:a + 1, :] + s1[0:8, :]
            cands.append(jnp.where(sub8 < float(16 // (a + 1)), c, -jnp.inf))
            poss.append(sub8 + float(16 * a))
        cands.append(s0[8:16, :] + s1[0:1, :])
        poss.append((sub8 + 8.0) * 16.0)
        sels = [jnp.zeros_like(c) for c in cands]
        tops = []
        for it in range(PEER_TOPK):
            m = jnp.max(_tree(jnp.maximum, cands), axis=0, keepdims=True)
            if exact:
                firsts = [jnp.where(c == m, p, 999.0) for c, p in zip(cands, poss)]
                first = jnp.min(_tree(jnp.minimum, firsts), axis=0, keepdims=True)
            for j in range(len(cands)):
                hit = (poss[j] == first) if exact else (cands[j] == m)
                cands[j] = jnp.where(hit, -jnp.inf, cands[j])
                sels[j] = jnp.where(hit, 1.0, sels[j])
            tops.append(m)
        z = _tree(jnp.add, [jnp.exp(m - tops[0]) for m in tops])
        inv_z = 1.0 / z
        n_rows = [jnp.sum(sels[0] + sels[1], axis=0, keepdims=True)]
        n_rows += [jnp.sum(sels[a + 1], axis=0, keepdims=True) for a in range(1, 8)]
        n_hi = sels[9]
        if not exact:
            note_ties(_tree(jnp.add, n_rows) + jnp.sum(n_hi, axis=0, keepdims=True))
        rank0 = rnk_ref[2 * h, :, ls]
        rank1 = rnk_ref[2 * h + 1, :, ls]
        n0 = jnp.zeros((N_KEYS, LANES), F32)
        for a in range(8):
            n0 = jnp.where(rank0 == float(a), n_rows[a], n0)
            n0 = jnp.where(rank0 == float(a + 8), n_hi[a:a + 1, :], n0)
        sc0 = sc_ref[2 * h, :, ls]
        sc1 = sc_ref[2 * h + 1, :, ls]
        a0 = jnp.where(rank0 < float(PEER_TOPK), jnp.exp(sc0 - s0[0:1, :]) * inv_z, 0.0)
        a1 = jnp.where(rank1 < float(PEER_TOPK), jnp.exp(sc1 - s1[0:1, :]), 0.0)
        a0_ref[h, :, ls] = _bf16_twice(a0)
        n0_ref[h, :, ls] = _bf16_twice(n0)
        a1_ref[h, :, ls] = pltpu.bitcast(a1.astype(BF16), jnp.uint32)
        r1_ref[h, :, ls] = pltpu.bitcast(rank1.astype(BF16), jnp.uint32)

    def stage1(hp, carry, exact):
        for col in range(ncol):
            ls = slice(col * LANES, (col + 1) * LANES)
            tops, rank = top16(sc_ref[hp, :, ls], exact)
            srt_ref[hp, :, ls] = tops
            rnk_ref[hp, :, ls] = rank
        return carry

    def stage2(h, carry):
        flag_ref[...] = jnp.zeros_like(flag_ref)
        for col in range(ncol):
            combine(h, slice(col * LANES, (col + 1) * LANES), False)

        @pl.when(jnp.max(flag_ref[...]) > 0.0)
        def _():
            for col in range(ncol):
                combine(h, slice(col * LANES, (col + 1) * LANES), True)

        return carry

    flag_ref[...] = jnp.zeros_like(flag_ref)
    lax.fori_loop(0, 2 * PEER_HEADS, functools.partial(stage1, exact=False), 0)

    @pl.when(jnp.max(flag_ref[...]) > 0.0)
    def _():
        lax.fori_loop(0, 2 * PEER_HEADS, functools.partial(stage1, exact=True), 0)

    lax.fori_loop(0, PEER_HEADS, stage2, 0)


def _route(h2t, wq_t, sk, tt=512):
    t = h2t.shape[1]
    out0 = pl.BlockSpec((PEER_HEADS, N_KEYS, tt), lambda i: (0, 0, i))
    out1 = pl.BlockSpec((PEER_HEADS, N_KEYS // 2, tt), lambda i: (0, 0, i))
    shp0 = jax.ShapeDtypeStruct((PEER_HEADS, N_KEYS, t), jnp.uint32)
    shp1 = jax.ShapeDtypeStruct((PEER_HEADS, N_KEYS // 2, t), jnp.uint32)
    return pl.pallas_call(
        functools.partial(_route_kernel, tt=tt),
        grid=(t // tt,),
        in_specs=[
            pl.BlockSpec((D_MODEL, tt), lambda i: (0, i)),
            pl.BlockSpec((2 * PEER_HEADS * PEER_HALF, D_MODEL), lambda i: (0, 0)),
            pl.BlockSpec((2, N_KEYS, PEER_HALF), lambda i: (0, 0, 0)),
        ],
        out_specs=[out0, out0, out1, out1],
        out_shape=[shp0, shp0, shp1, shp1],
        scratch_shapes=[
            pltpu.VMEM((2 * PEER_HEADS * PEER_HALF, tt), BF16),
            pltpu.VMEM((2 * PEER_HEADS, N_KEYS, tt), F32),
            pltpu.VMEM((2 * PEER_HEADS, PEER_TOPK, tt), F32),
            pltpu.VMEM((2 * PEER_HEADS, N_KEYS, tt), F32),
            pltpu.VMEM((8, LANES), F32),
        ],
        compiler_params=_cparams("arbitrary"),
        name="peer_route",
    )(h2t, wq_t, sk)


def _gelu_tanh(x):
    c = 2.0 * 0.7978845608028654
    neg_u = x * ((-c * 0.044715) * (x * x) - c)
    return x / (1.0 + jnp.exp(neg_u))


def _peer_step(h2t_ref, a0_ref, n0_ref, a1_ref, r1_ref, u_ref, v_ref, acc_ref,
               s_new, s_cur, w_cur, w_done, i0, *, tt, n_i):
    ncol = tt // LANES
    zero = jnp.zeros((PACK_ROWS, LANES), BF16)
    rows_per_piece = MXU_PIECE_ROWS // N_KEYS
    for il in range(n_i):
        mr = slice(il // rows_per_piece * MXU_PIECE_ROWS, (il // rows_per_piece + 1) * MXU_PIECE_ROWS)
        if il % rows_per_piece == 0:
            s_new[mr, :] = _dot(u_ref[mr, :], h2t_ref[...])
        for col in range(ncol):
            if il % rows_per_piece == rows_per_piece // 2 and col == 0:
                acc_ref[:, mr] += lax.dot_general(w_done[...], v_ref[:, mr], (((0,), (0,)), ((), ())),
                                                  preferred_element_type=F32)
            ls = slice(col * LANES, (col + 1) * LANES)
            n0 = [pltpu.bitcast(jnp.broadcast_to(n0_ref[h, pl.ds(i0, n_i), ls][il:il + 1, :], (8, LANES)), BF16)
                  for h in range(PEER_HEADS)]
            a0 = [pltpu.bitcast(jnp.broadcast_to(a0_ref[h, pl.ds(i0, n_i), ls][il:il + 1, :], (8, LANES)), BF16)
                  for h in range(PEER_HEADS)]
            for k in range(N_KEYS // PACK_ROWS):
                pr = slice(k * PACK_ROWS // 2, (k + 1) * PACK_ROWS // 2)
                gate = None
                for h in range(PEER_HEADS):
                    r1 = pltpu.bitcast(r1_ref[h, pr, ls], BF16)
                    a1 = pltpu.bitcast(a1_ref[h, pr, ls], BF16)
                    term = jnp.where(r1 < n0[h], a0[h] * a1, zero)
                    gate = term if gate is None else gate + term
                rows = slice(il * N_KEYS + k * PACK_ROWS, il * N_KEYS + (k + 1) * PACK_ROWS)
                act = _gelu_tanh(s_cur[rows, ls])
                w_cur[rows, ls] = gate * act.astype(BF16)


def _peer_kernel(h2t_ref, a0_ref, n0_ref, a1_ref, r1_ref, u_ref, v_ref, x1_ref, mod_ref, fg_ref,
                 o_ref, s0_ref, s1_ref, w0_ref, w1_ref, acc_ref, *, tt, ec, ne, total, final):
    g = pl.program_id(0)
    n_i = ec // N_KEYS
    assert n_i == 8, "one expert chunk must cover exactly one sublane tile of half-0 keys"
    g_gate = jnp.clip(g - 1, 0, total - 1)
    c_out = jnp.clip(g - 2, 0, total - 1) % ne
    i0 = pl.multiple_of((g_gate % ne) * n_i, n_i)

    @pl.when(g == 0)
    def _():
        s1_ref[...] = jnp.zeros_like(s1_ref)
        w0_ref[...] = jnp.zeros_like(w0_ref)
        w1_ref[...] = jnp.zeros_like(w1_ref)

    @pl.when(c_out == 0)
    def _():
        acc_ref[...] = jnp.zeros_like(acc_ref)

    step = functools.partial(_peer_step, h2t_ref, a0_ref, n0_ref, a1_ref, r1_ref, u_ref, v_ref, acc_ref,
                             tt=tt, n_i=n_i)

    @pl.when(g % 2 == 0)
    def _():
        step(s0_ref, s1_ref, w1_ref, w0_ref, i0)

    @pl.when(g % 2 == 1)
    def _():
        step(s1_ref, s0_ref, w0_ref, w1_ref, i0)

    @pl.when((g >= 2) & (c_out == ne - 1))
    def _():
        y = x1_ref[...] + mod_ref[0, 5:6, :] * acc_ref[...]
        if final:
            y = y * lax.rsqrt(jnp.mean(y * y, axis=-1, keepdims=True) + EPS) * fg_ref[...]
        o_ref[...] = y


def _peer(h2t, a0, n0, a1, r1, u, v, x1, mod, final_g, final, seq, tt=512, ec=1024):
    t = x1.shape[0]
    nb = mod.shape[0]
    tpb = seq // tt
    ne = N_EXPERTS // ec
    total = (t // tt) * ne
    assert ec == D_MODEL, "the kernel walks U-chunk rows and out^T rows with one 128-row block index"

    def st_a(g):
        return jnp.minimum(g, total - 1)

    def st_b(g):
        return jnp.clip(g - 1, 0, total - 1)

    def st_c(g):
        return jnp.clip(g - 2, 0, total - 1)

    mod_idx = (lambda g: (st_c(g) // ne // tpb, 0, 0)) if nb > 1 else (lambda g: (0, 0, 0))
    rt0 = pl.BlockSpec((PEER_HEADS, N_KEYS, tt), lambda g: (0, 0, st_b(g) // ne))
    rt1 = pl.BlockSpec((PEER_HEADS, N_KEYS // 2, tt), lambda g: (0, 0, st_b(g) // ne))
    return pl.pallas_call(
        functools.partial(_peer_kernel, tt=tt, ec=ec, ne=ne, total=total, final=final),
        grid=(total + 2,),
        in_specs=[
            pl.BlockSpec((D_MODEL, tt), lambda g: (0, st_a(g) // ne)),
            rt0, rt0, rt1, rt1,
            pl.BlockSpec((ec, D_MODEL), lambda g: (st_a(g) % ne, 0)),
            pl.BlockSpec((ec, D_MODEL), lambda g: (st_c(g) % ne, 0)),
            pl.BlockSpec((tt, D_MODEL), lambda g: (st_c(g) // ne, 0)),
            pl.BlockSpec((1, 6, D_MODEL), mod_idx),
            pl.BlockSpec((1, D_MODEL), lambda g: (0, 0)),
        ],
        out_specs=pl.BlockSpec((tt, D_MODEL), lambda g: (st_c(g) // ne, 0)),
        out_shape=jax.ShapeDtypeStruct((t, D_MODEL), F32),
        scratch_shapes=[
            pltpu.VMEM((ec, tt), F32), pltpu.VMEM((ec, tt), F32),
            pltpu.VMEM((ec, tt), BF16), pltpu.VMEM((ec, tt), BF16),
            pltpu.VMEM((tt, D_MODEL), F32),
        ],
        compiler_params=_cparams("arbitrary"),
        name="peer_mix",
    )(h2t, a0, n0, a1, r1, u, v, x1, mod, final_g)


def _rope_tables(seq):
    n_rows = seq // GRID_W
    row = jnp.repeat(jnp.arange(n_rows), GRID_W).astype(F32)
    col = jnp.tile(jnp.arange(GRID_W), n_rows).astype(F32)
    n_f = HEAD_DIM // 4
    inv = 1.0 / (ROPE_THETA ** (jnp.arange(n_f, dtype=F32) / n_f))
    ang = jnp.concatenate([row[:, None] * inv, col[:, None] * inv], axis=-1)
    cos, sin = jnp.cos(ang), jnp.sin(ang)
    cos_t = jnp.tile(jnp.concatenate([cos, cos], axis=-1), (1, N_HEADS))
    sin_t = jnp.tile(jnp.concatenate([-sin, sin], axis=-1), (1, N_HEADS))
    return cos_t, sin_t


def _pad_heads(kv):
    z = jnp.zeros_like(kv[:, :HEAD_DIM])
    h0, h1 = kv[:, :HEAD_DIM], kv[:, HEAD_DIM:]
    return jnp.concatenate([h0, z, z, h0, h1, z, z, h1], axis=-1).astype(BF16)


def _layer(x, mod, lp, consts, batch, seq, ctx_kv):
    is_ctx = ctx_kv is None
    g1 = lp["norm1_g"]
    w_in = lp["w_in"]
    qa, qw, kxa, vxa, kxw, vxw, kvf = _in_qkv(
        x, mod, g1, w_in, seq, consts["bd"], lp["q_norm_g"], lp["k_norm_g"],
        None if is_ctx else consts["rope"])
    dft = consts["dft_ctx"] if is_ctx else consts["dft_lat"]
    f = _fourier(x, mod, g1, w_in, consts["ccs"], dft, batch, seq, min(seq, FOURIER_ROWS))
    cv = _conv(x, mod, g1, w_in, lp["conv_dw"], lp["conv_b"], lp["conv_ln_g"], lp["conv_ln_b"], batch, seq)
    no_sink = jnp.full((N_HEADS,), NEG_INF, F32)
    if is_ctx:
        a = _attention(qa, kxa, vxa, no_sink, batch, seq)
        w = _attention(qw, kxw, vxw, lp["win_sink"], batch, seq)
    else:
        ka_c, va_c, kw_c, vw_c = ctx_kv
        a = _attention(qa, kxa, vxa, no_sink, batch, seq, ctx=(ka_c, va_c))
        w = _attention(qw, kxw, vxw, lp["win_sink"], batch, seq, ctx=(kw_c, vw_c), windowed=True)
    x1, h2t = _out_proj(f, cv, a, w, x, mod, lp["b_gate"], g1, lp["norm2_g"], w_in,
                        lp["w_fourier_out"], lp["w_conv_out"], lp["w_attn_out"], lp["w_win_out"],
                        lp["w_out"], seq)
    a0, n0, a1, r1 = _route(h2t, lp["w_peer_q_t"], lp["peer_subkeys"])
    x2 = _peer(h2t, a0, n0, a1, r1, lp["peer_u"], lp["peer_v"], x1, mod, lp["final_norm_g"], lp["is_last"], seq)
    return x2, kvf


def kernel(x_prompt, x_sample, cache_attn_k, cache_attn_v, cache_win_k, cache_win_v, c, c_ctx, norm1_g, norm2_g, w_ada, b_ada, w_in, b_gate, w_fourier_out, conv_dw, conv_b, conv_ln_g, conv_ln_b, w_conv_out, q_norm_g, k_norm_g, w_attn_out, win_sink, w_win_out, w_out, w_peer_q, peer_subkeys, peer_u, peer_v, final_norm_g):
    b_ctx, s_ctx, _ = x_prompt.shape
    b_lat, s_lat, _ = x_sample.shape
    past = cache_attn_k.shape[2]

    cond = jnp.concatenate([c_ctx[None, :], c, jnp.zeros((16 - 1 - b_lat, D_MODEL), F32)], axis=0)
    ada = _ada_all(cond, w_ada, b_ada).reshape(DEPTH, 16, 6, D_MODEL)

    consts = {
        "bd": jnp.asarray(np.kron(np.eye(N_HEADS), np.ones((HEAD_DIM, HEAD_DIM))), BF16),
        "ccs": jnp.asarray(_channel_dft_tables()).astype(BF16),
        "dft_ctx": jnp.asarray(_dft_tables(s_ctx)).astype(BF16),
        "dft_lat": jnp.asarray(_dft_tables(s_lat)).astype(BF16),
        "rope": _rope_tables(s_lat),
    }

    xp = x_prompt.reshape(b_ctx * s_ctx, D_MODEL)
    xs = x_sample.reshape(b_lat * s_lat, D_MODEL)
    kv_out = []
    for l in range(DEPTH):
        lp = {
            "norm1_g": norm1_g[l][None, :], "norm2_g": norm2_g[l][None, :],
            "w_in": w_in[l].astype(BF16), "b_gate": b_gate[l],
            "w_fourier_out": w_fourier_out[l].astype(BF16),
            "conv_dw": conv_dw[l], "conv_b": conv_b[l][None, :],
            "conv_ln_g": conv_ln_g[l][None, :], "conv_ln_b": conv_ln_b[l][None, :],
            "w_conv_out": w_conv_out[l].astype(BF16),
            "q_norm_g": jnp.tile(q_norm_g[l], N_HEADS)[None, :],
            "k_norm_g": jnp.tile(k_norm_g[l], 2)[None, :],
            "w_attn_out": w_attn_out[l].astype(BF16), "win_sink": win_sink[l],
            "w_win_out": w_win_out[l].astype(BF16), "w_out": w_out[l].astype(BF16),
            "w_peer_q_t": w_peer_q[l].T.astype(BF16),
            "peer_subkeys": peer_subkeys[l].astype(BF16),
            "peer_u": peer_u[l].astype(BF16),
            "peer_v": peer_v[l].astype(BF16),
            "final_norm_g": final_norm_g[None, :], "is_last": l == DEPTH - 1,
        }
        mod_ctx = ada[l, 0:1]
        mod_lat = ada[l, 1:1 + b_lat]
        xp, kvf = _layer(xp, mod_ctx, lp, consts, b_ctx, s_ctx, None)
        kv_out.append(kvf)
        cached = tuple(
            _pad_heads(cc[:, l].reshape(b_lat * past, ATTN_KV))
            for cc in (cache_attn_k, cache_attn_v, cache_win_k, cache_win_v))
        xs, _ = _layer(xs, mod_lat, lp, consts, b_lat, s_lat, cached)

    y_prompt = xp.reshape(b_ctx, s_ctx, D_MODEL)
    y_sample = xs.reshape(b_lat, s_lat, D_MODEL)
    outs = []
    for j in range(4):
        per_layer = [kvf[:, j * 128:(j + 1) * 128].reshape(b_ctx, s_ctx, 2, HEAD_DIM) for kvf in kv_out]
        outs.append(jnp.stack(per_layer, axis=1))
    return (y_prompt, y_sample, outs[0], outs[1], outs[2], outs[3])
```

```python
import functools

import numpy as np
import jax
import jax.numpy as jnp
from jax import lax
from jax.experimental import pallas as pl
from jax.experimental.pallas import tpu as pltpu

F32 = jnp.float32
BF16 = jnp.bfloat16

D_MODEL = 1024
DEPTH = 4
GRID_W = 64
HEAD_DIM = 64
FNET_GROUP_DIM = 64
FNET_WIDTH = 512
CONV_WIDTH = 512
CONV_K = 31
N_HEADS = 8
ATTN_Q = 512
ATTN_KV = 128
WINDOW = 128
ROPE_THETA = 10000.0
N_BRANCH = 4
COL_F, COL_C, COL_QKV, COL_GL = 0, 512, 1536, 3072
IN_COLS = 7168
N_KEYS = 128
N_EXPERTS = N_KEYS * N_KEYS
PEER_HEADS = 8
PEER_HALF = 128
PEER_TOPK = 16
EPS = 1e-6
NEG_INF = -1e30
LANES = 128
PACK_ROWS = 16
MXU_PIECE_ROWS = 512
FOURIER_ROWS = 1024
ATTN_Q_ROWS = 512

VMEM_LIMIT = 56 * 1024 * 1024


def _cparams(*sem):
    return pltpu.CompilerParams(dimension_semantics=sem, vmem_limit_bytes=VMEM_LIMIT)


def _dot(a, b):
    return jnp.dot(a, b, preferred_element_type=F32)


def _dot_nt(a, b):
    return lax.dot_general(a, b, (((1,), (1,)), ((), ())), preferred_element_type=F32)


def _normmod(x, g, sc, sh):
    ms = jnp.mean(x * x, axis=-1, keepdims=True)
    y = x * lax.rsqrt(ms + EPS)
    return (y * g) * (1.0 + sc) + sh


def _ada_kernel(c_ref, w_ref, b_ref, o_ref):
    c = c_ref[...]
    a = (c * jax.nn.sigmoid(c)).astype(BF16)
    o_ref[0] = _dot(a, w_ref[0].astype(BF16)) + b_ref[0]


def _ada_all(cond16, w_ada, b_ada):
    tn = 1536
    n = w_ada.shape[2]
    return pl.pallas_call(
        _ada_kernel,
        grid=(DEPTH, n // tn),
        in_specs=[
            pl.BlockSpec((16, D_MODEL), lambda l, j: (0, 0)),
            pl.BlockSpec((1, D_MODEL, tn), lambda l, j: (l, 0, j)),
            pl.BlockSpec((1, 1, tn), lambda l, j: (l, 0, j)),
        ],
        out_specs=pl.BlockSpec((1, 16, tn), lambda l, j: (l, 0, j)),
        out_shape=jax.ShapeDtypeStruct((DEPTH, 16, n), F32),
        compiler_params=_cparams("arbitrary", "arbitrary"),
        name="ada",
    )(cond16, w_ada, b_ada.reshape(DEPTH, 1, n))


def _seg_mean_sq(x, bd):
    sq = x * x
    hi = sq.astype(BF16)
    lo = (sq - hi.astype(F32)).astype(BF16)
    return (_dot(hi, bd) + _dot(lo, bd)) * (1.0 / HEAD_DIM)


def _rope(x, cos, sin_signed):
    w = x.shape[1]
    fwd = pltpu.roll(x, HEAD_DIM // 2, axis=1)
    bwd = pltpu.roll(x, w - HEAD_DIM // 2, axis=1)
    lane = lax.broadcasted_iota(jnp.int32, x.shape, 1)
    first = (lane % HEAD_DIM) < (HEAD_DIM // 2)
    partner = jnp.where(first, bwd, fwd)
    return x * cos + partner * sin_signed


def _store_head_padded(o_ref, k):
    kr = pltpu.roll(k, HEAD_DIM, axis=1)
    lane = lax.broadcasted_iota(jnp.int32, k.shape, 1)
    lo = lane < HEAD_DIM
    zero = jnp.zeros_like(k)
    o_ref[:, 0:128] = jnp.where(lo, k, zero).astype(BF16)
    o_ref[:, 128:256] = jnp.where(lo, zero, kr).astype(BF16)
    o_ref[:, 256:384] = jnp.where(lo, kr, zero).astype(BF16)
    o_ref[:, 384:512] = jnp.where(lo, zero, k).astype(BF16)


def _in_qkv_kernel(*refs, use_rope):
    if use_rope:
        (x_ref, mod_ref, g_ref, w_ref, bd_ref, qg_ref, kg_ref, cos_ref, sin_ref,
         qa_ref, qw_ref, kxa_ref, vxa_ref, kxw_ref, vxw_ref, kvf_ref) = refs
    else:
        (x_ref, mod_ref, g_ref, w_ref, bd_ref, qg_ref, kg_ref,
         qa_ref, qw_ref, kxa_ref, vxa_ref, kxw_ref, vxw_ref, kvf_ref) = refs
    h = _normmod(x_ref[...], g_ref[...], mod_ref[0, 1:2, :], mod_ref[0, 0:1, :]).astype(BF16)
    acc = _dot(h, w_ref[...])
    aq = acc[:, 0:512]
    ak = acc[:, 512:640]
    av = acc[:, 640:768]
    wq = acc[:, 768:1280]
    wk = acc[:, 1280:1408]
    wv = acc[:, 1408:1536]
    bd = bd_ref[...]
    aq = aq * lax.rsqrt(_seg_mean_sq(aq, bd) + EPS) * qg_ref[...]
    ak = ak * lax.rsqrt(_seg_mean_sq(ak, bd[0:128, 0:128]) + EPS) * kg_ref[...]
    if use_rope:
        cos = cos_ref[...]
        sin = sin_ref[...]
        aq = _rope(aq, cos, sin)
        wq = _rope(wq, cos, sin)
        ak = _rope(ak, cos[:, 0:128], sin[:, 0:128])
        wk = _rope(wk, cos[:, 0:128], sin[:, 0:128])
    scale = HEAD_DIM ** -0.5
    qa_ref[...] = (aq * scale).astype(BF16)
    qw_ref[...] = (wq * scale).astype(BF16)
    _store_head_padded(kxa_ref, ak)
    _store_head_padded(vxa_ref, av)
    _store_head_padded(kxw_ref, wk)
    _store_head_padded(vxw_ref, wv)
    kvf_ref[:, 0:128] = ak
    kvf_ref[:, 128:256] = av
    kvf_ref[:, 256:384] = wk
    kvf_ref[:, 384:512] = wv


def _in_qkv(x, mod, g, w_in, seq, bd, qg8, kg2, rope_tabs, tt=512):
    t = x.shape[0]
    nb = mod.shape[0]
    tpb = seq // tt
    mod_idx = (lambda i: (i // tpb, 0, 0)) if nb > 1 else (lambda i: (0, 0, 0))
    use_rope = rope_tabs is not None
    ncol = 1536
    in_specs = [
        pl.BlockSpec((tt, D_MODEL), lambda i: (i, 0)),
        pl.BlockSpec((1, 6, D_MODEL), mod_idx),
        pl.BlockSpec((1, D_MODEL), lambda i: (0, 0)),
        pl.BlockSpec((D_MODEL, ncol), lambda i: (0, COL_QKV // ncol)),
        pl.BlockSpec((512, 512), lambda i: (0, 0)),
        pl.BlockSpec((1, 512), lambda i: (0, 0)),
        pl.BlockSpec((1, 128), lambda i: (0, 0)),
    ]
    args = [x, mod, g, w_in, bd, qg8, kg2]
    if use_rope:
        in_specs += [pl.BlockSpec((tt, 512), lambda i: (i % tpb, 0))] * 2
        args += list(rope_tabs)
    tok = pl.BlockSpec((tt, 512), lambda i: (i, 0))
    return pl.pallas_call(
        functools.partial(_in_qkv_kernel, use_rope=use_rope),
        grid=(t // tt,),
        in_specs=in_specs,
        out_specs=[tok] * 7,
        out_shape=[jax.ShapeDtypeStruct((t, 512), BF16)] * 6 + [jax.ShapeDtypeStruct((t, 512), F32)],
        compiler_params=_cparams("arbitrary"),
        name="in_qkv",
    )(*args)


def _attn_kernel(*refs, tq, seq, n_ctx, windowed):
    if n_ctx:
        sink_ref, q_ref, kx_ref, vx_ref, kc_ref, vc_ref, o_ref = refs
    else:
        sink_ref, q_ref, kx_ref, vx_ref, o_ref = refs
    i = pl.program_id(1)
    if windowed:
        span = tq + 2 * WINDOW
        start = jnp.clip(i * tq - WINDOW, 0, seq - span)
        start = pl.multiple_of(start, WINDOW)
        qpos = i * tq + lax.broadcasted_iota(jnp.int32, (tq, span), 0)
        kpos = start + lax.broadcasted_iota(jnp.int32, (tq, span), 1)
        valid = jnp.abs(kpos - qpos) <= WINDOW
    for jb in range(4):
        q2 = q_ref[:, jb * 128:(jb + 1) * 128]
        acc = None
        for e in range(2):
            head = jb * 2 + e
            v_idx = (jb // 2) * 2 + e
            cols = slice(v_idx * 128, (v_idx + 1) * 128)
            if windowed:
                k_loc = kx_ref[pl.ds(start, span), cols]
                v_loc = vx_ref[pl.ds(start, span), cols]
            else:
                k_loc = kx_ref[:, cols]
                v_loc = vx_ref[:, cols]
            s = _dot_nt(q2, k_loc)
            if windowed:
                s = jnp.where(valid, s, NEG_INF)
            sink = sink_ref[head]
            m = jnp.maximum(jnp.max(s, axis=-1, keepdims=True), sink)
            if n_ctx:
                s_c = _dot_nt(q2, kc_ref[:, cols])
                m = jnp.maximum(m, jnp.max(s_c, axis=-1, keepdims=True))
            p = jnp.exp(s - m)
            l = jnp.sum(p, axis=-1, keepdims=True) + jnp.exp(sink - m)
            o = _dot(p.astype(BF16), v_loc)
            if n_ctx:
                p_c = jnp.exp(s_c - m)
                l = l + jnp.sum(p_c, axis=-1, keepdims=True)
                o = o + _dot(p_c.astype(BF16), vc_ref[:, cols])
            o = o / l
            acc = o if acc is None else acc + o
        o_ref[:, jb * 128:(jb + 1) * 128] = acc.astype(o_ref.dtype)


def _attention(q, kx, vx, sink, batch, seq, ctx=None, windowed=False):
    tq = min(seq, ATTN_Q_ROWS)
    assert not windowed or seq >= tq + 2 * WINDOW, "the banded key span must fit inside one sequence"
    n_ctx = 0 if ctx is None else ctx[0].shape[0] // batch
    nq = seq // tq
    in_specs = [
        pl.BlockSpec(memory_space=pltpu.SMEM),
        pl.BlockSpec((tq, 512), lambda b, i: (b * nq + i, 0)),
        pl.BlockSpec((seq, 512), lambda b, i: (b, 0)),
        pl.BlockSpec((seq, 512), lambda b, i: (b, 0)),
    ]
    args = [sink, q, kx, vx]
    if n_ctx:
        in_specs += [pl.BlockSpec((n_ctx, 512), lambda b, i: (b, 0))] * 2
        args += list(ctx)
    return pl.pallas_call(
        functools.partial(_attn_kernel, tq=tq, seq=seq, n_ctx=n_ctx, windowed=windowed),
        grid=(batch, nq),
        in_specs=in_specs,
        out_specs=pl.BlockSpec((tq, 512), lambda b, i: (b * nq + i, 0)),
        out_shape=jax.ShapeDtypeStruct((batch * seq, 512), BF16),
        compiler_params=_cparams("arbitrary", "arbitrary"),
        name="attn_win" if windowed else "attn",
    )(*args)


def _fourier_kernel(x_ref, mod_ref, g_ref, w_ref, ccs_ref, dft_ref, o_ref, z_ref, *, seq, scale):
    @pl.when(pl.program_id(1) == 0)
    def _():
        rows = min(seq, PROJ_ROWS)

        def body(r, carry):
            base = pl.multiple_of(r * rows, rows)
            h = _normmod(x_ref[pl.ds(base, rows), :], g_ref[...], mod_ref[0, 1:2, :], mod_ref[0, 0:1, :])
            f = _dot(h.astype(BF16), w_ref[...]).astype(BF16)
            z = _dot(f, ccs_ref[...])
            z_ref[pl.ds(base, rows), :] = z[:, 0:FNET_WIDTH].astype(BF16)
            z_ref[pl.ds(seq + base, rows), :] = z[:, FNET_WIDTH:].astype(BF16)
            return carry

        lax.fori_loop(0, seq // rows, body, 0)

    o_ref[...] = (_dot(dft_ref[...], z_ref[...]) * scale).astype(o_ref.dtype)


def _fourier(x, mod, g, w_in, ccs, dft, batch, seq, tr):
    nr = seq // tr
    nb = mod.shape[0]
    mod_idx = (lambda b, i: (b, 0, 0)) if nb > 1 else (lambda b, i: (0, 0, 0))
    return pl.pallas_call(
        functools.partial(_fourier_kernel, seq=seq, scale=float((seq * FNET_GROUP_DIM) ** -0.5)),
        grid=(batch, nr),
        in_specs=[
            pl.BlockSpec((seq, D_MODEL), lambda b, i: (b, 0)),
            pl.BlockSpec((1, 6, D_MODEL), mod_idx),
            pl.BlockSpec((1, D_MODEL), lambda b, i: (0, 0)),
            pl.BlockSpec((D_MODEL, FNET_WIDTH), lambda b, i: (0, COL_F // FNET_WIDTH)),
            pl.BlockSpec((FNET_WIDTH, 2 * FNET_WIDTH), lambda b, i: (0, 0)),
            pl.BlockSpec((tr, 2 * seq), lambda b, i: (i, 0)),
        ],
        out_specs=pl.BlockSpec((tr, FNET_WIDTH), lambda b, i: (b * nr + i, 0)),
        out_shape=jax.ShapeDtypeStruct((batch * seq, FNET_WIDTH), BF16),
        scratch_shapes=[pltpu.VMEM((2 * seq, FNET_WIDTH), BF16)],
        compiler_params=_cparams("arbitrary", "arbitrary"),
        name="fourier",
    )(x, mod, g, w_in, ccs, dft)


def _dft_tables(seq):
    k = np.arange(seq, dtype=np.int64)
    ang = 2.0 * np.pi * ((k[:, None] * k[None, :]) % seq).astype(np.float64) / seq
    return np.concatenate([np.cos(ang), -np.sin(ang)], axis=1).astype(np.float32)


def _channel_dft_tables():
    g = FNET_GROUP_DIM
    k = np.arange(g, dtype=np.int64)
    ang = 2.0 * np.pi * ((k[:, None] * k[None, :]) % g).astype(np.float64) / g
    eye = np.eye(FNET_WIDTH // g)
    return np.concatenate([np.kron(eye, np.cos(ang)), np.kron(eye, np.sin(ang))], axis=1).astype(np.float32)


CONV_PAD = 16
CONV_ROWS = 128
PROJ_ROWS = 512


def _conv_kernel(x_ref, mod_ref, g_ref, wa_ref, wg_ref, dw_ref, b_ref, lg_ref, lb_ref, o_ref, hp_ref, *, seq):
    zeros = jnp.zeros((CONV_PAD, CONV_WIDTH), F32)
    hp_ref[0:CONV_PAD, :] = zeros
    hp_ref[seq + CONV_PAD:seq + 2 * CONV_PAD, :] = zeros

    rows = min(seq, PROJ_ROWS)

    def glu_body(r, carry):
        base = pl.multiple_of(r * rows, rows)
        h = _normmod(x_ref[pl.ds(base, rows), :], g_ref[...], mod_ref[0, 1:2, :], mod_ref[0, 0:1, :])
        h = h.astype(BF16)
        hp_ref[pl.ds(base + CONV_PAD, rows), :] = _dot(h, wa_ref[...]) * jax.nn.sigmoid(_dot(h, wg_ref[...]))
        return carry

    lax.fori_loop(0, seq // rows, glu_body, 0)

    blk_rows = CONV_ROWS + 2 * CONV_PAD

    def conv_body(r, carry):
        base = pl.multiple_of(r * CONV_ROWS, CONV_ROWS)
        outs = []
        for cch in range(CONV_WIDTH // LANES):
            ls = slice(cch * LANES, (cch + 1) * LANES)
            blk = hp_ref[pl.ds(base, blk_rows), ls]
            acc = jnp.zeros((CONV_ROWS, LANES), F32)
            for s in range(8):
                blk_s = blk if s == 0 else pltpu.roll(blk, blk_rows - s, axis=0)
                for m in range(4):
                    off = 8 * m + s
                    k = off - 1
                    if 0 <= k < CONV_K:
                        acc = acc + blk_s[8 * m:8 * m + CONV_ROWS, :] * dw_ref[k:k + 1, ls]
            outs.append(acc)
        y = jnp.concatenate(outs, axis=1) + b_ref[...]
        mu = jnp.mean(y, axis=-1, keepdims=True)
        yc = y - mu
        var = jnp.mean(yc * yc, axis=-1, keepdims=True)
        hn = yc * lax.rsqrt(var + EPS) * lg_ref[...] + lb_ref[...]
        o_ref[pl.ds(base, CONV_ROWS), :] = (hn * jax.nn.sigmoid(hn)).astype(o_ref.dtype)
        return carry

    lax.fori_loop(0, seq // CONV_ROWS, conv_body, 0)


def _conv(x, mod, g, w_in, dw, b, lg, lb, batch, seq):
    vec = pl.BlockSpec((1, CONV_WIDTH), lambda bi: (0, 0))
    nb = mod.shape[0]
    mod_idx = (lambda bi: (bi, 0, 0)) if nb > 1 else (lambda bi: (0, 0, 0))
    return pl.pallas_call(
        functools.partial(_conv_kernel, seq=seq),
        grid=(batch,),
        in_specs=[
            pl.BlockSpec((seq, D_MODEL), lambda bi: (bi, 0)),
            pl.BlockSpec((1, 6, D_MODEL), mod_idx),
            pl.BlockSpec((1, D_MODEL), lambda bi: (0, 0)),
            pl.BlockSpec((D_MODEL, CONV_WIDTH), lambda bi: (0, COL_C // CONV_WIDTH)),
            pl.BlockSpec((D_MODEL, CONV_WIDTH), lambda bi: (0, COL_C // CONV_WIDTH + 1)),
            pl.BlockSpec((CONV_K, CONV_WIDTH), lambda bi: (0, 0)),
            vec, vec, vec,
        ],
        out_specs=pl.BlockSpec((seq, CONV_WIDTH), lambda bi: (bi, 0)),
        out_shape=jax.ShapeDtypeStruct((batch * seq, CONV_WIDTH), BF16),
        scratch_shapes=[pltpu.VMEM((seq + 2 * CONV_PAD, CONV_WIDTH), F32)],
        compiler_params=_cparams("arbitrary"),
        name="conv",
    )(x, mod, g, w_in, w_in, dw, b, lg, lb)


def _out_kernel(f_ref, c_ref, a_ref, w_ref, x_ref, mod_ref, bg_ref, g1_ref, g2_ref,
                wg0_ref, wg1_ref, wg2_ref, wg3_ref, wf_ref, wc_ref, wa_ref, ww_ref, wo_ref, x1_ref, h2t_ref):
    x = x_ref[...]
    h1 = _normmod(x, g1_ref[...], mod_ref[0, 1:2, :], mod_ref[0, 0:1, :]).astype(BF16)
    merged = None
    branches = ((f_ref, wf_ref, wg0_ref), (c_ref, wc_ref, wg1_ref), (a_ref, wa_ref, wg2_ref), (w_ref, ww_ref, wg3_ref))
    for i, (br, wt, wg) in enumerate(branches):
        y = _dot(br[...], wt[...])
        gate = jax.nn.sigmoid(_dot(h1, wg[...]) + bg_ref[i:i + 1, :])
        merged = gate * y if merged is None else merged + gate * y
    mix = _dot(merged.astype(BF16), wo_ref[...])
    x1 = x + mod_ref[0, 2:3, :] * mix
    x1_ref[...] = x1
    h2 = _normmod(x1, g2_ref[...], mod_ref[0, 4:5, :], mod_ref[0, 3:4, :])
    h2t_ref[...] = h2.T.astype(BF16)


def _out_proj(f, c, a, w, x, mod, bg, g1, g2, w_in, wf, wc, wa, ww, wo, seq, tt=256):
    t = x.shape[0]
    nb = mod.shape[0]
    tpb = seq // tt
    mod_idx = (lambda i: (i // tpb, 0, 0)) if nb > 1 else (lambda i: (0, 0, 0))
    br = pl.BlockSpec((tt, 512), lambda i: (i, 0))
    wbr = pl.BlockSpec((512, D_MODEL), lambda i: (0, 0))
    vec = pl.BlockSpec((1, D_MODEL), lambda i: (0, 0))
    gl0 = COL_GL // D_MODEL
    wgl = [pl.BlockSpec((D_MODEL, D_MODEL), functools.partial(lambda i, j: (0, j), j=gl0 + k)) for k in range(N_BRANCH)]
    return pl.pallas_call(
        _out_kernel,
        grid=(t // tt,),
        in_specs=[
            br, br, br, br,
            pl.BlockSpec((tt, D_MODEL), lambda i: (i, 0)),
            pl.BlockSpec((1, 6, D_MODEL), mod_idx),
            pl.BlockSpec((N_BRANCH, D_MODEL), lambda i: (0, 0)),
            vec, vec,
            *wgl,
            wbr, wbr, wbr, wbr,
            pl.BlockSpec((D_MODEL, D_MODEL), lambda i: (0, 0)),
        ],
        out_specs=[
            pl.BlockSpec((tt, D_MODEL), lambda i: (i, 0)),
            pl.BlockSpec((D_MODEL, tt), lambda i: (0, i)),
        ],
        out_shape=[jax.ShapeDtypeStruct((t, D_MODEL), F32), jax.ShapeDtypeStruct((D_MODEL, t), BF16)],
        compiler_params=_cparams("arbitrary"),
        name="out_proj",
    )(f, c, a, w, x, mod, bg, g1, g2, w_in, w_in, w_in, w_in, wf, wc, wa, ww, wo)


NOT_TOP = 99.0


def _bf16_twice(x):
    bits = pltpu.bitcast(x.astype(BF16).astype(F32), jnp.uint32)
    return bits | (bits >> 16)


def _route_kernel(h2t_ref, wq_ref, sk_ref, a0_ref, n0_ref, a1_ref, r1_ref,
                  qt_ref, sc_ref, srt_ref, rnk_ref, flag_ref, *, tt):
    ncol = tt // LANES
    qt_ref[...] = _dot(wq_ref[...], h2t_ref[...]).astype(BF16)
    for hp in range(2 * PEER_HEADS):
        sc_ref[hp] = _dot(sk_ref[hp % 2], qt_ref[hp * PEER_HALF:(hp + 1) * PEER_HALF, :])

    key_iota = lax.broadcasted_iota(jnp.int32, (N_KEYS, LANES), 0).astype(F32)

    sub8 = lax.broadcasted_iota(jnp.int32, (8, LANES), 0).astype(F32)

    def _tree(op, xs):
        xs = list(xs)
        while len(xs) > 1:
            xs = [op(xs[i], xs[i + 1]) if i + 1 < len(xs) else xs[i] for i in range(0, len(xs), 2)]
        return xs[0]

    def note_ties(count):
        flag_ref[...] = jnp.maximum(flag_ref[...], jnp.where(count != float(PEER_TOPK), 1.0, 0.0))

    def top16(s, exact):
        rank = jnp.full((N_KEYS, LANES), NOT_TOP, F32)
        tops = []
        for it in range(PEER_TOPK):
            m = jnp.max(s, axis=0, keepdims=True)
            if exact:
                first = jnp.min(jnp.where(s == m, key_iota, float(N_KEYS)), axis=0, keepdims=True)
                hit = key_iota == first
            else:
                hit = s == m
            s = jnp.where(hit, -jnp.inf, s)
            rank = jnp.where(hit, float(it), rank)
            tops.append(m)
        if not exact:
            note_ties(jnp.sum(jnp.where(rank < float(PEER_TOPK), 1.0, 0.0), axis=0, keepdims=True))
        return jnp.concatenate(tops, axis=0), rank

    def combine(h, ls, exact):
        s0 = srt_ref[2 * h, :, ls]
        s1 = srt_ref[2 * h + 1, :, ls]
        cands = [s0[0:1, :] + s1[0:8, :], s0[0:1, :] + s1[8:16, :]]
        poss = [sub8, sub8 + 8.0]
        for a in range(1, 8):
            c = s0[a:a + 1, :] + s1[0:8, :]
            cands.append(jnp.where(sub8 < float(16 // (a + 1)), c, -jnp.inf))
            poss.append(sub8 + float(16 * a))
        cands.append(s0[8:16, :] + s1[0:1, :])
        poss.append((sub8 + 8.0) * 16.0)
        sels = [jnp.zeros_like(c) for c in cands]
        tops = []
        for it in range(PEER_TOPK):
            m = jnp.max(_tree(jnp.maximum, cands), axis=0, keepdims=True)
            if exact:
                firsts = [jnp.where(c == m, p, 999.0) for c, p in zip(cands, poss)]
                first = jnp.min(_tree(jnp.minimum, firsts), axis=0, keepdims=True)
            for j in range(len(cands)):
                hit = (poss[j] == first) if exact else (cands[j] == m)
                cands[j] = jnp.where(hit, -jnp.inf, cands[j])
                sels[j] = jnp.where(hit, 1.0, sels[j])
            tops.append(m)
        z = _tree(jnp.add, [jnp.exp(m - tops[0]) for m in tops])
        inv_z = 1.0 / z
        n_rows = [jnp.sum(sels[0] + sels[1], axis=0, keepdims=True)]
        n_rows += [jnp.sum(sels[a + 1], axis=0, keepdims=True) for a in range(1, 8)]
        n_hi = sels[9]
        if not exact:
            note_ties(_tree(jnp.add, n_rows) + jnp.sum(n_hi, axis=0, keepdims=True))
        rank0 = rnk_ref[2 * h, :, ls]
        rank1 = rnk_ref[2 * h + 1, :, ls]
        n0 = jnp.zeros((N_KEYS, LANES), F32)
        for a in range(8):
            n0 = jnp.where(rank0 == float(a), n_rows[a], n0)
            n0 = jnp.where(rank0 == float(a + 8), n_hi[a:a + 1, :], n0)
        sc0 = sc_ref[2 * h, :, ls]
        sc1 = sc_ref[2 * h + 1, :, ls]
        a0 = jnp.where(rank0 < float(PEER_TOPK), jnp.exp(sc0 - s0[0:1, :]) * inv_z, 0.0)
        a1 = jnp.where(rank1 < float(PEER_TOPK), jnp.exp(sc1 - s1[0:1, :]), 0.0)
        a0_ref[h, :, ls] = _bf16_twice(a0)
        n0_ref[h, :, ls] = _bf16_twice(n0)
        a1_ref[h, :, ls] = pltpu.bitcast(a1.astype(BF16), jnp.uint32)
        r1_ref[h, :, ls] = pltpu.bitcast(rank1.astype(BF16), jnp.uint32)

    def stage1(hp, carry, exact):
        for col in range(ncol):
            ls = slice(col * LANES, (col + 1) * LANES)
            tops, rank = top16(sc_ref[hp, :, ls], exact)
            srt_ref[hp, :, ls] = tops
            rnk_ref[hp, :, ls] = rank
        return carry

    def stage2(h, carry):
        flag_ref[...] = jnp.zeros_like(flag_ref)
        for col in range(ncol):
            combine(h, slice(col * LANES, (col + 1) * LANES), False)

        @pl.when(jnp.max(flag_ref[...]) > 0.0)
        def _():
            for col in range(ncol):
                combine(h, slice(col * LANES, (col + 1) * LANES), True)

        return carry

    flag_ref[...] = jnp.zeros_like(flag_ref)
    lax.fori_loop(0, 2 * PEER_HEADS, functools.partial(stage1, exact=False), 0)

    @pl.when(jnp.max(flag_ref[...]) > 0.0)
    def _():
        lax.fori_loop(0, 2 * PEER_HEADS, functools.partial(stage1, exact=True), 0)

    lax.fori_loop(0, PEER_HEADS, stage2, 0)


def _route(h2t, wq_t, sk, tt=512):
    t = h2t.shape[1]
    out0 = pl.BlockSpec((PEER_HEADS, N_KEYS, tt), lambda i: (0, 0, i))
    out1 = pl.BlockSpec((PEER_HEADS, N_KEYS // 2, tt), lambda i: (0, 0, i))
    shp0 = jax.ShapeDtypeStruct((PEER_HEADS, N_KEYS, t), jnp.uint32)
    shp1 = jax.ShapeDtypeStruct((PEER_HEADS, N_KEYS // 2, t), jnp.uint32)
    return pl.pallas_call(
        functools.partial(_route_kernel, tt=tt),
        grid=(t // tt,),
        in_specs=[
            pl.BlockSpec((D_MODEL, tt), lambda i: (0, i)),
            pl.BlockSpec((2 * PEER_HEADS * PEER_HALF, D_MODEL), lambda i: (0, 0)),
            pl.BlockSpec((2, N_KEYS, PEER_HALF), lambda i: (0, 0, 0)),
        ],
        out_specs=[out0, out0, out1, out1],
        out_shape=[shp0, shp0, shp1, shp1],
        scratch_shapes=[
            pltpu.VMEM((2 * PEER_HEADS * PEER_HALF, tt), BF16),
            pltpu.VMEM((2 * PEER_HEADS, N_KEYS, tt), F32),
            pltpu.VMEM((2 * PEER_HEADS, PEER_TOPK, tt), F32),
            pltpu.VMEM((2 * PEER_HEADS, N_KEYS, tt), F32),
            pltpu.VMEM((8, LANES), F32),
        ],
        compiler_params=_cparams("arbitrary"),
        name="peer_route",
    )(h2t, wq_t, sk)


def _gelu_tanh(x):
    c = 2.0 * 0.7978845608028654
    neg_u = x * ((-c * 0.044715) * (x * x) - c)
    return x / (1.0 + jnp.exp(neg_u))


def _peer_step(h2t_ref, a0_ref, n0_ref, a1_ref, r1_ref, u_ref, v_ref, acc_ref,
               s_new, s_cur, w_cur, w_done, i0, *, tt, n_i):
    ncol = tt // LANES
    zero = jnp.zeros((PACK_ROWS, LANES), BF16)
    rows_per_piece = MXU_PIECE_ROWS // N_KEYS
    n_out = D_MODEL // MXU_PIECE_ROWS
    out_at = {(2 * p + 1) * n_i // (2 * n_out): p for p in range(n_out)}
    for il in range(n_i):
        if il % rows_per_piece == 0:
            mr = slice(il * N_KEYS, il * N_KEYS + MXU_PIECE_ROWS)
            s_new[mr, :] = _dot(u_ref[mr, :], h2t_ref[...])
        for col in range(ncol):
            if il in out_at and col == 0:
                dc = slice(out_at[il] * MXU_PIECE_ROWS, (out_at[il] + 1) * MXU_PIECE_ROWS)
                acc_ref[:, dc] += lax.dot_general(w_done[...], v_ref[:, dc], (((0,), (0,)), ((), ())),
                                                  preferred_element_type=F32)
            ls = slice(col * LANES, (col + 1) * LANES)
            n0 = [pltpu.bitcast(jnp.broadcast_to(n0_ref[h, pl.ds(i0, n_i), ls][il:il + 1, :], (8, LANES)), BF16)
                  for h in range(PEER_HEADS)]
            a0 = [pltpu.bitcast(jnp.broadcast_to(a0_ref[h, pl.ds(i0, n_i), ls][il:il + 1, :], (8, LANES)), BF16)
                  for h in range(PEER_HEADS)]
            for k in range(N_KEYS // PACK_ROWS):
                pr = slice(k * PACK_ROWS // 2, (k + 1) * PACK_ROWS // 2)
                gate = None
                for h in range(PEER_HEADS):
                    r1 = pltpu.bitcast(r1_ref[h, pr, ls], BF16)
                    a1 = pltpu.bitcast(a1_ref[h, pr, ls], BF16)
                    term = jnp.where(r1 < n0[h], a0[h] * a1, zero)
                    gate = term if gate is None else gate + term
                rows = slice(il * N_KEYS + k * PACK_ROWS, il * N_KEYS + (k + 1) * PACK_ROWS)
                act = _gelu_tanh(s_cur[rows, ls])
                w_cur[rows, ls] = gate * act.astype(BF16)


def _peer_kernel(h2t_ref, a0_ref, n0_ref, a1_ref, r1_ref, u_ref, v_ref, x1_ref, mod_ref, fg_ref,
                 o_ref, s0_ref, s1_ref, w0_ref, w1_ref, acc_ref, *, tt, ec, ne, total, final):
    g = pl.program_id(0)
    n_i = ec // N_KEYS
    assert n_i % 8 == 0, "an expert chunk must cover whole sublane tiles of half-0 keys"
    g_gate = jnp.clip(g - 1, 0, total - 1)
    c_out = jnp.clip(g - 2, 0, total - 1) % ne
    i0 = pl.multiple_of((g_gate % ne) * n_i, 8)

    @pl.when(g == 0)
    def _():
        s1_ref[...] = jnp.zeros_like(s1_ref)
        w0_ref[...] = jnp.zeros_like(w0_ref)
        w1_ref[...] = jnp.zeros_like(w1_ref)

    @pl.when(c_out == 0)
    def _():
        acc_ref[...] = jnp.zeros_like(acc_ref)

    step = functools.partial(_peer_step, h2t_ref, a0_ref, n0_ref, a1_ref, r1_ref, u_ref, v_ref, acc_ref,
                             tt=tt, n_i=n_i)

    @pl.when(g % 2 == 0)
    def _():
        step(s0_ref, s1_ref, w1_ref, w0_ref, i0)

    @pl.when(g % 2 == 1)
    def _():
        step(s1_ref, s0_ref, w0_ref, w1_ref, i0)

    @pl.when((g >= 2) & (c_out == ne - 1))
    def _():
        y = x1_ref[...] + mod_ref[0, 5:6, :] * acc_ref[...]
        if final:
            y = y * lax.rsqrt(jnp.mean(y * y, axis=-1, keepdims=True) + EPS) * fg_ref[...]
        o_ref[...] = y


def _peer(h2t, a0, n0, a1, r1, u, v, x1, mod, final_g, final, seq, tt=512, ec=2048):
    t = x1.shape[0]
    nb = mod.shape[0]
    tpb = seq // tt
    ne = N_EXPERTS // ec
    total = (t // tt) * ne

    def st_a(g):
        return jnp.minimum(g, total - 1)

    def st_b(g):
        return jnp.clip(g - 1, 0, total - 1)

    def st_c(g):
        return jnp.clip(g - 2, 0, total - 1)

    mod_idx = (lambda g: (st_c(g) // ne // tpb, 0, 0)) if nb > 1 else (lambda g: (0, 0, 0))
    rt0 = pl.BlockSpec((PEER_HEADS, N_KEYS, tt), lambda g: (0, 0, st_b(g) // ne))
    rt1 = pl.BlockSpec((PEER_HEADS, N_KEYS // 2, tt), lambda g: (0, 0, st_b(g) // ne))
    return pl.pallas_call(
        functools.partial(_peer_kernel, tt=tt, ec=ec, ne=ne, total=total, final=final),
        grid=(total + 2,),
        in_specs=[
            pl.BlockSpec((D_MODEL, tt), lambda g: (0, st_a(g) // ne)),
            rt0, rt0, rt1, rt1,
            pl.BlockSpec((ec, D_MODEL), lambda g: (st_a(g) % ne, 0)),
            pl.BlockSpec((ec, D_MODEL), lambda g: (st_c(g) % ne, 0)),
            pl.BlockSpec((tt, D_MODEL), lambda g: (st_c(g) // ne, 0)),
            pl.BlockSpec((1, 6, D_MODEL), mod_idx),
            pl.BlockSpec((1, D_MODEL), lambda g: (0, 0)),
        ],
        out_specs=pl.BlockSpec((tt, D_MODEL), lambda g: (st_c(g) // ne, 0)),
        out_shape=jax.ShapeDtypeStruct((t, D_MODEL), F32),
        scratch_shapes=[
            pltpu.VMEM((ec, tt), F32), pltpu.VMEM((ec, tt), F32),
            pltpu.VMEM((ec, tt), BF16), pltpu.VMEM((ec, tt), BF16),
            pltpu.VMEM((tt, D_MODEL), F32),
        ],
        compiler_params=_cparams("arbitrary"),
        name="peer_mix",
    )(h2t, a0, n0, a1, r1, u, v, x1, mod, final_g)


def _rope_tables(seq):
    n_rows = seq // GRID_W
    row = jnp.repeat(jnp.arange(n_rows), GRID_W).astype(F32)
    col = jnp.tile(jnp.arange(GRID_W), n_rows).astype(F32)
    n_f = HEAD_DIM // 4
    inv = 1.0 / (ROPE_THETA ** (jnp.arange(n_f, dtype=F32) / n_f))
    ang = jnp.concatenate([row[:, None] * inv, col[:, None] * inv], axis=-1)
    cos, sin = jnp.cos(ang), jnp.sin(ang)
    cos_t = jnp.tile(jnp.concatenate([cos, cos], axis=-1), (1, N_HEADS))
    sin_t = jnp.tile(jnp.concatenate([-sin, sin], axis=-1), (1, N_HEADS))
    return cos_t, sin_t


def _pad_heads(kv):
    z = jnp.zeros_like(kv[:, :HEAD_DIM])
    h0, h1 = kv[:, :HEAD_DIM], kv[:, HEAD_DIM:]
    return jnp.concatenate([h0, z, z, h0, h1, z, z, h1], axis=-1).astype(BF16)


def _layer(x, mod, lp, consts, batch, seq, ctx_kv):
    is_ctx = ctx_kv is None
    g1 = lp["norm1_g"]
    w_in = lp["w_in"]
    qa, qw, kxa, vxa, kxw, vxw, kvf = _in_qkv(
        x, mod, g1, w_in, seq, consts["bd"], lp["q_norm_g"], lp["k_norm_g"],
        None if is_ctx else consts["rope"])
    dft = consts["dft_ctx"] if is_ctx else consts["dft_lat"]
    f = _fourier(x, mod, g1, w_in, consts["ccs"], dft, batch, seq, min(seq, FOURIER_ROWS))
    cv = _conv(x, mod, g1, w_in, lp["conv_dw"], lp["conv_b"], lp["conv_ln_g"], lp["conv_ln_b"], batch, seq)
    no_sink = jnp.full((N_HEADS,), NEG_INF, F32)
    if is_ctx:
        a = _attention(qa, kxa, vxa, no_sink, batch, seq)
        w = _attention(qw, kxw, vxw, lp["win_sink"], batch, seq)
    else:
        ka_c, va_c, kw_c, vw_c = ctx_kv
        a = _attention(qa, kxa, vxa, no_sink, batch, seq, ctx=(ka_c, va_c))
        w = _attention(qw, kxw, vxw, lp["win_sink"], batch, seq, ctx=(kw_c, vw_c), windowed=True)
    x1, h2t = _out_proj(f, cv, a, w, x, mod, lp["b_gate"], g1, lp["norm2_g"], w_in,
                        lp["w_fourier_out"], lp["w_conv_out"], lp["w_attn_out"], lp["w_win_out"],
                        lp["w_out"], seq)
    a0, n0, a1, r1 = _route(h2t, lp["w_peer_q_t"], lp["peer_subkeys"])
    x2 = _peer(h2t, a0, n0, a1, r1, lp["peer_u"], lp["peer_v"], x1, mod, lp["final_norm_g"], lp["is_last"], seq)
    return x2, kvf


def kernel(x_prompt, x_sample, cache_attn_k, cache_attn_v, cache_win_k, cache_win_v, c, c_ctx, norm1_g, norm2_g, w_ada, b_ada, w_in, b_gate, w_fourier_out, conv_dw, conv_b, conv_ln_g, conv_ln_b, w_conv_out, q_norm_g, k_norm_g, w_attn_out, win_sink, w_win_out, w_out, w_peer_q, peer_subkeys, peer_u, peer_v, final_norm_g):
    b_ctx, s_ctx, _ = x_prompt.shape
    b_lat, s_lat, _ = x_sample.shape
    past = cache_attn_k.shape[2]

    cond = jnp.concatenate([c_ctx[None, :], c, jnp.zeros((16 - 1 - b_lat, D_MODEL), F32)], axis=0)
    ada = _ada_all(cond, w_ada, b_ada).reshape(DEPTH, 16, 6, D_MODEL)

    consts = {
        "bd": jnp.asarray(np.kron(np.eye(N_HEADS), np.ones((HEAD_DIM, HEAD_DIM))), BF16),
        "ccs": jnp.asarray(_channel_dft_tables()).astype(BF16),
        "dft_ctx": jnp.asarray(_dft_tables(s_ctx)).astype(BF16),
        "dft_lat": jnp.asarray(_dft_tables(s_lat)).astype(BF16),
        "rope": _rope_tables(s_lat),
    }

    xp = x_prompt.reshape(b_ctx * s_ctx, D_MODEL)
    xs = x_sample.reshape(b_lat * s_lat, D_MODEL)
    kv_out = []
    for l in range(DEPTH):
        lp = {
            "norm1_g": norm1_g[l][None, :], "norm2_g": norm2_g[l][None, :],
            "w_in": w_in[l].astype(BF16), "b_gate": b_gate[l],
            "w_fourier_out": w_fourier_out[l].astype(BF16),
            "conv_dw": conv_dw[l], "conv_b": conv_b[l][None, :],
            "conv_ln_g": conv_ln_g[l][None, :], "conv_ln_b": conv_ln_b[l][None, :],
            "w_conv_out": w_conv_out[l].astype(BF16),
            "q_norm_g": jnp.tile(q_norm_g[l], N_HEADS)[None, :],
            "k_norm_g": jnp.tile(k_norm_g[l], 2)[None, :],
            "w_attn_out": w_attn_out[l].astype(BF16), "win_sink": win_sink[l],
            "w_win_out": w_win_out[l].astype(BF16), "w_out": w_out[l].astype(BF16),
            "w_peer_q_t": w_peer_q[l].T.astype(BF16),
            "peer_subkeys": peer_subkeys[l].astype(BF16),
            "peer_u": peer_u[l].astype(BF16),
            "peer_v": peer_v[l].astype(BF16),
            "final_norm_g": final_norm_g[None, :], "is_last": l == DEPTH - 1,
        }
        mod_ctx = ada[l, 0:1]
        mod_lat = ada[l, 1:1 + b_lat]
        xp, kvf = _layer(xp, mod_ctx, lp, consts, b_ctx, s_ctx, None)
        kv_out.append(kvf)
        cached = tuple(
            _pad_heads(cc[:, l].reshape(b_lat * past, ATTN_KV))
            for cc in (cache_attn_k, cache_attn_v, cache_win_k, cache_win_v))
        xs, _ = _layer(xs, mod_lat, lp, consts, b_lat, s_lat, cached)

    y_prompt = xp.reshape(b_ctx, s_ctx, D_MODEL)
    y_sample = xs.reshape(b_lat, s_lat, D_MODEL)
    outs = []
    for j in range(4):
        per_layer = [kvf[:, j * 128:(j + 1) * 128].reshape(b_ctx, s_ctx, 2, HEAD_DIM) for kvf in kv_out]
        outs.append(jnp.stack(per_layer, axis=1))
    return (y_prompt, y_sample, outs[0], outs[1], outs[2], outs[3])
```
